```python
import jax, jax.numpy as jnp
from jax import lax
import numpy as np

D_MODEL = 2048
BATCH = 8
SEQ = 2048
DEPTH = 2

CHUNK = 64
NORM_EPS = 1e-6
D_FF = 5632
FFN_RES = 0.5
MIX_WIDTH = D_MODEL
DN_HEADS = 8
DN_HEAD_DIM = 128
DN_WIDTH = DN_HEADS * DN_HEAD_DIM
DN_CONV = 4
S5_WIDTH = MIX_WIDTH - DN_WIDTH
S5_GROUP = 16
S5_GROUPS = S5_WIDTH // S5_GROUP
S5_STATE = 64
S5_DT_MIN = 1e-3
S5_DT_MAX = 1e-1
GLA_HEADS = 4
GLA_WIDTH = MIX_WIDTH // 2
GLA_KEY_WIDTH = GLA_WIDTH // 2
GLA_HEAD_K = GLA_KEY_WIDTH // GLA_HEADS
GLA_HEAD_V = GLA_WIDTH // GLA_HEADS
GLA_GATE_RANK = 16
GLA_TAU = 16.0
HG_WIDTH = MIX_WIDTH - GLA_WIDTH
HG_EXPAND = 128
HG_HEADS = HG_WIDTH // HG_EXPAND
EVEN_SPLITS = (3 * DN_WIDTH, DN_WIDTH, DN_HEADS, DN_HEADS, S5_WIDTH)
ODD_SPLITS = (GLA_KEY_WIDTH, GLA_KEY_WIDTH, GLA_WIDTH, GLA_WIDTH, GLA_GATE_RANK, HG_WIDTH, HG_WIDTH, HG_WIDTH, HG_WIDTH)

kernel_name = 'hybrid_deltanet_s5_gla_hgrn2_macaron'

F32 = jnp.float32


def rmsnorm(x, w):
    xf = x.astype(F32)
    y = xf * lax.rsqrt(jnp.mean(xf * xf, axis=-1, keepdims=True) + NORM_EPS)
    return (y * w.astype(F32)).astype(x.dtype)


def l2norm(x):
    return x * lax.rsqrt(jnp.sum(x * x, axis=-1, keepdims=True) + NORM_EPS)


def swiglu(x, w_gate, w_up, w_down):
    return (jax.nn.silu(x @ w_gate) * (x @ w_up)) @ w_down


def split_cols(t, sizes):
    return jnp.split(t, [int(i) for i in np.cumsum(sizes)[:-1]], axis=-1)


def causal_dwconv(x, w):
    return lax.conv_general_dilated(x, w.astype(x.dtype), window_strides=(1,), padding=[(w.shape[0] - 1, 0)], dimension_numbers=('NWC', 'WIO', 'NWC'), feature_group_count=x.shape[-1])


def to_chunks(t):
    b, s, h, d = t.shape
    return t.reshape(b, s // CHUNK, CHUNK, h, d).transpose(1, 0, 3, 2, 4)


def from_chunks(t):
    n, b, h, c, d = t.shape
    return t.transpose(1, 0, 3, 2, 4).reshape(b, n * c, h, d)


def causal_mask(strict=False):
    return jnp.tril(jnp.ones((CHUNK, CHUNK), dtype=bool), -1 if strict else 0)


def gated_delta_rule(q, k, v, g, beta):
    q, k, v = (to_chunks(t.astype(F32)) for t in (q, k, v))
    g = to_chunks(g.astype(F32)[..., None])[..., 0]
    beta = to_chunks(beta.astype(F32)[..., None])
    gc = jnp.cumsum(g, axis=-1)
    decay = jnp.exp(jnp.where(causal_mask(), gc[..., :, None] - gc[..., None, :], -jnp.inf))
    kb = k * beta
    a = jnp.where(causal_mask(True), jnp.einsum('nbhid,nbhjd->nbhij', kb, k) * decay, 0.0)
    tmat = a + jnp.eye(CHUNK, dtype=F32)
    w = lax.linalg.triangular_solve(tmat, kb * jnp.exp(gc)[..., None], left_side=True, lower=True, unit_diagonal=True)
    u = lax.linalg.triangular_solve(tmat, v * beta, left_side=True, lower=True, unit_diagonal=True)
    qk = jnp.einsum('nbhid,nbhjd->nbhij', q, k) * decay
    q_dec = q * jnp.exp(gc)[..., None]
    k_dec = k * jnp.exp(gc[..., -1:] - gc)[..., None]
    chunk_decay = jnp.exp(gc[..., -1])[..., None, None]

    def step(state, inp):
        w_c, u_c, qk_c, qd_c, kd_c, cd_c = inp
        v_new = u_c - jnp.einsum('bhcd,bhde->bhce', w_c, state)
        o = jnp.einsum('bhcd,bhde->bhce', qd_c, state) + jnp.einsum('bhij,bhje->bhie', qk_c, v_new)
        state = cd_c * state + jnp.einsum('bhcd,bhce->bhde', kd_c, v_new)
        return state, o

    n, b, h, c, dk = q.shape
    state0 = jnp.zeros((b, h, dk, v.shape[-1]), F32)
    _, o = lax.scan(step, state0, (w, u, qk, q_dec, k_dec, chunk_decay))
    return from_chunks(o)


def chunked_gla(q, k, v, log_a):
    q, k, v, log_a = (to_chunks(t.astype(F32)) for t in (q, k, v, log_a))
    b = jnp.cumsum(log_a, axis=-2)
    b_ref = b[..., CHUNK // 2 - 1:CHUNK // 2, :]
    att = jnp.einsum('nbhid,nbhjd->nbhij', q * jnp.exp(b - b_ref), k * jnp.exp(b_ref - b))
    att = jnp.where(causal_mask(), att, 0.0)
    o_intra = jnp.einsum('nbhij,nbhje->nbhie', att, v)
    q_dec = q * jnp.exp(b)
    k_dec = k * jnp.exp(b[..., -1:, :] - b)
    chunk_decay = jnp.exp(b[..., -1, :])[..., None]

    def step(state, inp):
        qd_c, kd_c, v_c, cd_c = inp
        o = jnp.einsum('bhcd,bhde->bhce', qd_c, state)
        state = cd_c * state + jnp.einsum('bhcd,bhce->bhde', kd_c, v_c)
        return state, o

    n, bb, h, c, dk = q.shape
    state0 = jnp.zeros((bb, h, dk, v.shape[-1]), F32)
    _, o_inter = lax.scan(step, state0, (q_dec, k_dec, v, chunk_decay))
    return from_chunks(o_inter + o_intra)


def _lin_combine(e1, e2):
    a1, h1 = e1
    a2, h2 = e2
    return a1 * a2, a2 * h1 + h2


def s5_branch(u, a_re, a_im, b_re, b_im, c_re, c_im, d, log_step, glu_w, glu_b):
    bsz, s, _ = u.shape
    uf = u.astype(F32).reshape(bsz, s, S5_GROUPS, S5_GROUP)
    lam = lax.complex(a_re.astype(F32), a_im.astype(F32))
    dt = jnp.exp(log_step.astype(F32))[:, None]
    lam_bar = jnp.exp(lam * dt)
    b_bar = ((lam_bar - 1.0) / lam)[..., None] * lax.complex(b_re.astype(F32), b_im.astype(F32))
    bu = jnp.einsum('bsgh,gph->bsgp', uf.astype(jnp.complex64), b_bar)
    lam_seq = jnp.broadcast_to(lam_bar, (s,) + lam_bar.shape)
    states = jax.vmap(lambda e: lax.associative_scan(_linear_combine_wrap, (lam_seq, e))[1])(bu)
    c = lax.complex(c_re.astype(F32), c_im.astype(F32))
    y = jnp.einsum('bsgp,ghp->bsgh', states, c).real + d.astype(F32) * uf
    y = jax.nn.gelu(y.reshape(bsz, s, S5_WIDTH))
    y = y * jax.nn.sigmoid(y @ glu_w.astype(F32) + glu_b.astype(F32))
    return y.astype(u.dtype)


def _linear_combine_wrap(e1, e2):
    return _lin_combine(e1, e2)


def even_mixer(h, w_in, conv_w, a_log, dt_bias, dn_norm_w, s5_a_re, s5_a_im, s5_b_re, s5_b_im, s5_c_re, s5_c_im, s5_d, s5_log_step, s5_glu_w, s5_glu_b, w_out):
    bsz, s, _ = h.shape
    qkv, z, a_in, b_in, u = split_cols(h @ w_in, EVEN_SPLITS)
    qkv = jax.nn.silu(causal_dwconv(qkv, conv_w))
    q, k, v = (t.astype(F32).reshape(bsz, s, DN_HEADS, DN_HEAD_DIM) for t in jnp.split(qkv, 3, axis=-1))
    q = l2norm(q) * DN_HEAD_DIM ** -0.5
    k = l2norm(k)
    g = -jnp.exp(a_log.astype(F32)) * jax.nn.softplus(a_in.astype(F32) + dt_bias.astype(F32))
    beta = jax.nn.sigmoid(b_in.astype(F32))
    o = gated_delta_rule(q, k, v, g, beta)
    o = rmsnorm(o, dn_norm_w) * jax.nn.silu(z.astype(F32).reshape(bsz, s, DN_HEADS, DN_HEAD_DIM))
    y_a = o.reshape(bsz, s, DN_WIDTH).astype(h.dtype)
    y_b = s5_branch(u, s5_a_re, s5_a_im, s5_b_re, s5_b_im, s5_c_re, s5_c_im, s5_d, s5_log_step, s5_glu_w, s5_glu_b)
    return jnp.concatenate([y_a, y_b], axis=-1) @ w_out


def odd_mixer(h, layer, lb_logits, w_in, gla_gate_w2, gla_gate_b, gla_norm_w, hg_norm_w, w_out):
    bsz, s, _ = h.shape
    q_c, k_c, v_c, r_c, gate_lr, q_d, f_d, i_d, g_d = split_cols(h @ w_in, ODD_SPLITS)

    def heads(t, nh):
        return t.astype(F32).reshape(bsz, s, nh, -1)

    log_a = jax.nn.log_sigmoid(gate_lr.astype(F32) @ gla_gate_w2.astype(F32) + gla_gate_b.astype(F32)) / GLA_TAU
    o_c = chunked_gla(heads(q_c, GLA_HEADS) * GLA_HEAD_K ** -0.5, heads(k_c, GLA_HEADS), heads(v_c, GLA_HEADS), heads(log_a, GLA_HEADS))
    o_c = rmsnorm(o_c, gla_norm_w) * jax.nn.silu(heads(r_c, GLA_HEADS))
    lb_all = jnp.cumsum(jax.nn.softmax(lb_logits.astype(F32), axis=0), axis=0)
    lb = (lb_all[layer] - lb_all[0]).reshape(HG_HEADS, HG_EXPAND)
    z_f = heads(f_d, HG_HEADS)
    f = lb + (1.0 - lb) * jax.nn.sigmoid(z_f)
    one_minus_f = (1.0 - lb) * jax.nn.sigmoid(-z_f)
    o_d = chunked_gla(heads(q_d, HG_HEADS), one_minus_f, heads(i_d, HG_HEADS), jnp.log(f))
    o_d = rmsnorm(o_d, hg_norm_w) * jax.nn.silu(heads(g_d, HG_HEADS))
    y = jnp.concatenate([o_c.reshape(bsz, s, GLA_WIDTH), o_d.reshape(bsz, s, HG_WIDTH)], axis=-1).astype(h.dtype)
    return y @ w_out


def setup_inputs(seed: int = 0) -> dict:
    keys = iter(jax.random.split(jax.random.key(seed), 64))

    def nrm(shape, scale):
        return scale * jax.random.normal(next(keys), shape, F32)

    def gain(n):
        return 1.0 + 0.05 * jax.random.normal(next(keys), (n,), F32)

    def ffn(prefix):
        return {prefix + '_norm': gain(D_MODEL),
                prefix + '_w_gate': nrm((D_MODEL, D_FF), D_MODEL ** -0.5),
                prefix + '_w_up': nrm((D_MODEL, D_FF), D_MODEL ** -0.5),
                prefix + '_w_down': nrm((D_FF, D_MODEL), D_FF ** -0.5)}

    p = {'x': nrm((BATCH, SEQ, D_MODEL), 1.0)}
    p.update(ffn('l0_ffn1'))
    p['l0_mix_norm'] = gain(D_MODEL)
    p['l0_w_in'] = nrm((D_MODEL, sum(EVEN_SPLITS)), D_MODEL ** -0.5)
    p['l0_dn_conv_w'] = nrm((DN_CONV, 1, 3 * DN_WIDTH), DN_CONV ** -0.5)
    p['l0_dn_a_log'] = jnp.log(jax.random.uniform(next(keys), (DN_HEADS,), F32, minval=1.0, maxval=16.0))
    dt = jnp.exp(jax.random.uniform(next(keys), (DN_HEADS,), F32, minval=float(np.log(1e-3)), maxval=float(np.log(1e-1))))
    p['l0_dn_dt_bias'] = dt + jnp.log(-jnp.expm1(-dt))
    p['l0_dn_norm_w'] = gain(DN_HEAD_DIM)
    p['l0_s5_a_re'] = -0.5 + nrm((S5_GROUPS, S5_STATE), 0.01)
    p['l0_s5_a_im'] = jnp.pi * jnp.arange(S5_STATE, dtype=F32) + nrm((S5_GROUPS, S5_STATE), 0.01)
    p['l0_s5_b_re'] = nrm((S5_GROUPS, S5_STATE, S5_GROUP), (2 * S5_GROUP) ** -0.5)
    p['l0_s5_b_im'] = nrm((S5_GROUPS, S5_STATE, S5_GROUP), (2 * S5_GROUP) ** -0.5)
    p['l0_s5_c_re'] = nrm((S5_GROUPS, S5_GROUP, S5_STATE), (2 * S5_STATE) ** -0.5)
    p['l0_s5_c_im'] = nrm((S5_GROUPS, S5_GROUP, S5_STATE), (2 * S5_STATE) ** -0.5)
    p['l0_s5_d'] = nrm((S5_GROUPS, S5_GROUP), 1.0)
    p['l0_s5_log_step'] = jax.random.uniform(next(keys), (S5_GROUPS,), F32, minval=float(np.log(S5_DT_MIN)), maxval=float(np.log(S5_DT_MAX)))
    p['l0_s5_glu_w'] = nrm((S5_WIDTH, S5_WIDTH), S5_WIDTH ** -0.5)
    p['l0_s5_glu_b'] = nrm((S5_WIDTH,), 0.01)
    p['l0_w_out'] = nrm((MIX_WIDTH, D_MODEL), MIX_WIDTH ** -0.5)
    p.update(ffn('l0_ffn2'))
    p.update(ffn('l1_ffn1'))
    p['l1_mix_norm'] = gain(D_MODEL)
    p['l1_w_in'] = nrm((D_MODEL, sum(ODD_SPLITS)), D_MODEL ** -0.5)
    p['l1_gla_gate_w2'] = nrm((GLA_GATE_RANK, GLA_KEY_WIDTH), GLA_GATE_RANK ** -0.5)
    p['l1_gla_gate_b'] = nrm((GLA_KEY_WIDTH,), 0.01)
    p['l1_gla_norm_w'] = gain(GLA_HEAD_V)
    p['l1_hg_norm_w'] = gain(HG_EXPAND)
    p['l1_w_out'] = nrm((MIX_WIDTH, D_MODEL), MIX_WIDTH ** -0.5)
    p.update(ffn('l1_ffn2'))
    p['hgrn_lb_logits'] = nrm((DEPTH, HG_WIDTH), 0.1)
    p['final_norm'] = gain(D_MODEL)
    return p


def reference(x,
              l0_ffn1_norm, l0_ffn1_w_gate, l0_ffn1_w_up, l0_ffn1_w_down,
              l0_mix_norm, l0_w_in, l0_dn_conv_w, l0_dn_a_log, l0_dn_dt_bias, l0_dn_norm_w,
              l0_s5_a_re, l0_s5_a_im, l0_s5_b_re, l0_s5_b_im, l0_s5_c_re, l0_s5_c_im, l0_s5_d, l0_s5_log_step,
              l0_s5_glu_w, l0_s5_glu_b, l0_w_out,
              l0_ffn2_norm, l0_ffn2_w_gate, l0_ffn2_w_up, l0_ffn2_w_down,
              l1_ffn1_norm, l1_ffn1_w_gate, l1_ffn1_w_up, l1_ffn1_w_down,
              l1_mix_norm, l1_w_in, l1_gla_gate_w2, l1_gla_gate_b, l1_gla_norm_w, l1_hg_norm_w, l1_w_out,
              l1_ffn2_norm, l1_ffn2_w_gate, l1_ffn2_w_up, l1_ffn2_w_down,
              hgrn_lb_logits, final_norm):
    ffn1 = [(l0_ffn1_norm, l0_ffn1_w_gate, l0_ffn1_w_up, l0_ffn1_w_down),
            (l1_ffn1_norm, l1_ffn1_w_gate, l1_ffn1_w_up, l1_ffn1_w_down)]
    ffn2 = [(l0_ffn2_norm, l0_ffn2_w_gate, l0_ffn2_w_up, l0_ffn2_w_down),
            (l1_ffn2_norm, l1_ffn2_w_gate, l1_ffn2_w_up, l1_ffn2_w_down)]
    mix_norm = [l0_mix_norm, l1_mix_norm]
    mixer_params = [(l0_w_in, l0_dn_conv_w, l0_dn_a_log, l0_dn_dt_bias, l0_dn_norm_w,
                     l0_s5_a_re, l0_s5_a_im, l0_s5_b_re, l0_s5_b_im, l0_s5_c_re, l0_s5_c_im, l0_s5_d,
                     l0_s5_log_step, l0_s5_glu_w, l0_s5_glu_b, l0_w_out),
                    (l1_w_in, l1_gla_gate_w2, l1_gla_gate_b, l1_gla_norm_w, l1_hg_norm_w, l1_w_out)]
    for layer in range(DEPTH):
        f1 = ffn1[layer]
        x = x + FFN_RES * swiglu(rmsnorm(x, f1[0]), f1[1], f1[2], f1[3])
        h = rmsnorm(x, mix_norm[layer])
        if layer % 2 == 0:
            x = x + even_mixer(h, *mixer_params[layer])
        else:
            x = x + odd_mixer(h, layer, hgrn_lb_logits, *mixer_params[layer])
        f2 = ffn2[layer]
        x = x + FFN_RES * swiglu(rmsnorm(x, f2[0]), f2[1], f2[2], f2[3])
    return rmsnorm(x, final_norm)
```

```python
import functools

import jax
import jax.numpy as jnp
from jax import lax
from jax.experimental import pallas as pl
from jax.experimental.pallas import tpu as pltpu

F32 = jnp.float32
BF16 = jnp.bfloat16

D_MODEL = 2048
CHUNK = 64
NORM_EPS = 1e-6
D_FF = 5632
FFN_RES = 0.5
DN_HEADS = 8
DN_HEAD_DIM = 128
DN_WIDTH = DN_HEADS * DN_HEAD_DIM
DN_CONV = 4
S5_WIDTH = D_MODEL - DN_WIDTH
S5_GROUP = 16
S5_GROUPS = S5_WIDTH // S5_GROUP
S5_STATE = 64
GLA_HEADS = 4
GLA_WIDTH = D_MODEL // 2
GLA_KEY_WIDTH = GLA_WIDTH // 2
GLA_HEAD_K = GLA_KEY_WIDTH // GLA_HEADS
GLA_HEAD_V = GLA_WIDTH // GLA_HEADS
GLA_GATE_RANK = 16
GLA_TAU = 16.0
HG_WIDTH = D_MODEL - GLA_WIDTH
HG_EXPAND = 128
HG_HEADS = HG_WIDTH // HG_EXPAND

LANES = 128
SUBLANES = 8
VMEM_LIMIT = 56 * 1024 * 1024

S5_TILE_GROUPS = LANES // S5_GROUP
S5_TILE_STATE = S5_TILE_GROUPS * S5_STATE
S5_TILES = S5_WIDTH // LANES


def _params(sem):
    return pltpu.CompilerParams(dimension_semantics=sem, vmem_limit_bytes=VMEM_LIMIT)


def _rms(x, w):
    return x * lax.rsqrt(jnp.mean(x * x, axis=-1, keepdims=True) + NORM_EPS) * w


def _silu(x):
    return x * jax.nn.sigmoid(x)


def _softplus(x):
    return jnp.maximum(x, 0.0) + jnp.log1p(jnp.exp(-jnp.abs(x)))


def _dot(a, b):
    return jnp.dot(a.astype(BF16), b.astype(BF16), preferred_element_type=F32)


def _dot_nt(a, b):
    return lax.dot_general(a.astype(BF16), b.astype(BF16), (((1,), (1,)), ((), ())), preferred_element_type=F32)


def _dot_tn(a, b):
    return lax.dot_general(a.astype(BF16), b.astype(BF16), (((0,), (0,)), ((), ())), preferred_element_type=F32)


def _chunk_cumsum(x):
    pos = lax.broadcasted_iota(jnp.int32, x.shape, 0) % CHUNK
    shift = 1
    while shift < CHUNK:
        x = x + jnp.where(pos >= shift, pltpu.roll(x, shift, axis=0), 0.0)
        shift *= 2
    return x


def _tri_masks():
    row = lax.broadcasted_iota(jnp.int32, (CHUNK, CHUNK), 0)
    col = lax.broadcasted_iota(jnp.int32, (CHUNK, CHUNK), 1)
    return row >= col, row > col


def _ffn_kernel(x_ref, nw_ref, wg_ref, wu_ref, wd_ref, fw_ref, o_ref, xn_ref, *, n_ff_tiles, final):
    j = pl.program_id(1)

    @pl.when(j == 0)
    def _():
        x = x_ref[...]
        xn_ref[...] = _rms(x, nw_ref[...]).astype(BF16)
        o_ref[...] = x

    xn = xn_ref[...]
    g = jnp.dot(xn, wg_ref[...], preferred_element_type=F32)
    u = jnp.dot(xn, wu_ref[...], preferred_element_type=F32)
    h = (FFN_RES * (_silu(g) * u)).astype(BF16)
    o_ref[...] += jnp.dot(h, wd_ref[...], preferred_element_type=F32)

    if final:
        @pl.when(j == n_ff_tiles - 1)
        def _():
            o_ref[...] = _rms(o_ref[...], fw_ref[...])


def _ffn(x2, nw, wg, wu, wd, fw, *, final, tm=512, tf=512):
    t, d = x2.shape
    f = wg.shape[1]
    return pl.pallas_call(
        functools.partial(_ffn_kernel, n_ff_tiles=f // tf, final=final),
        grid=(t // tm, f // tf),
        in_specs=[pl.BlockSpec((tm, d), lambda i, j: (i, 0)),
                  pl.BlockSpec((1, d), lambda i, j: (0, 0)),
                  pl.BlockSpec((d, tf), lambda i, j: (0, j)),
                  pl.BlockSpec((d, tf), lambda i, j: (0, j)),
                  pl.BlockSpec((tf, d), lambda i, j: (j, 0)),
                  pl.BlockSpec((1, d), lambda i, j: (0, 0))],
        out_specs=pl.BlockSpec((tm, d), lambda i, j: (i, 0)),
        out_shape=jax.ShapeDtypeStruct((t, d), F32),
        scratch_shapes=[pltpu.VMEM((tm, d), BF16)],
        compiler_params=_params(("parallel", "arbitrary")),
        name="ffn",
    )(x2, nw.reshape(1, d), wg, wu, wd, fw.reshape(1, d))


def _inproj_kernel(x_ref, nw_ref, w_ref, ws_ref, o_ref, os_ref, xn_ref):
    @pl.when(pl.program_id(1) == 0)
    def _():
        xn = _rms(x_ref[...], nw_ref[...]).astype(BF16)
        xn_ref[...] = xn
        os_ref[...] = jnp.dot(xn, ws_ref[...], preferred_element_type=F32)

    o_ref[...] = jnp.dot(xn_ref[...], w_ref[...], preferred_element_type=F32)


def _inproj(x2, nw, w_main, w_small, *, tm=512, tn=512):
    t, d = x2.shape
    n = w_main.shape[1]
    return pl.pallas_call(
        _inproj_kernel,
        grid=(t // tm, n // tn),
        in_specs=[pl.BlockSpec((tm, d), lambda i, j: (i, 0)),
                  pl.BlockSpec((1, d), lambda i, j: (0, 0)),
                  pl.BlockSpec((d, tn), lambda i, j: (0, j)),
                  pl.BlockSpec((d, LANES), lambda i, j: (0, 0))],
        out_specs=[pl.BlockSpec((tm, tn), lambda i, j: (i, j)),
                   pl.BlockSpec((tm, LANES), lambda i, j: (i, 0))],
        out_shape=[jax.ShapeDtypeStruct((t, n), F32), jax.ShapeDtypeStruct((t, LANES), F32)],
        scratch_shapes=[pltpu.VMEM((tm, d), BF16)],
        compiler_params=_params(("parallel", "arbitrary")),
        name="inproj",
    )(x2, nw.reshape(1, d), w_main, w_small)


def _outproj_kernel(x_ref, ya_ref, yb_ref, wa_ref, wb_ref, o_ref):
    o_ref[...] = (x_ref[...] + jnp.dot(ya_ref[...], wa_ref[...], preferred_element_type=F32)
                  + jnp.dot(yb_ref[...], wb_ref[...], preferred_element_type=F32))


def _outproj(x2, ya, yb, w, *, tm=512):
    t, d = x2.shape
    ka = ya.shape[1]
    kb = yb.shape[1]
    assert ka == kb
    return pl.pallas_call(
        _outproj_kernel,
        grid=(t // tm,),
        in_specs=[pl.BlockSpec((tm, d), lambda i: (i, 0)),
                  pl.BlockSpec((tm, ka), lambda i: (i, 0)),
                  pl.BlockSpec((tm, kb), lambda i: (i, 0)),
                  pl.BlockSpec((ka, d), lambda i: (0, 0)),
                  pl.BlockSpec((kb, d), lambda i: (1, 0))],
        out_specs=pl.BlockSpec((tm, d), lambda i: (i, 0)),
        out_shape=jax.ShapeDtypeStruct((t, d), F32),
        compiler_params=_params(("parallel",)),
        name="outproj",
    )(x2, ya, yb, w, w)


def _causal_conv(x, w):
    rows = lax.broadcasted_iota(jnp.int32, x.shape, 0)
    y = x * w[DN_CONV - 1:DN_CONV, :]
    for back in range(1, DN_CONV):
        shifted = jnp.where(rows >= back, pltpu.roll(x, back, axis=0), 0.0)
        y = y + shifted * w[DN_CONV - 1 - back:DN_CONV - back, :]
    return y


def _l2norm(x):
    return x * lax.rsqrt(jnp.sum(x * x, axis=-1, keepdims=True) + NORM_EPS)


def _deltanet_kernel(q_ref, k_ref, v_ref, z_ref, ab_ref, at_ref, bt_ref, cwq_ref, cwk_ref, cwv_ref,
                     alog_ref, dtb_ref, alog1_ref, dtb1_ref, nw_ref, cum_ref, o_ref,
                     qs, ks, vs, gs, bs, gts):
    h = pl.program_id(1)
    seq = q_ref.shape[1]
    n_chunks = seq // CHUNK

    qs[...] = _l2norm(_silu(_causal_conv(q_ref[0], cwq_ref[...]))) * DN_HEAD_DIM ** -0.5
    ks[...] = _l2norm(_silu(_causal_conv(k_ref[0], cwk_ref[...])))
    vs[...] = _silu(_causal_conv(v_ref[0], cwv_ref[...]))

    ab = ab_ref[0]
    lane = lax.broadcasted_iota(jnp.int32, ab.shape, 1)
    g_all = -jnp.exp(alog_ref[...]) * _softplus(ab + dtb_ref[...])
    gc_all = _chunk_cumsum(g_all)
    gcol = jnp.sum(jnp.where(lane == h, gc_all, 0.0), axis=-1, keepdims=True)
    bcol = jnp.sum(jnp.where(lane == DN_HEADS + h, jax.nn.sigmoid(ab), 0.0), axis=-1, keepdims=True)
    gs[...] = jnp.broadcast_to(gcol, gs.shape)
    bs[...] = jnp.broadcast_to(bcol, bs.shape)
    g_t = -jnp.exp(alog1_ref[0]) * _softplus(at_ref[0, 0] + dtb1_ref[0])
    gts[...] = jnp.dot(g_t, cum_ref[...], preferred_element_type=F32, precision=lax.Precision.HIGHEST)

    incl, strict = _tri_masks()
    nw = nw_ref[...]

    def chunk(c, state):
        r = pl.ds(pl.multiple_of(c * CHUNK, CHUNK), CHUNK)
        q = qs[r, :]
        k = ks[r, :]
        v = vs[r, :]
        gcb = gs[r, :]
        beta = bs[r, :]
        gc_row = gts[pl.ds(c, 1), :]
        decay = jnp.exp(jnp.where(incl, gcb[:, :CHUNK] - gc_row, -jnp.inf))
        kb = k * beta
        a = jnp.where(strict, _dot_nt(kb, k) * decay, 0.0)
        eg = jnp.exp(gcb)
        x = jnp.concatenate([kb * eg, v * beta], axis=1)
        x = x - _dot(a, x)
        p = a
        for _ in range(5):
            p = _dot(p, p)
            x = x + _dot(p, x)
        w = x[:, :DN_HEAD_DIM]
        u = x[:, DN_HEAD_DIM:]
        qk = jnp.where(incl, _dot_nt(q, k) * decay, 0.0)
        g_last = gcb[CHUNK - 1:CHUNK, :]
        v_new = u - _dot(w, state)
        o = _dot(q * eg, state) + _dot(qk, v_new)
        state = jnp.exp(g_last) * state + _dot_tn(k * jnp.exp(g_last - gcb), v_new)
        o_ref[0, r, :] = (_rms(o, nw) * _silu(z_ref[0, r, :])).astype(o_ref.dtype)
        return state

    lax.fori_loop(0, n_chunks, chunk, jnp.zeros((DN_HEAD_DIM, DN_HEAD_DIM), F32))


def _deltanet(main, small, conv_w, a_log, dt_bias, norm_w):
    bsz, seq, _ = main.shape
    n_chunks = seq // CHUNK
    d = DN_HEAD_DIM
    ab_t = small[:, :, :2 * DN_HEADS].transpose(0, 2, 1).reshape(bsz, 2 * DN_HEADS, n_chunks, CHUNK)
    cw = conv_w.reshape(DN_CONV, 3 * DN_WIDTH)
    pad = jnp.zeros((LANES - DN_HEADS,), F32)
    alog_row = jnp.concatenate([a_log, pad]).reshape(1, LANES)
    dtb_row = jnp.concatenate([dt_bias, pad]).reshape(1, LANES)
    cum = jnp.triu(jnp.ones((CHUNK, CHUNK), F32))
    col = lambda off: pl.BlockSpec((1, seq, d), lambda b, h: (b, 0, off + h))
    cwspec = lambda off: pl.BlockSpec((DN_CONV, d), lambda b, h: (0, off + h))
    tspec = lambda off: pl.BlockSpec((1, 1, n_chunks, CHUNK), lambda b, h: (b, off + h, 0, 0))
    row = pl.BlockSpec((1, LANES), lambda b, h: (0, 0))
    one = pl.BlockSpec((1, 1, 1), lambda b, h: (h, 0, 0))
    return pl.pallas_call(
        _deltanet_kernel,
        grid=(bsz, DN_HEADS),
        in_specs=[col(0), col(DN_HEADS), col(2 * DN_HEADS), col(3 * DN_HEADS),
                  pl.BlockSpec((1, seq, LANES), lambda b, h: (b, 0, 0)),
                  tspec(0), tspec(DN_HEADS),
                  cwspec(0), cwspec(DN_HEADS), cwspec(2 * DN_HEADS),
                  row, row, one, one, row,
                  pl.BlockSpec((CHUNK, CHUNK), lambda b, h: (0, 0))],
        out_specs=pl.BlockSpec((1, seq, d), lambda b, h: (b, 0, h)),
        out_shape=jax.ShapeDtypeStruct((bsz, seq, DN_WIDTH), BF16),
        scratch_shapes=[pltpu.VMEM((seq, d), F32)] * 5 + [pltpu.VMEM((n_chunks, CHUNK), F32)],
        compiler_params=_params(("parallel", "arbitrary")),
        name="deltanet",
    )(main, main, main, main, small, ab_t, ab_t, cw, cw, cw, alog_row, dtb_row,
      a_log.reshape(DN_HEADS, 1, 1), dt_bias.reshape(DN_HEADS, 1, 1), norm_w.reshape(1, d), cum)


def _gelu_tanh(x):
    return x * (0.5 * (1.0 + jnp.tanh(0.7978845608028654 * (x + 0.044715 * (x * x * x)))))


def _s5_kernel(u_ref, are_ref, aim_ref, ls_ref, bre_ref, bim_ref, cre_ref, cim_ref, d_ref, o_ref,
               ut, xs, yt, state, bbd, cbd, lam):
    ns = S5_TILE_STATE
    bsz, ts, _ = u_ref.shape

    @pl.when(pl.program_id(1) == 0)
    def _():
        a_re = are_ref[0]
        a_im = aim_ref[0]
        dt = jnp.exp(ls_ref[0])
        mag = jnp.exp(a_re * dt)
        l_re = mag * jnp.cos(a_im * dt)
        l_im = mag * jnp.sin(a_im * dt)
        den = a_re * a_re + a_im * a_im
        c_re = ((l_re - 1.0) * a_re + l_im * a_im) / den
        c_im = (l_im * a_re - (l_re - 1.0) * a_im) / den
        b_re = bre_ref[0]
        b_im = bim_ref[0]
        bbd[:, :ns] = (c_re * b_re - c_im * b_im).astype(BF16)
        bbd[:, ns:] = (c_re * b_im + c_im * b_re).astype(BF16)
        cbd[:ns, :] = cre_ref[0].astype(BF16)
        cbd[ns:, :] = (-cim_ref[0]).astype(BF16)
        lam[:, :ns] = jnp.broadcast_to(l_re, (bsz, ns))
        lam[:, ns:] = jnp.broadcast_to(l_im, (bsz, ns))
        state[...] = jnp.zeros_like(state)

    for b in range(bsz):
        ut[pl.ds(b, ts, stride=bsz), :] = u_ref[b]
    xs[...] = jnp.dot(ut[...].astype(BF16), bbd[...], preferred_element_type=F32)
    l_re = lam[:, :ns]
    l_im = lam[:, ns:]

    def step(t, carry):
        x_re, x_im = carry
        r = pl.ds(pl.multiple_of(t * bsz, bsz), bsz)
        n_re = l_re * x_re - l_im * x_im + xs[r, :ns]
        n_im = l_re * x_im + l_im * x_re + xs[r, ns:]
        xs[r, :ns] = n_re
        xs[r, ns:] = n_im
        return n_re, n_im

    x_re, x_im = lax.fori_loop(0, ts, step, (state[:, :ns], state[:, ns:]))
    state[:, :ns] = x_re
    state[:, ns:] = x_im
    y = jnp.dot(xs[...].astype(BF16), cbd[...], preferred_element_type=F32) + d_ref[0] * ut[...]
    yt[...] = _gelu_tanh(y)
    for b in range(bsz):
        o_ref[b] = yt[pl.ds(b, ts, stride=bsz), :]


def _s5(main, u_col_block, a_re, a_im, b_re, b_im, c_re, c_im, d, log_step, *, ts=128):
    bsz, seq, _ = main.shape
    assert bsz == SUBLANES
    tg, ns, nt = S5_TILE_GROUPS, S5_TILE_STATE, S5_TILES
    eye = jnp.eye(tg, dtype=F32)

    def expand_b(b):
        bt = b.reshape(nt, tg, S5_STATE, S5_GROUP)
        return jnp.einsum('ngph,gk->nghkp', bt, eye).reshape(nt, LANES, ns)

    def expand_c(c):
        ct = c.reshape(nt, tg, S5_GROUP, S5_STATE)
        return jnp.einsum('nghp,gk->ngpkh', ct, eye).reshape(nt, ns, LANES)

    chan = lambda a: a.reshape(nt, 1, ns)
    ls = jnp.broadcast_to(log_step[:, None], (S5_GROUPS, S5_STATE))
    pspec = pl.BlockSpec((1, 1, ns), lambda c, t: (c, 0, 0))
    bspec = pl.BlockSpec((1, LANES, ns), lambda c, t: (c, 0, 0))
    cspec = pl.BlockSpec((1, ns, LANES), lambda c, t: (c, 0, 0))
    return pl.pallas_call(
        _s5_kernel,
        grid=(nt, seq // ts),
        in_specs=[pl.BlockSpec((bsz, ts, LANES), lambda c, t: (0, t, u_col_block + c)),
                  pspec, pspec, pspec, bspec, bspec, cspec, cspec,
                  pl.BlockSpec((1, 1, LANES), lambda c, t: (c, 0, 0))],
        out_specs=pl.BlockSpec((bsz, ts, LANES), lambda c, t: (0, t, c)),
        out_shape=jax.ShapeDtypeStruct((bsz, seq, S5_WIDTH), F32),
        scratch_shapes=[pltpu.VMEM((bsz * ts, LANES), F32),
                        pltpu.VMEM((bsz * ts, 2 * ns), F32),
                        pltpu.VMEM((bsz * ts, LANES), F32),
                        pltpu.VMEM((bsz, 2 * ns), F32),
                        pltpu.VMEM((LANES, 2 * ns), BF16),
                        pltpu.VMEM((2 * ns, LANES), BF16),
                        pltpu.VMEM((bsz, 2 * ns), F32)],
        compiler_params=_params(("parallel", "arbitrary")),
        name="s5",
    )(main, chan(a_re), chan(a_im), chan(ls), expand_b(b_re), expand_b(b_im), expand_c(c_re), expand_c(c_im),
      d.reshape(nt, 1, LANES))


def _glu_kernel(y_ref, w_ref, b_ref, o_ref):
    y = y_ref[...]
    gate = jnp.dot(y.astype(BF16), w_ref[...], preferred_element_type=F32) + b_ref[...]
    o_ref[...] = (y * jax.nn.sigmoid(gate)).astype(o_ref.dtype)


def _glu(y2, w, b, *, tm=512):
    t, n = y2.shape
    return pl.pallas_call(
        _glu_kernel,
        grid=(t // tm,),
        in_specs=[pl.BlockSpec((tm, n), lambda i: (i, 0)),
                  pl.BlockSpec((n, n), lambda i: (0, 0)),
                  pl.BlockSpec((1, n), lambda i: (0, 0))],
        out_specs=pl.BlockSpec((tm, n), lambda i: (i, 0)),
        out_shape=jax.ShapeDtypeStruct((t, n), BF16),
        compiler_params=_params(("parallel",)),
        name="s5_glu",
    )(y2, w, b.reshape(1, n))


def _gla_scan(qs, ks, vs, bs, gate_ref, nw_ref, o_ref):
    seq, dk = qs.shape
    dv = vs.shape[1]
    incl, _ = _tri_masks()
    nw = nw_ref[...]

    def chunk(c, state_t):
        r = pl.ds(pl.multiple_of(c * CHUNK, CHUNK), CHUNK)
        q = qs[r, :]
        k = ks[r, :]
        v = vs[r, :]
        b = bs[r, :]
        b_mid = b[CHUNK // 2 - 1:CHUNK // 2, :]
        b_last = b[CHUNK - 1:CHUNK, :]
        att = jnp.where(incl, _dot_nt(q * jnp.exp(b - b_mid), k * jnp.exp(b_mid - b)), 0.0)
        o = _dot(att, v) + _dot_nt(q * jnp.exp(b), state_t)
        state_t = jnp.exp(b_last) * state_t + _dot_tn(v, k * jnp.exp(b_last - b))
        o_ref[0, r, :] = (_rms(o, nw) * _silu(gate_ref[0, r, :])).astype(o_ref.dtype)
        return state_t

    lax.fori_loop(0, seq // CHUNK, chunk, jnp.zeros((dv, dk), F32))


def _gla_kernel(q_ref, k_ref, v_ref, r_ref, lr_ref, w2_ref, gb_ref, nw_ref, o_ref, qs, ks, vs, bs):
    qs[...] = q_ref[0] * GLA_HEAD_K ** -0.5
    ks[...] = k_ref[0]
    vs[...] = v_ref[0]
    gate = _dot(lr_ref[0], w2_ref[...]) + gb_ref[...]
    log_a = -_softplus(-gate) / GLA_TAU
    bs[...] = _chunk_cumsum(log_a)
    _gla_scan(qs, ks, vs, bs, r_ref, nw_ref, o_ref)


def _hgrn2_kernel(q_ref, f_ref, i_ref, g_ref, lbl_ref, nw_ref, o_ref, qs, ks, vs, bs, *, layer):
    logits = lbl_ref[...]
    e = jnp.exp(logits - jnp.max(logits, axis=0, keepdims=True))
    p = e / jnp.sum(e, axis=0, keepdims=True)
    lb_first = p[0:1, :]
    lb_layer = lb_first
    for i in range(1, layer + 1):
        lb_layer = lb_layer + p[i:i + 1, :]
    lb = lb_layer - lb_first
    z_f = f_ref[0]
    qs[...] = q_ref[0]
    ks[...] = (1.0 - lb) * jax.nn.sigmoid(-z_f)
    vs[...] = i_ref[0]
    bs[...] = _chunk_cumsum(jnp.log(lb + (1.0 - lb) * jax.nn.sigmoid(z_f)))
    _gla_scan(qs, ks, vs, bs, g_ref, nw_ref, o_ref)


def _gla(main, small, w2, gate_b, norm_w):
    bsz, seq, _ = main.shape
    dk, dv = GLA_HEAD_K, GLA_HEAD_V
    w2p = jnp.concatenate([w2, jnp.zeros((LANES - GLA_GATE_RANK, GLA_KEY_WIDTH), F32)], axis=0).astype(BF16)
    kspec = lambda off: pl.BlockSpec((1, seq, dk), lambda b, h: (b, 0, off + h))
    vspec = lambda off: pl.BlockSpec((1, seq, dv), lambda b, h: (b, 0, off + h))
    return pl.pallas_call(
        _gla_kernel,
        grid=(bsz, GLA_HEADS),
        in_specs=[kspec(0), kspec(GLA_HEADS), vspec(GLA_HEADS), vspec(2 * GLA_HEADS),
                  pl.BlockSpec((1, seq, LANES), lambda b, h: (b, 0, 0)),
                  pl.BlockSpec((LANES, dk), lambda b, h: (0, h)),
                  pl.BlockSpec((1, dk), lambda b, h: (0, h)),
                  pl.BlockSpec((1, dv), lambda b, h: (0, 0))],
        out_specs=pl.BlockSpec((1, seq, dv), lambda b, h: (b, 0, h)),
        out_shape=jax.ShapeDtypeStruct((bsz, seq, GLA_WIDTH), BF16),
        scratch_shapes=[pltpu.VMEM((seq, dk), F32), pltpu.VMEM((seq, dk), F32),
                        pltpu.VMEM((seq, dv), F32), pltpu.VMEM((seq, dk), F32)],
        compiler_params=_params(("parallel", "arbitrary")),
        name="gla",
    )(main, main, main, main, small, w2p, gate_b.reshape(1, GLA_KEY_WIDTH), norm_w.reshape(1, dv))


def _hgrn2(main, lb_logits, layer, norm_w):
    bsz, seq, _ = main.shape
    d = HG_EXPAND
    base = (2 * GLA_KEY_WIDTH + 2 * GLA_WIDTH) // d
    col = lambda off: pl.BlockSpec((1, seq, d), lambda b, h: (b, 0, base + off + h))
    depth = lb_logits.shape[0]
    return pl.pallas_call(
        functools.partial(_hgrn2_kernel, layer=layer),
        grid=(bsz, HG_HEADS),
        in_specs=[col(0), col(HG_HEADS), col(2 * HG_HEADS), col(3 * HG_HEADS),
                  pl.BlockSpec((depth, d), lambda b, h: (0, h)),
                  pl.BlockSpec((1, d), lambda b, h: (0, 0))],
        out_specs=pl.BlockSpec((1, seq, d), lambda b, h: (b, 0, h)),
        out_shape=jax.ShapeDtypeStruct((bsz, seq, HG_WIDTH), BF16),
        scratch_shapes=[pltpu.VMEM((seq, d), F32)] * 4,
        compiler_params=_params(("parallel", "arbitrary")),
        name="hgrn2",
    )(main, main, main, main, lb_logits, norm_w.reshape(1, d))


def _pad_cols(w):
    return jnp.concatenate([w, jnp.zeros((w.shape[0], LANES - w.shape[1]), w.dtype)], axis=1)


def _even_mixer(x2, bsz, seq, mix_norm, w_in, conv_w, a_log, dt_bias, dn_norm_w, s5_a_re, s5_a_im, s5_b_re, s5_b_im,
                s5_c_re, s5_c_im, s5_d, s5_log_step, s5_glu_w, s5_glu_b, w_out):
    n_qkvz = 4 * DN_WIDTH
    n_ab = 2 * DN_HEADS
    w_main = jnp.concatenate([w_in[:, :n_qkvz], w_in[:, n_qkvz + n_ab:]], axis=1).astype(BF16)
    w_small = _pad_cols(w_in[:, n_qkvz:n_qkvz + n_ab]).astype(BF16)
    main, small = _inproj(x2, mix_norm, w_main, w_small)
    main = main.reshape(bsz, seq, -1)
    small = small.reshape(bsz, seq, LANES)
    y_a = _deltanet(main, small, conv_w, a_log, dt_bias, dn_norm_w)
    y_s5 = _s5(main, n_qkvz // LANES, s5_a_re, s5_a_im, s5_b_re, s5_b_im, s5_c_re, s5_c_im, s5_d, s5_log_step)
    y_b = _glu(y_s5.reshape(bsz * seq, S5_WIDTH), s5_glu_w.astype(BF16), s5_glu_b)
    return _outproj(x2, y_a.reshape(bsz * seq, DN_WIDTH), y_b, w_out.astype(BF16))


def _odd_mixer(x2, bsz, seq, layer, lb_logits, mix_norm, w_in, gate_w2, gate_b, gla_norm_w, hg_norm_w, w_out):
    n_c = 2 * GLA_KEY_WIDTH + 2 * GLA_WIDTH
    w_main = jnp.concatenate([w_in[:, :n_c], w_in[:, n_c + GLA_GATE_RANK:]], axis=1).astype(BF16)
    w_small = _pad_cols(w_in[:, n_c:n_c + GLA_GATE_RANK]).astype(BF16)
    main, small = _inproj(x2, mix_norm, w_main, w_small)
    main = main.reshape(bsz, seq, -1)
    small = small.reshape(bsz, seq, LANES)
    y_c = _gla(main, small, gate_w2, gate_b, gla_norm_w)
    y_d = _hgrn2(main, lb_logits, layer, hg_norm_w)
    return _outproj(x2, y_c.reshape(bsz * seq, GLA_WIDTH), y_d.reshape(bsz * seq, HG_WIDTH), w_out.astype(BF16))


def kernel(x, l0_ffn1_norm, l0_ffn1_w_gate, l0_ffn1_w_up, l0_ffn1_w_down, l0_mix_norm, l0_w_in, l0_dn_conv_w, l0_dn_a_log, l0_dn_dt_bias, l0_dn_norm_w, l0_s5_a_re, l0_s5_a_im, l0_s5_b_re, l0_s5_b_im, l0_s5_c_re, l0_s5_c_im, l0_s5_d, l0_s5_log_step, l0_s5_glu_w, l0_s5_glu_b, l0_w_out, l0_ffn2_norm, l0_ffn2_w_gate, l0_ffn2_w_up, l0_ffn2_w_down, l1_ffn1_norm, l1_ffn1_w_gate, l1_ffn1_w_up, l1_ffn1_w_down, l1_mix_norm, l1_w_in, l1_gla_gate_w2, l1_gla_gate_b, l1_gla_norm_w, l1_hg_norm_w, l1_w_out, l1_ffn2_norm, l1_ffn2_w_gate, l1_ffn2_w_up, l1_ffn2_w_down, hgrn_lb_logits, final_norm):
    bsz, seq, d = x.shape
    x2 = x.reshape(bsz * seq, d)

    def ffn(x2, nw, wg, wu, wd, final=False):
        return _ffn(x2, nw, wg.astype(BF16), wu.astype(BF16), wd.astype(BF16), final_norm, final=final)

    x2 = ffn(x2, l0_ffn1_norm, l0_ffn1_w_gate, l0_ffn1_w_up, l0_ffn1_w_down)
    x2 = _even_mixer(x2, bsz, seq, l0_mix_norm, l0_w_in, l0_dn_conv_w, l0_dn_a_log, l0_dn_dt_bias, l0_dn_norm_w,
                     l0_s5_a_re, l0_s5_a_im, l0_s5_b_re, l0_s5_b_im, l0_s5_c_re, l0_s5_c_im, l0_s5_d, l0_s5_log_step,
                     l0_s5_glu_w, l0_s5_glu_b, l0_w_out)
    x2 = ffn(x2, l0_ffn2_norm, l0_ffn2_w_gate, l0_ffn2_w_up, l0_ffn2_w_down)
    x2 = ffn(x2, l1_ffn1_norm, l1_ffn1_w_gate, l1_ffn1_w_up, l1_ffn1_w_down)
    x2 = _odd_mixer(x2, bsz, seq, 1, hgrn_lb_logits, l1_mix_norm, l1_w_in, l1_gla_gate_w2, l1_gla_gate_b,
                    l1_gla_norm_w, l1_hg_norm_w, l1_w_out)
    x2 = ffn(x2, l1_ffn2_norm, l1_ffn2_w_gate, l1_ffn2_w_up, l1_ffn2_w_down, final=True)
    return x2.reshape(bsz, seq, d)
```

```python
import functools

import jax
import jax.numpy as jnp
from jax import lax
from jax.experimental import pallas as pl
from jax.experimental.pallas import tpu as pltpu

F32 = jnp.float32
BF16 = jnp.bfloat16

D_MODEL = 2048
CHUNK = 64
NORM_EPS = 1e-6
D_FF = 5632
FFN_RES = 0.5
DN_HEADS = 8
DN_HEAD_DIM = 128
DN_WIDTH = DN_HEADS * DN_HEAD_DIM
DN_CONV = 4
S5_WIDTH = D_MODEL - DN_WIDTH
S5_GROUP = 16
S5_GROUPS = S5_WIDTH // S5_GROUP
S5_STATE = 64
GLA_HEADS = 4
GLA_WIDTH = D_MODEL // 2
GLA_KEY_WIDTH = GLA_WIDTH // 2
GLA_HEAD_K = GLA_KEY_WIDTH // GLA_HEADS
GLA_HEAD_V = GLA_WIDTH // GLA_HEADS
GLA_GATE_RANK = 16
GLA_TAU = 16.0
HG_WIDTH = D_MODEL - GLA_WIDTH
HG_EXPAND = 128
HG_HEADS = HG_WIDTH // HG_EXPAND

LANES = 128
SUBLANES = 8
VMEM_LIMIT = 56 * 1024 * 1024

S5_TILE_GROUPS = LANES // S5_GROUP
S5_TILE_STATE = S5_TILE_GROUPS * S5_STATE
S5_TILES = S5_WIDTH // LANES

DN_PAIR = 2
DN_GROUP = 2
DN_ROWS = DN_GROUP * CHUNK
DN_GROUPS_PER_ITER = 8
N_SPLIT = 3
SEL_ROWS = 16
GLA_GROUP = 4


def _params(sem):
    return pltpu.CompilerParams(dimension_semantics=sem, vmem_limit_bytes=VMEM_LIMIT)


def _rms(x, w):
    return x * lax.rsqrt(jnp.mean(x * x, axis=-1, keepdims=True) + NORM_EPS) * w


def _silu(x):
    return x * jax.nn.sigmoid(x)


def _softplus(x):
    return jnp.maximum(x, 0.0) + jnp.log1p(jnp.exp(-jnp.abs(x)))


def _dot(a, b):
    return jnp.dot(a.astype(BF16), b.astype(BF16), preferred_element_type=F32)


def _dot_nt(a, b):
    return lax.dot_general(a.astype(BF16), b.astype(BF16), (((1,), (1,)), ((), ())), preferred_element_type=F32)


def _dot_tn(a, b):
    return lax.dot_general(a.astype(BF16), b.astype(BF16), (((0,), (0,)), ((), ())), preferred_element_type=F32)


def _chunk_cumsum(x):
    pos = lax.broadcasted_iota(jnp.int32, x.shape, 0) % CHUNK
    shift = 1
    while shift < CHUNK:
        x = x + jnp.where(pos >= shift, pltpu.roll(x, shift, axis=0), 0.0)
        shift *= 2
    return x


def _tri_masks(n=CHUNK):
    row = lax.broadcasted_iota(jnp.int32, (n, n), 0)
    col = lax.broadcasted_iota(jnp.int32, (n, n), 1)
    same = (row // CHUNK) == (col // CHUNK)
    return same & (row >= col), same & (row > col)


def _ffn_kernel(x_ref, nw_ref, wg_ref, wu_ref, wd_ref, fw_ref, o_ref, xn_ref, *, n_ff_tiles, final):
    j = pl.program_id(1)

    @pl.when(j == 0)
    def _():
        x = x_ref[...]
        xn_ref[...] = _rms(x, nw_ref[...]).astype(BF16)
        o_ref[...] = x

    xn = xn_ref[...]
    g = jnp.dot(xn, wg_ref[...], preferred_element_type=F32)
    u = jnp.dot(xn, wu_ref[...], preferred_element_type=F32)
    h = (FFN_RES * (_silu(g) * u)).astype(BF16)
    o_ref[...] += jnp.dot(h, wd_ref[...], preferred_element_type=F32)

    if final:
        @pl.when(j == n_ff_tiles - 1)
        def _():
            o_ref[...] = _rms(o_ref[...], fw_ref[...])


def _ffn(x2, nw, wg, wu, wd, fw, *, final, tm=1024, tf=512):
    t, d = x2.shape
    f = wg.shape[1]
    return pl.pallas_call(
        functools.partial(_ffn_kernel, n_ff_tiles=f // tf, final=final),
        grid=(t // tm, f // tf),
        in_specs=[pl.BlockSpec((tm, d), lambda i, j: (i, 0)),
                  pl.BlockSpec((1, d), lambda i, j: (0, 0)),
                  pl.BlockSpec((d, tf), lambda i, j: (0, j)),
                  pl.BlockSpec((d, tf), lambda i, j: (0, j)),
                  pl.BlockSpec((tf, d), lambda i, j: (j, 0)),
                  pl.BlockSpec((1, d), lambda i, j: (0, 0))],
        out_specs=pl.BlockSpec((tm, d), lambda i, j: (i, 0)),
        out_shape=jax.ShapeDtypeStruct((t, d), F32),
        scratch_shapes=[pltpu.VMEM((tm, d), BF16)],
        compiler_params=_params(("parallel", "arbitrary")),
        name="ffn",
    )(x2, nw.reshape(1, d), wg, wu, wd, fw.reshape(1, d))


def _inproj_kernel(x_ref, nw_ref, w_ref, ws_ref, o_ref, os_ref, xn_ref):
    @pl.when(pl.program_id(1) == 0)
    def _():
        xn = _rms(x_ref[...], nw_ref[...]).astype(BF16)
        xn_ref[...] = xn
        os_ref[...] = jnp.dot(xn, ws_ref[...], preferred_element_type=F32)

    o_ref[...] = jnp.dot(xn_ref[...], w_ref[...], preferred_element_type=F32)


def _inproj(x2, nw, w_main, w_small, *, tm=512, tn=1024):
    t, d = x2.shape
    n = w_main.shape[1]
    return pl.pallas_call(
        _inproj_kernel,
        grid=(t // tm, n // tn),
        in_specs=[pl.BlockSpec((tm, d), lambda i, j: (i, 0)),
                  pl.BlockSpec((1, d), lambda i, j: (0, 0)),
                  pl.BlockSpec((d, tn), lambda i, j: (0, j)),
                  pl.BlockSpec((d, LANES), lambda i, j: (0, 0))],
        out_specs=[pl.BlockSpec((tm, tn), lambda i, j: (i, j)),
                   pl.BlockSpec((tm, LANES), lambda i, j: (i, 0))],
        out_shape=[jax.ShapeDtypeStruct((t, n), F32), jax.ShapeDtypeStruct((t, LANES), F32)],
        scratch_shapes=[pltpu.VMEM((tm, d), BF16)],
        compiler_params=_params(("parallel", "arbitrary")),
        name="inproj",
    )(x2, nw.reshape(1, d), w_main, w_small)


def _outproj_kernel(x_ref, ya_ref, yb_ref, wa_ref, wb_ref, o_ref):
    o_ref[...] = (x_ref[...] + jnp.dot(ya_ref[...], wa_ref[...], preferred_element_type=F32)
                  + jnp.dot(yb_ref[...], wb_ref[...], preferred_element_type=F32))


def _outproj(x2, ya, yb, w, *, tm=512):
    t, d = x2.shape
    ka = ya.shape[1]
    kb = yb.shape[1]
    assert ka == kb
    return pl.pallas_call(
        _outproj_kernel,
        grid=(t // tm,),
        in_specs=[pl.BlockSpec((tm, d), lambda i: (i, 0)),
                  pl.BlockSpec((tm, ka), lambda i: (i, 0)),
                  pl.BlockSpec((tm, kb), lambda i: (i, 0)),
                  pl.BlockSpec((ka, d), lambda i: (0, 0)),
                  pl.BlockSpec((kb, d), lambda i: (1, 0))],
        out_specs=pl.BlockSpec((tm, d), lambda i: (i, 0)),
        out_shape=jax.ShapeDtypeStruct((t, d), F32),
        compiler_params=_params(("parallel",)),
        name="outproj",
    )(x2, ya, yb, w, w)


def _causal_conv(x, w, xpad):
    seq = x.shape[0]
    xpad[SUBLANES:, :] = x
    y = x * w[DN_CONV - 1:DN_CONV, :]
    for back in range(1, DN_CONV):
        y = y + xpad[pl.ds(SUBLANES - back, seq), :] * w[DN_CONV - 1 - back:DN_CONV - back, :]
    return y


def _split3(x):
    parts = []
    for _ in range(N_SPLIT):
        p = x.astype(BF16).astype(F32)
        parts.append(p)
        x = x - p
    return parts


def _l2norm(x):
    return x * lax.rsqrt(jnp.sum(x * x, axis=-1, keepdims=True) + NORM_EPS)


def _deltanet_kernel(q_ref, k_ref, v_ref, z_ref, at_ref, bt_ref, cwq_ref, cwk_ref, cwv_ref,
                     alog1_ref, dtb1_ref, nw_ref, cum_ref, sel_ref, o_ref,
                     xpad, qs, ks, vs, gts, parts_s, m_s, n_s, qp_s, op_s, cd_s):
    seq = q_ref.shape[1]
    n_chunks = seq // CHUNK
    d = DN_HEAD_DIM
    incl, strict = _tri_masks(DN_ROWS)
    nw = nw_ref[...]
    sel = sel_ref[...]
    xpad[:SUBLANES, :] = jnp.zeros((SUBLANES, d), F32)

    for hh in range(DN_PAIR):
        cols = slice(hh * d, (hh + 1) * d)
        qs[...] = _l2norm(_silu(_causal_conv(q_ref[0, :, cols], cwq_ref[:, cols], xpad))) * DN_HEAD_DIM ** -0.5
        ks[...] = _l2norm(_silu(_causal_conv(k_ref[0, :, cols], cwk_ref[:, cols], xpad)))
        vs[...] = _silu(_causal_conv(v_ref[0, :, cols], cwv_ref[:, cols], xpad))
        g_t = -jnp.exp(alog1_ref[hh]) * _softplus(at_ref[0, hh] + dtb1_ref[hh])
        gc_t = jnp.dot(g_t, cum_ref[...], preferred_element_type=F32, precision=lax.Precision.HIGHEST)
        gts[...] = gc_t
        for i, part in enumerate(_split3(gc_t) + _split3(jax.nn.sigmoid(bt_ref[0, hh]))):
            parts_s[i] = part

        def prep(it, carry):
            groups = [it * DN_GROUPS_PER_ITER + g for g in range(DN_GROUPS_PER_ITER)]
            rows = [pl.ds(pl.multiple_of(gi * DN_ROWS, DN_ROWS), DN_ROWS) for gi in groups]
            q = [qs[r, :] for r in rows]
            k = [ks[r, :] for r in rows]
            src = [jnp.concatenate([parts_s[i, pl.ds(gi, 1), :] for i in range(2 * N_SPLIT)]
                                   + [jnp.zeros((SEL_ROWS - 2 * N_SPLIT, DN_ROWS), F32)], axis=0) for gi in groups]
            colb = [_dot_tn(s, sel) for s in src]
            gcb = [c[:, :d] for c in colb]
            beta = [c[:, d:] for c in colb]
            decay = [jnp.exp(jnp.where(incl, jnp.concatenate([gc] * (DN_ROWS // d), axis=1) - gts[pl.ds(gi, 1), :],
                                       -jnp.inf)) for gc, gi in zip(gcb, groups)]
            kb = [ki * bi for ki, bi in zip(k, beta)]
            a = [jnp.where(strict, _dot_nt(kbi, ki) * dc, 0.0) for kbi, ki, dc in zip(kb, k, decay)]
            eg = [jnp.exp(gc) for gc in gcb]
            x = [jnp.concatenate([kbi * egi, vs[r, :] * bi], axis=1) for kbi, egi, r, bi in zip(kb, eg, rows, beta)]
            pb = [ai.astype(BF16) for ai in a]
            x = [xi - _dot(pi, xi) for pi, xi in zip(pb, x)]
            for _ in range(CHUNK.bit_length() - 2):
                pb = [jnp.dot(pi, pi, preferred_element_type=F32).astype(BF16) for pi in pb]
                x = [xi + _dot(pi, xi) for pi, xi in zip(pb, x)]
            xb = [xi.astype(BF16) for xi in x]
            qk = [jnp.where(incl, _dot_nt(qi, ki) * dc, 0.0) for qi, ki, dc in zip(q, k, decay)]
            qx = [_dot(qki, xi) for qki, xi in zip(qk, xb)]
            for g in range(DN_GROUPS_PER_ITER):
                last = [gcb[g][(i + 1) * CHUNK - 1:(i + 1) * CHUNK, :] for i in range(DN_GROUP)]
                g_last = jnp.concatenate([jnp.broadcast_to(li, (CHUNK, d)) for li in last], axis=0)
                kd = (k[g] * jnp.exp(g_last - gcb[g])).astype(BF16)
                for i in range(DN_GROUP):
                    c = groups[g] * DN_GROUP + i
                    cr = slice(i * CHUNK, (i + 1) * CHUNK)
                    kx = _dot_tn(kd[cr, :], xb[g][cr, :])
                    m_s[hh, c] = kx[:, :d].astype(BF16)
                    n_s[hh, c] = kx[:, d:]
                    cd_s[hh, pl.ds(c, 1), :] = jnp.exp(last[i])
                qp_s[hh, rows[g], :] = (q[g] * eg[g] - qx[g][:, :d]).astype(BF16)
                op_s[hh, rows[g], :] = qx[g][:, d:]
            return carry

        lax.fori_loop(0, seq // (DN_ROWS * DN_GROUPS_PER_ITER), prep, 0)

    def step(c, states):
        r = pl.ds(pl.multiple_of(c * CHUNK, CHUNK), CHUNK)
        new = []
        for hh in range(DN_PAIR):
            cols = slice(hh * d, (hh + 1) * d)
            sb = states[hh].astype(BF16)
            o = jnp.dot(qp_s[hh, r, :], sb, preferred_element_type=F32) + op_s[hh, r, :]
            o_ref[0, r, cols] = (_rms(o, nw) * _silu(z_ref[0, r, cols])).astype(o_ref.dtype)
            new.append(cd_s[hh, pl.ds(c, 1), :] * states[hh] + n_s[hh, c]
                       - jnp.dot(m_s[hh, c], sb, preferred_element_type=F32))
        return tuple(new)

    lax.fori_loop(0, n_chunks, step, tuple(jnp.zeros((d, d), F32) for _ in range(DN_PAIR)))


def _deltanet(main, small, conv_w, a_log, dt_bias, norm_w):
    bsz, seq, _ = main.shape
    n_chunks = seq // CHUNK
    d = DN_HEAD_DIM
    pw = DN_PAIR * d
    n_pairs = DN_HEADS // DN_PAIR
    n_groups = seq // DN_ROWS
    assert seq % (DN_ROWS * DN_GROUPS_PER_ITER) == 0 and DN_ROWS % d == 0
    ab_t = small[:, :, :2 * DN_HEADS].transpose(0, 2, 1).reshape(bsz, 2 * DN_HEADS, n_groups, DN_ROWS)
    cw = conv_w.reshape(DN_CONV, 3 * DN_WIDTH)
    cum = jnp.kron(jnp.eye(DN_GROUP, dtype=F32), jnp.triu(jnp.ones((CHUNK, CHUNK), F32)))
    sel = jnp.kron((jnp.arange(SEL_ROWS)[:, None] // N_SPLIT == jnp.arange(2)[None, :]).astype(BF16),
                   jnp.ones((1, d), BF16))
    col = lambda off: pl.BlockSpec((1, seq, pw), lambda b, h: (b, 0, off + h))
    cwspec = lambda off: pl.BlockSpec((DN_CONV, pw), lambda b, h: (0, off + h))
    tspec = lambda off: pl.BlockSpec((1, DN_PAIR, n_groups, DN_ROWS), lambda b, h: (b, off + h, 0, 0))
    one = pl.BlockSpec((DN_PAIR, 1, 1), lambda b, h: (h, 0, 0))
    return pl.pallas_call(
        _deltanet_kernel,
        grid=(bsz, n_pairs),
        in_specs=[col(0), col(n_pairs), col(2 * n_pairs), col(3 * n_pairs),
                  tspec(0), tspec(n_pairs),
                  cwspec(0), cwspec(n_pairs), cwspec(2 * n_pairs),
                  one, one,
                  pl.BlockSpec((1, d), lambda b, h: (0, 0)),
                  pl.BlockSpec((DN_ROWS, DN_ROWS), lambda b, h: (0, 0)),
                  pl.BlockSpec((SEL_ROWS, 2 * d), lambda b, h: (0, 0))],
        out_specs=pl.BlockSpec((1, seq, pw), lambda b, h: (b, 0, h)),
        out_shape=jax.ShapeDtypeStruct((bsz, seq, DN_WIDTH), BF16),
        scratch_shapes=[pltpu.VMEM((seq + SUBLANES, d), F32)] + [pltpu.VMEM((seq, d), F32)] * 3 + [
                        pltpu.VMEM((n_groups, DN_ROWS), F32),
                        pltpu.VMEM((2 * N_SPLIT, n_groups, DN_ROWS), F32),
                        pltpu.VMEM((DN_PAIR, n_chunks, d, d), BF16),
                        pltpu.VMEM((DN_PAIR, n_chunks, d, d), F32),
                        pltpu.VMEM((DN_PAIR, seq, d), BF16),
                        pltpu.VMEM((DN_PAIR, seq, d), F32),
                        pltpu.VMEM((DN_PAIR, n_chunks, d), F32)],
        compiler_params=_params(("parallel", "arbitrary")),
        name="deltanet",
    )(main, main, main, main, ab_t, ab_t, cw, cw, cw,
      a_log.reshape(DN_HEADS, 1, 1), dt_bias.reshape(DN_HEADS, 1, 1), norm_w.reshape(1, d), cum, sel)


def _gelu_tanh(x):
    return x * (0.5 * (1.0 + jnp.tanh(0.7978845608028654 * (x + 0.044715 * (x * x * x)))))


def _s5_kernel(u_ref, are_ref, aim_ref, ls_ref, bre_ref, bim_ref, cre_ref, cim_ref, d_ref, o_ref,
               ut, xs, yt, state, bbd, cbd, lam):
    ns = S5_TILE_STATE
    bsz, ts, _ = u_ref.shape

    @pl.when(pl.program_id(1) == 0)
    def _():
        a_re = are_ref[0]
        a_im = aim_ref[0]
        dt = jnp.exp(ls_ref[0])
        mag = jnp.exp(a_re * dt)
        l_re = mag * jnp.cos(a_im * dt)
        l_im = mag * jnp.sin(a_im * dt)
        den = a_re * a_re + a_im * a_im
        c_re = ((l_re - 1.0) * a_re + l_im * a_im) / den
        c_im = (l_im * a_re - (l_re - 1.0) * a_im) / den
        b_re = bre_ref[0]
        b_im = bim_ref[0]
        bbd[:, :ns] = (c_re * b_re - c_im * b_im).astype(BF16)
        bbd[:, ns:] = (c_re * b_im + c_im * b_re).astype(BF16)
        cbd[:ns, :] = cre_ref[0].astype(BF16)
        cbd[ns:, :] = (-cim_ref[0]).astype(BF16)
        lam[:, :ns] = jnp.broadcast_to(l_re, (bsz, ns))
        lam[:, ns:] = jnp.broadcast_to(l_im, (bsz, ns))
        state[...] = jnp.zeros_like(state)

    for b in range(bsz):
        ut[pl.ds(b, ts, stride=bsz), :] = u_ref[b]
    xs[...] = jnp.dot(ut[...].astype(BF16), bbd[...], preferred_element_type=F32)
    l_re = lam[:, :ns]
    l_im = lam[:, ns:]

    def step(t, carry):
        x_re, x_im = carry
        r = pl.ds(pl.multiple_of(t * bsz, bsz), bsz)
        n_re = l_re * x_re - l_im * x_im + xs[r, :ns]
        n_im = l_re * x_im + l_im * x_re + xs[r, ns:]
        xs[r, :ns] = n_re
        xs[r, ns:] = n_im
        return n_re, n_im

    x_re, x_im = lax.fori_loop(0, ts, step, (state[:, :ns], state[:, ns:]))
    state[:, :ns] = x_re
    state[:, ns:] = x_im
    y = jnp.dot(xs[...].astype(BF16), cbd[...], preferred_element_type=F32) + d_ref[0] * ut[...]
    yt[...] = _gelu_tanh(y)
    for b in range(bsz):
        o_ref[b] = yt[pl.ds(b, ts, stride=bsz), :]


def _s5(main, u_col_block, a_re, a_im, b_re, b_im, c_re, c_im, d, log_step, *, ts=128):
    bsz, seq, _ = main.shape
    assert bsz == SUBLANES
    tg, ns, nt = S5_TILE_GROUPS, S5_TILE_STATE, S5_TILES
    eye = jnp.eye(tg, dtype=F32)

    def expand_b(b):
        bt = b.reshape(nt, tg, S5_STATE, S5_GROUP)
        return jnp.einsum('ngph,gk->nghkp', bt, eye).reshape(nt, LANES, ns)

    def expand_c(c):
        ct = c.reshape(nt, tg, S5_GROUP, S5_STATE)
        return jnp.einsum('nghp,gk->ngpkh', ct, eye).reshape(nt, ns, LANES)

    chan = lambda a: a.reshape(nt, 1, ns)
    ls = jnp.broadcast_to(log_step[:, None], (S5_GROUPS, S5_STATE))
    pspec = pl.BlockSpec((1, 1, ns), lambda c, t: (c, 0, 0))
    bspec = pl.BlockSpec((1, LANES, ns), lambda c, t: (c, 0, 0))
    cspec = pl.BlockSpec((1, ns, LANES), lambda c, t: (c, 0, 0))
    return pl.pallas_call(
        _s5_kernel,
        grid=(nt, seq // ts),
        in_specs=[pl.BlockSpec((bsz, ts, LANES), lambda c, t: (0, t, u_col_block + c)),
                  pspec, pspec, pspec, bspec, bspec, cspec, cspec,
                  pl.BlockSpec((1, 1, LANES), lambda c, t: (c, 0, 0))],
        out_specs=pl.BlockSpec((bsz, ts, LANES), lambda c, t: (0, t, c)),
        out_shape=jax.ShapeDtypeStruct((bsz, seq, S5_WIDTH), F32),
        scratch_shapes=[pltpu.VMEM((bsz * ts, LANES), F32),
                        pltpu.VMEM((bsz * ts, 2 * ns), F32),
                        pltpu.VMEM((bsz * ts, LANES), F32),
                        pltpu.VMEM((bsz, 2 * ns), F32),
                        pltpu.VMEM((LANES, 2 * ns), BF16),
                        pltpu.VMEM((2 * ns, LANES), BF16),
                        pltpu.VMEM((bsz, 2 * ns), F32)],
        compiler_params=_params(("parallel", "arbitrary")),
        name="s5",
    )(main, chan(a_re), chan(a_im), chan(ls), expand_b(b_re), expand_b(b_im), expand_c(c_re), expand_c(c_im),
      d.reshape(nt, 1, LANES))


def _glu_kernel(y_ref, w_ref, b_ref, o_ref):
    y = y_ref[...]
    gate = jnp.dot(y.astype(BF16), w_ref[...], preferred_element_type=F32) + b_ref[...]
    o_ref[...] = (y * jax.nn.sigmoid(gate)).astype(o_ref.dtype)


def _glu(y2, w, b, *, tm=512):
    t, n = y2.shape
    return pl.pallas_call(
        _glu_kernel,
        grid=(t // tm,),
        in_specs=[pl.BlockSpec((tm, n), lambda i: (i, 0)),
                  pl.BlockSpec((n, n), lambda i: (0, 0)),
                  pl.BlockSpec((1, n), lambda i: (0, 0))],
        out_specs=pl.BlockSpec((tm, n), lambda i: (i, 0)),
        out_shape=jax.ShapeDtypeStruct((t, n), BF16),
        compiler_params=_params(("parallel",)),
        name="s5_glu",
    )(y2, w, b.reshape(1, n))


def _gla_scan(qs, ks, vs, bs, gate_ref, nw_ref, o_ref, st_s):
    seq, dk = qs.shape
    dv = vs.shape[1]
    n_chunks = seq // CHUNK
    incl, _ = _tri_masks()
    nw = nw_ref[...]

    def chunk_rows(it):
        cs = [it * GLA_GROUP + i for i in range(GLA_GROUP)]
        return cs, [pl.ds(pl.multiple_of(c * CHUNK, CHUNK), CHUNK) for c in cs]

    def states(it, state_t):
        cs, rows = chunk_rows(it)
        b = [bs[r, :] for r in rows]
        b_last = [bi[CHUNK - 1:CHUNK, :] for bi in b]
        inc = [_dot_tn(vs[r, :], ks[r, :] * jnp.exp(bl - bi)) for r, bl, bi in zip(rows, b_last, b)]
        for c, bl, ic in zip(cs, b_last, inc):
            st_s[c] = state_t.astype(BF16)
            state_t = jnp.exp(bl) * state_t + ic
        return state_t

    lax.fori_loop(0, n_chunks // GLA_GROUP, states, jnp.zeros((dv, dk), F32))

    def outputs(it, carry):
        cs, rows = chunk_rows(it)
        q = [qs[r, :] for r in rows]
        b = [bs[r, :] for r in rows]
        b_mid = [bi[CHUNK // 2 - 1:CHUNK // 2, :] for bi in b]
        att = [jnp.where(incl, _dot_nt(qi * jnp.exp(bi - bm), ks[r, :] * jnp.exp(bm - bi)), 0.0)
               for qi, bi, bm, r in zip(q, b, b_mid, rows)]
        o_inter = [_dot_nt(qi * jnp.exp(bi), st_s[c]) for qi, bi, c in zip(q, b, cs)]
        o = [_dot(ai, vs[r, :]) + oi for ai, r, oi in zip(att, rows, o_inter)]
        for r, oi in zip(rows, o):
            o_ref[0, r, :] = (_rms(oi, nw) * _silu(gate_ref[0, r, :])).astype(o_ref.dtype)
        return carry

    lax.fori_loop(0, n_chunks // GLA_GROUP, outputs, 0)


def _gla_kernel(q_ref, k_ref, v_ref, r_ref, lr_ref, w2_ref, gb_ref, nw_ref, o_ref, qs, ks, vs, bs, st_s):
    qs[...] = q_ref[0] * GLA_HEAD_K ** -0.5
    ks[...] = k_ref[0]
    vs[...] = v_ref[0]
    gate = _dot(lr_ref[0], w2_ref[...]) + gb_ref[...]
    log_a = -_softplus(-gate) / GLA_TAU
    bs[...] = _chunk_cumsum(log_a)
    _gla_scan(qs, ks, vs, bs, r_ref, nw_ref, o_ref, st_s)


def _hgrn2_kernel(q_ref, f_ref, i_ref, g_ref, lbl_ref, nw_ref, o_ref, qs, ks, vs, bs, st_s, *, layer):
    logits = lbl_ref[...]
    e = jnp.exp(logits - jnp.max(logits, axis=0, keepdims=True))
    p = e / jnp.sum(e, axis=0, keepdims=True)
    lb_first = p[0:1, :]
    lb_layer = lb_first
    for i in range(1, layer + 1):
        lb_layer = lb_layer + p[i:i + 1, :]
    lb = lb_layer - lb_first
    z_f = f_ref[0]
    qs[...] = q_ref[0]
    ks[...] = (1.0 - lb) * jax.nn.sigmoid(-z_f)
    vs[...] = i_ref[0]
    bs[...] = _chunk_cumsum(jnp.log(lb + (1.0 - lb) * jax.nn.sigmoid(z_f)))
    _gla_scan(qs, ks, vs, bs, g_ref, nw_ref, o_ref, st_s)


def _gla(main, small, w2, gate_b, norm_w):
    bsz, seq, _ = main.shape
    dk, dv = GLA_HEAD_K, GLA_HEAD_V
    w2p = jnp.concatenate([w2, jnp.zeros((LANES - GLA_GATE_RANK, GLA_KEY_WIDTH), F32)], axis=0).astype(BF16)
    kspec = lambda off: pl.BlockSpec((1, seq, dk), lambda b, h: (b, 0, off + h))
    vspec = lambda off: pl.BlockSpec((1, seq, dv), lambda b, h: (b, 0, off + h))
    return pl.pallas_call(
        _gla_kernel,
        grid=(bsz, GLA_HEADS),
        in_specs=[kspec(0), kspec(GLA_HEADS), vspec(GLA_HEADS), vspec(2 * GLA_HEADS),
                  pl.BlockSpec((1, seq, LANES), lambda b, h: (b, 0, 0)),
                  pl.BlockSpec((LANES, dk), lambda b, h: (0, h)),
                  pl.BlockSpec((1, dk), lambda b, h: (0, h)),
                  pl.BlockSpec((1, dv), lambda b, h: (0, 0))],
        out_specs=pl.BlockSpec((1, seq, dv), lambda b, h: (b, 0, h)),
        out_shape=jax.ShapeDtypeStruct((bsz, seq, GLA_WIDTH), BF16),
        scratch_shapes=[pltpu.VMEM((seq, dk), F32), pltpu.VMEM((seq, dk), F32),
                        pltpu.VMEM((seq, dv), F32), pltpu.VMEM((seq, dk), F32),
                        pltpu.VMEM((seq // CHUNK, dv, dk), BF16)],
        compiler_params=_params(("parallel", "arbitrary")),
        name="gla",
    )(main, main, main, main, small, w2p, gate_b.reshape(1, GLA_KEY_WIDTH), norm_w.reshape(1, dv))


def _hgrn2(main, lb_logits, layer, norm_w):
    bsz, seq, _ = main.shape
    d = HG_EXPAND
    base = (2 * GLA_KEY_WIDTH + 2 * GLA_WIDTH) // d
    col = lambda off: pl.BlockSpec((1, seq, d), lambda b, h: (b, 0, base + off + h))
    depth = lb_logits.shape[0]
    return pl.pallas_call(
        functools.partial(_hgrn2_kernel, layer=layer),
        grid=(bsz, HG_HEADS),
        in_specs=[col(0), col(HG_HEADS), col(2 * HG_HEADS), col(3 * HG_HEADS),
                  pl.BlockSpec((depth, d), lambda b, h: (0, h)),
                  pl.BlockSpec((1, d), lambda b, h: (0, 0))],
        out_specs=pl.BlockSpec((1, seq, d), lambda b, h: (b, 0, h)),
        out_shape=jax.ShapeDtypeStruct((bsz, seq, HG_WIDTH), BF16),
        scratch_shapes=[pltpu.VMEM((seq, d), F32)] * 4 + [pltpu.VMEM((seq // CHUNK, d, d), BF16)],
        compiler_params=_params(("parallel", "arbitrary")),
        name="hgrn2",
    )(main, main, main, main, lb_logits, norm_w.reshape(1, d))


def _pad_cols(w):
    return jnp.concatenate([w, jnp.zeros((w.shape[0], LANES - w.shape[1]), w.dtype)], axis=1)


def _even_mixer(x2, bsz, seq, mix_norm, w_in, conv_w, a_log, dt_bias, dn_norm_w, s5_a_re, s5_a_im, s5_b_re, s5_b_im,
                s5_c_re, s5_c_im, s5_d, s5_log_step, s5_glu_w, s5_glu_b, w_out):
    n_qkvz = 4 * DN_WIDTH
    n_ab = 2 * DN_HEADS
    w_main = jnp.concatenate([w_in[:, :n_qkvz], w_in[:, n_qkvz + n_ab:]], axis=1).astype(BF16)
    w_small = _pad_cols(w_in[:, n_qkvz:n_qkvz + n_ab]).astype(BF16)
    main, small = _inproj(x2, mix_norm, w_main, w_small)
    main = main.reshape(bsz, seq, -1)
    small = small.reshape(bsz, seq, LANES)
    y_a = _deltanet(main, small, conv_w, a_log, dt_bias, dn_norm_w)
    y_s5 = _s5(main, n_qkvz // LANES, s5_a_re, s5_a_im, s5_b_re, s5_b_im, s5_c_re, s5_c_im, s5_d, s5_log_step)
    y_b = _glu(y_s5.reshape(bsz * seq, S5_WIDTH), s5_glu_w.astype(BF16), s5_glu_b)
    return _outproj(x2, y_a.reshape(bsz * seq, DN_WIDTH), y_b, w_out.astype(BF16))


def _odd_mixer(x2, bsz, seq, layer, lb_logits, mix_norm, w_in, gate_w2, gate_b, gla_norm_w, hg_norm_w, w_out):
    n_c = 2 * GLA_KEY_WIDTH + 2 * GLA_WIDTH
    w_main = jnp.concatenate([w_in[:, :n_c], w_in[:, n_c + GLA_GATE_RANK:]], axis=1).astype(BF16)
    w_small = _pad_cols(w_in[:, n_c:n_c + GLA_GATE_RANK]).astype(BF16)
    main, small = _inproj(x2, mix_norm, w_main, w_small)
    main = main.reshape(bsz, seq, -1)
    small = small.reshape(bsz, seq, LANES)
    y_c = _gla(main, small, gate_w2, gate_b, gla_norm_w)
    y_d = _hgrn2(main, lb_logits, layer, hg_norm_w)
    return _outproj(x2, y_c.reshape(bsz * seq, GLA_WIDTH), y_d.reshape(bsz * seq, HG_WIDTH), w_out.astype(BF16))


def kernel(x, l0_ffn1_norm, l0_ffn1_w_gate, l0_ffn1_w_up, l0_ffn1_w_down, l0_mix_norm, l0_w_in, l0_dn_conv_w, l0_dn_a_log, l0_dn_dt_bias, l0_dn_norm_w, l0_s5_a_re, l0_s5_a_im, l0_s5_b_re, l0_s5_b_im, l0_s5_c_re, l0_s5_c_im, l0_s5_d, l0_s5_log_step, l0_s5_glu_w, l0_s5_glu_b, l0_w_out, l0_ffn2_norm, l0_ffn2_w_gate, l0_ffn2_w_up, l0_ffn2_w_down, l1_ffn1_norm, l1_ffn1_w_gate, l1_ffn1_w_up, l1_ffn1_w_down, l1_mix_norm, l1_w_in, l1_gla_gate_w2, l1_gla_gate_b, l1_gla_norm_w, l1_hg_norm_w, l1_w_out, l1_ffn2_norm, l1_ffn2_w_gate, l1_ffn2_w_up, l1_ffn2_w_down, hgrn_lb_logits, final_norm):
    bsz, seq, d = x.shape
    x2 = x.reshape(bsz * seq, d)

    def ffn(x2, nw, wg, wu, wd, final=False):
        return _ffn(x2, nw, wg.astype(BF16), wu.astype(BF16), wd.astype(BF16), final_norm, final=final)

    x2 = ffn(x2, l0_ffn1_norm, l0_ffn1_w_gate, l0_ffn1_w_up, l0_ffn1_w_down)
    x2 = _even_mixer(x2, bsz, seq, l0_mix_norm, l0_w_in, l0_dn_conv_w, l0_dn_a_log, l0_dn_dt_bias, l0_dn_norm_w,
                     l0_s5_a_re, l0_s5_a_im, l0_s5_b_re, l0_s5_b_im, l0_s5_c_re, l0_s5_c_im, l0_s5_d, l0_s5_log_step,
                     l0_s5_glu_w, l0_s5_glu_b, l0_w_out)
    x2 = ffn(x2, l0_ffn2_norm, l0_ffn2_w_gate, l0_ffn2_w_up, l0_ffn2_w_down)
    x2 = ffn(x2, l1_ffn1_norm, l1_ffn1_w_gate, l1_ffn1_w_up, l1_ffn1_w_down)
    x2 = _odd_mixer(x2, bsz, seq, 1, hgrn_lb_logits, l1_mix_norm, l1_w_in, l1_gla_gate_w2, l1_gla_gate_b,
                    l1_gla_norm_w, l1_hg_norm_w, l1_w_out)
    x2 = ffn(x2, l1_ffn2_norm, l1_ffn2_w_gate, l1_ffn2_w_up, l1_ffn2_w_down, final=True)
    return x2.reshape(bsz, seq, d)
```

```python
import functools

import jax
import jax.numpy as jnp
from jax import lax
from jax.experimental import pallas as pl
from jax.experimental.pallas import tpu as pltpu

F32 = jnp.float32
BF16 = jnp.bfloat16

D_MODEL = 2048
CHUNK = 64
NORM_EPS = 1e-6
D_FF = 5632
FFN_RES = 0.5
DN_HEADS = 8
DN_HEAD_DIM = 128
DN_WIDTH = DN_HEADS * DN_HEAD_DIM
DN_CONV = 4
S5_WIDTH = D_MODEL - DN_WIDTH
S5_GROUP = 16
S5_GROUPS = S5_WIDTH // S5_GROUP
S5_STATE = 64
GLA_HEADS = 4
GLA_WIDTH = D_MODEL // 2
GLA_KEY_WIDTH = GLA_WIDTH // 2
GLA_HEAD_K = GLA_KEY_WIDTH // GLA_HEADS
GLA_HEAD_V = GLA_WIDTH // GLA_HEADS
GLA_GATE_RANK = 16
GLA_TAU = 16.0
HG_WIDTH = D_MODEL - GLA_WIDTH
HG_EXPAND = 128
HG_HEADS = HG_WIDTH // HG_EXPAND

LANES = 128
SUBLANES = 8
BF16_ROWS = 16
VMEM_LIMIT = 56 * 1024 * 1024

S5_TILE_GROUPS = LANES // S5_GROUP
S5_TILE_STATE = S5_TILE_GROUPS * S5_STATE
S5_TILES = S5_WIDTH // LANES

DN_PAIR = 2
DN_GROUP = 2
DN_ROWS = DN_GROUP * CHUNK
DN_GROUPS_PER_ITER = 8
N_SPLIT = 3
SEL_ROWS = 16
GLA_GROUP = 4


def _params(sem):
    return pltpu.CompilerParams(dimension_semantics=sem, vmem_limit_bytes=VMEM_LIMIT)


def _rms(x, w):
    return x * lax.rsqrt(jnp.mean(x * x, axis=-1, keepdims=True) + NORM_EPS) * w


def _silu(x):
    return x * jax.nn.sigmoid(x)


def _softplus(x):
    return jnp.maximum(x, 0.0) + jnp.log1p(jnp.exp(-jnp.abs(x)))


def _dot(a, b):
    return jnp.dot(a.astype(BF16), b.astype(BF16), preferred_element_type=F32)


def _dot_nt(a, b):
    return lax.dot_general(a.astype(BF16), b.astype(BF16), (((1,), (1,)), ((), ())), preferred_element_type=F32)


def _dot_tn(a, b):
    return lax.dot_general(a.astype(BF16), b.astype(BF16), (((0,), (0,)), ((), ())), preferred_element_type=F32)


def _chunk_cumsum(x):
    pos = lax.broadcasted_iota(jnp.int32, x.shape, 0) % CHUNK
    shift = 1
    while shift < CHUNK:
        x = x + jnp.where(pos >= shift, pltpu.roll(x, shift, axis=0), 0.0)
        shift *= 2
    return x


def _tri_masks(n=CHUNK):
    row = lax.broadcasted_iota(jnp.int32, (n, n), 0)
    col = lax.broadcasted_iota(jnp.int32, (n, n), 1)
    same = (row // CHUNK) == (col // CHUNK)
    return same & (row >= col), same & (row > col)


def _ffn_kernel(x_ref, nw_ref, wg_ref, wu_ref, wd_ref, fw_ref, o_ref, xn_ref, *, n_ff_tiles, final):
    j = pl.program_id(1)

    @pl.when(j == 0)
    def _():
        x = x_ref[...]
        xn_ref[...] = _rms(x, nw_ref[...]).astype(BF16)
        o_ref[...] = x

    xn = xn_ref[...]
    g = jnp.dot(xn, wg_ref[...], preferred_element_type=F32)
    u = jnp.dot(xn, wu_ref[...], preferred_element_type=F32)
    h = (FFN_RES * (_silu(g) * u)).astype(BF16)
    o_ref[...] += jnp.dot(h, wd_ref[...], preferred_element_type=F32)

    if final:
        @pl.when(j == n_ff_tiles - 1)
        def _():
            o_ref[...] = _rms(o_ref[...], fw_ref[...])


def _ffn(x2, nw, wg, wu, wd, fw, *, final, tm=1024, tf=512):
    t, d = x2.shape
    f = wg.shape[1]
    return pl.pallas_call(
        functools.partial(_ffn_kernel, n_ff_tiles=f // tf, final=final),
        grid=(t // tm, f // tf),
        in_specs=[pl.BlockSpec((tm, d), lambda i, j: (i, 0)),
                  pl.BlockSpec((1, d), lambda i, j: (0, 0)),
                  pl.BlockSpec((d, tf), lambda i, j: (0, j)),
                  pl.BlockSpec((d, tf), lambda i, j: (0, j)),
                  pl.BlockSpec((tf, d), lambda i, j: (j, 0)),
                  pl.BlockSpec((1, d), lambda i, j: (0, 0))],
        out_specs=pl.BlockSpec((tm, d), lambda i, j: (i, 0)),
        out_shape=jax.ShapeDtypeStruct((t, d), F32),
        scratch_shapes=[pltpu.VMEM((tm, d), BF16)],
        compiler_params=_params(("parallel", "arbitrary")),
        name="ffn",
    )(x2, nw.reshape(1, d), wg, wu, wd, fw.reshape(1, d))


def _inproj_kernel(x_ref, nw_ref, w_ref, ws_ref, o_ref, os_ref, xn_ref):
    @pl.when(pl.program_id(1) == 0)
    def _():
        xn = _rms(x_ref[...], nw_ref[...]).astype(BF16)
        xn_ref[...] = xn
        os_ref[...] = jnp.dot(xn, ws_ref[...], preferred_element_type=F32)

    o_ref[...] = jnp.dot(xn_ref[...], w_ref[...], preferred_element_type=F32)


def _inproj(x2, nw, w_main, w_small, *, tm=1024, tn=1024):
    t, d = x2.shape
    n = w_main.shape[1]
    return pl.pallas_call(
        _inproj_kernel,
        grid=(t // tm, n // tn),
        in_specs=[pl.BlockSpec((tm, d), lambda i, j: (i, 0)),
                  pl.BlockSpec((1, d), lambda i, j: (0, 0)),
                  pl.BlockSpec((d, tn), lambda i, j: (0, j)),
                  pl.BlockSpec((d, LANES), lambda i, j: (0, 0))],
        out_specs=[pl.BlockSpec((tm, tn), lambda i, j: (i, j)),
                   pl.BlockSpec((tm, LANES), lambda i, j: (i, 0))],
        out_shape=[jax.ShapeDtypeStruct((t, n), F32), jax.ShapeDtypeStruct((t, LANES), F32)],
        scratch_shapes=[pltpu.VMEM((tm, d), BF16)],
        compiler_params=_params(("parallel", "arbitrary")),
        name="inproj",
    )(x2, nw.reshape(1, d), w_main, w_small)


def _outproj_kernel(x_ref, ya_ref, yb_ref, wa_ref, wb_ref, o_ref):
    o_ref[...] = (x_ref[...] + jnp.dot(ya_ref[...], wa_ref[...], preferred_element_type=F32)
                  + jnp.dot(yb_ref[...], wb_ref[...], preferred_element_type=F32))


def _outproj(x2, ya, yb, w, *, tm=512):
    t, d = x2.shape
    ka = ya.shape[1]
    kb = yb.shape[1]
    assert ka == kb
    return pl.pallas_call(
        _outproj_kernel,
        grid=(t // tm,),
        in_specs=[pl.BlockSpec((tm, d), lambda i: (i, 0)),
                  pl.BlockSpec((tm, ka), lambda i: (i, 0)),
                  pl.BlockSpec((tm, kb), lambda i: (i, 0)),
                  pl.BlockSpec((ka, d), lambda i: (0, 0)),
                  pl.BlockSpec((kb, d), lambda i: (1, 0))],
        out_specs=pl.BlockSpec((tm, d), lambda i: (i, 0)),
        out_shape=jax.ShapeDtypeStruct((t, d), F32),
        compiler_params=_params(("parallel",)),
        name="outproj",
    )(x2, ya, yb, w, w)


def _causal_conv(x, w, xpad):
    seq = x.shape[0]
    xpad[SUBLANES:, :] = x
    y = x * w[DN_CONV - 1:DN_CONV, :]
    for back in range(1, DN_CONV):
        y = y + xpad[pl.ds(SUBLANES - back, seq), :] * w[DN_CONV - 1 - back:DN_CONV - back, :]
    return y


def _split3(x):
    parts = []
    for _ in range(N_SPLIT):
        p = x.astype(BF16).astype(F32)
        parts.append(p)
        x = x - p
    return parts


def _l2norm(x):
    return x * lax.rsqrt(jnp.sum(x * x, axis=-1, keepdims=True) + NORM_EPS)


def _deltanet_kernel(q_ref, k_ref, v_ref, z_ref, at_ref, bt_ref, cwq_ref, cwk_ref, cwv_ref,
                     alog1_ref, dtb1_ref, nw_ref, cum_ref, sel_ref, o_ref,
                     xpad, qs, ks, vs, gts, parts_s, qm_s, n_s, op_s, cd_s):
    seq = q_ref.shape[1]
    n_chunks = seq // CHUNK
    d = DN_HEAD_DIM
    incl, strict = _tri_masks(DN_ROWS)
    nw = nw_ref[...]
    sel = sel_ref[...]
    xpad[:SUBLANES, :] = jnp.zeros((SUBLANES, d), F32)

    for hh in range(DN_PAIR):
        cols = slice(hh * d, (hh + 1) * d)
        qs[...] = _l2norm(_silu(_causal_conv(q_ref[0, :, cols], cwq_ref[:, cols], xpad))) * DN_HEAD_DIM ** -0.5
        ks[...] = _l2norm(_silu(_causal_conv(k_ref[0, :, cols], cwk_ref[:, cols], xpad)))
        vs[...] = _silu(_causal_conv(v_ref[0, :, cols], cwv_ref[:, cols], xpad))
        g_t = -jnp.exp(alog1_ref[hh]) * _softplus(at_ref[0, hh] + dtb1_ref[hh])
        gc_t = jnp.dot(g_t, cum_ref[...], preferred_element_type=F32, precision=lax.Precision.HIGHEST)
        gts[...] = gc_t
        for i, part in enumerate(_split3(gc_t) + _split3(jax.nn.sigmoid(bt_ref[0, hh]))):
            parts_s[i] = part

        def prep(it, carry):
            groups = [it * DN_GROUPS_PER_ITER + g for g in range(DN_GROUPS_PER_ITER)]
            rows = [pl.ds(pl.multiple_of(gi * DN_ROWS, DN_ROWS), DN_ROWS) for gi in groups]
            q = [qs[r, :] for r in rows]
            k = [ks[r, :] for r in rows]
            src = [jnp.concatenate([parts_s[i, pl.ds(gi, 1), :] for i in range(2 * N_SPLIT)]
                                   + [jnp.zeros((SEL_ROWS - 2 * N_SPLIT, DN_ROWS), F32)], axis=0) for gi in groups]
            colb = [_dot_tn(s, sel) for s in src]
            gcb = [c[:, :d] for c in colb]
            beta = [c[:, d:] for c in colb]
            decay = [jnp.exp(jnp.where(incl, jnp.concatenate([gc] * (DN_ROWS // d), axis=1) - gts[pl.ds(gi, 1), :],
                                       -jnp.inf)) for gc, gi in zip(gcb, groups)]
            kb = [ki * bi for ki, bi in zip(k, beta)]
            a = [jnp.where(strict, _dot_nt(kbi, ki) * dc, 0.0) for kbi, ki, dc in zip(kb, k, decay)]
            eg = [jnp.exp(gc) for gc in gcb]
            x = [jnp.concatenate([kbi * egi, vs[r, :] * bi], axis=1) for kbi, egi, r, bi in zip(kb, eg, rows, beta)]
            pb = [ai.astype(BF16) for ai in a]
            x = [xi - _dot(pi, xi) for pi, xi in zip(pb, x)]
            for _ in range(CHUNK.bit_length() - 2):
                pb = [jnp.dot(pi, pi, preferred_element_type=F32).astype(BF16) for pi in pb]
                x = [xi + _dot(pi, xi) for pi, xi in zip(pb, x)]
            xb = [xi.astype(BF16) for xi in x]
            qk = [jnp.where(incl, _dot_nt(qi, ki) * dc, 0.0) for qi, ki, dc in zip(q, k, decay)]
            qx = [_dot(qki, xi) for qki, xi in zip(qk, xb)]
            for g in range(DN_GROUPS_PER_ITER):
                last = [gcb[g][(i + 1) * CHUNK - 1:(i + 1) * CHUNK, :] for i in range(DN_GROUP)]
                g_last = jnp.concatenate([jnp.broadcast_to(li, (CHUNK, d)) for li in last], axis=0)
                kd = (k[g] * jnp.exp(g_last - gcb[g])).astype(BF16)
                qp = (q[g] * eg[g] - qx[g][:, :d]).astype(BF16)
                for i in range(DN_GROUP):
                    c = groups[g] * DN_GROUP + i
                    cr = slice(i * CHUNK, (i + 1) * CHUNK)
                    kx = _dot_tn(kd[cr, :], xb[g][cr, :])
                    qm_s[hh, c, :CHUNK, :] = qp[cr, :]
                    qm_s[hh, c, CHUNK:, :] = kx[:, :d].astype(BF16)
                    n_s[hh, c] = kx[:, d:]
                    cd_s[hh, pl.ds(c, 1), :] = jnp.exp(last[i])
                op_s[hh, rows[g], :] = qx[g][:, d:]
            return carry

        lax.fori_loop(0, seq // (DN_ROWS * DN_GROUPS_PER_ITER), prep, 0)

    def step(c, states):
        r = pl.ds(pl.multiple_of(c * CHUNK, CHUNK), CHUNK)
        new = []
        for hh in range(DN_PAIR):
            cols = slice(hh * d, (hh + 1) * d)
            prod = jnp.dot(qm_s[hh, c], states[hh].astype(BF16), preferred_element_type=F32)
            o = prod[:CHUNK, :] + op_s[hh, r, :]
            o_ref[0, r, cols] = (_rms(o, nw) * _silu(z_ref[0, r, cols])).astype(o_ref.dtype)
            new.append(cd_s[hh, pl.ds(c, 1), :] * states[hh] + n_s[hh, c] - prod[CHUNK:, :])
        return tuple(new)

    lax.fori_loop(0, n_chunks, step, tuple(jnp.zeros((d, d), F32) for _ in range(DN_PAIR)))


def _deltanet(main, small, conv_w, a_log, dt_bias, norm_w):
    bsz, seq, _ = main.shape
    n_chunks = seq // CHUNK
    d = DN_HEAD_DIM
    pw = DN_PAIR * d
    n_pairs = DN_HEADS // DN_PAIR
    n_groups = seq // DN_ROWS
    assert seq % (DN_ROWS * DN_GROUPS_PER_ITER) == 0 and DN_ROWS % d == 0
    ab_t = small[:, :, :2 * DN_HEADS].transpose(0, 2, 1).reshape(bsz, 2 * DN_HEADS, n_groups, DN_ROWS)
    cw = conv_w.reshape(DN_CONV, 3 * DN_WIDTH)
    cum = jnp.kron(jnp.eye(DN_GROUP, dtype=F32), jnp.triu(jnp.ones((CHUNK, CHUNK), F32)))
    sel = jnp.kron((jnp.arange(SEL_ROWS)[:, None] // N_SPLIT == jnp.arange(2)[None, :]).astype(BF16),
                   jnp.ones((1, d), BF16))
    col = lambda off: pl.BlockSpec((1, seq, pw), lambda b, h: (b, 0, off + h))
    cwspec = lambda off: pl.BlockSpec((DN_CONV, pw), lambda b, h: (0, off + h))
    tspec = lambda off: pl.BlockSpec((1, DN_PAIR, n_groups, DN_ROWS), lambda b, h: (b, off + h, 0, 0))
    one = pl.BlockSpec((DN_PAIR, 1, 1), lambda b, h: (h, 0, 0))
    return pl.pallas_call(
        _deltanet_kernel,
        grid=(bsz, n_pairs),
        in_specs=[col(0), col(n_pairs), col(2 * n_pairs), col(3 * n_pairs),
                  tspec(0), tspec(n_pairs),
                  cwspec(0), cwspec(n_pairs), cwspec(2 * n_pairs),
                  one, one,
                  pl.BlockSpec((1, d), lambda b, h: (0, 0)),
                  pl.BlockSpec((DN_ROWS, DN_ROWS), lambda b, h: (0, 0)),
                  pl.BlockSpec((SEL_ROWS, 2 * d), lambda b, h: (0, 0))],
        out_specs=pl.BlockSpec((1, seq, pw), lambda b, h: (b, 0, h)),
        out_shape=jax.ShapeDtypeStruct((bsz, seq, DN_WIDTH), BF16),
        scratch_shapes=[pltpu.VMEM((seq + SUBLANES, d), F32)] + [pltpu.VMEM((seq, d), F32)] * 3 + [
                        pltpu.VMEM((n_groups, DN_ROWS), F32),
                        pltpu.VMEM((2 * N_SPLIT, n_groups, DN_ROWS), F32),
                        pltpu.VMEM((DN_PAIR, n_chunks, CHUNK + d, d), BF16),
                        pltpu.VMEM((DN_PAIR, n_chunks, d, d), F32),
                        pltpu.VMEM((DN_PAIR, seq, d), F32),
                        pltpu.VMEM((DN_PAIR, n_chunks, d), F32)],
        compiler_params=_params(("parallel", "arbitrary")),
        name="deltanet",
    )(main, main, main, main, ab_t, ab_t, cw, cw, cw,
      a_log.reshape(DN_HEADS, 1, 1), dt_bias.reshape(DN_HEADS, 1, 1), norm_w.reshape(1, d), cum, sel)


def _gelu_tanh(x):
    return x * (0.5 * (1.0 + jnp.tanh(0.7978845608028654 * (x + 0.044715 * (x * x * x)))))


def _s5_kernel(u_ref, are_ref, aim_ref, ls_ref, bre_ref, bim_ref, cre_ref, cim_ref, d_ref, o_ref,
               ut, xs, xb, yt, state, bbd, cbd, lam):
    ns = S5_TILE_STATE
    bsz, ts, _ = u_ref.shape

    @pl.when(pl.program_id(1) == 0)
    def _():
        a_re = are_ref[0]
        a_im = aim_ref[0]
        dt = jnp.exp(ls_ref[0])
        mag = jnp.exp(a_re * dt)
        l_re = mag * jnp.cos(a_im * dt)
        l_im = mag * jnp.sin(a_im * dt)
        den = a_re * a_re + a_im * a_im
        c_re = ((l_re - 1.0) * a_re + l_im * a_im) / den
        c_im = (l_im * a_re - (l_re - 1.0) * a_im) / den
        b_re = bre_ref[0]
        b_im = bim_ref[0]
        bbd[:, :ns] = (c_re * b_re - c_im * b_im).astype(BF16)
        bbd[:, ns:] = (c_re * b_im + c_im * b_re).astype(BF16)
        cbd[:ns, :] = cre_ref[0].astype(BF16)
        cbd[ns:, :] = (-cim_ref[0]).astype(BF16)
        lam[:, :ns] = jnp.broadcast_to(l_re, (bsz, ns))
        lam[:, ns:] = jnp.broadcast_to(l_im, (bsz, ns))
        state[...] = jnp.zeros_like(state)

    for b in range(bsz):
        ut[pl.ds(b, ts, stride=bsz), :] = u_ref[b]
    xs[...] = jnp.dot(ut[...].astype(BF16), bbd[...], preferred_element_type=F32)
    l_re = lam[:, :ns]
    l_im = lam[:, ns:]

    frames = BF16_ROWS // bsz

    def step(i, carry):
        x_re, x_im = carry
        r = pl.ds(pl.multiple_of(i * BF16_ROWS, BF16_ROWS), BF16_ROWS)
        bu = xs[r, :]
        res_re, res_im = [], []
        for f in range(frames):
            rows = slice(f * bsz, (f + 1) * bsz)
            x_re, x_im = (l_re * x_re - l_im * x_im + bu[rows, :ns], l_re * x_im + l_im * x_re + bu[rows, ns:])
            res_re.append(x_re)
            res_im.append(x_im)
        xb[r, :ns] = jnp.concatenate(res_re, axis=0).astype(BF16)
        xb[r, ns:] = jnp.concatenate(res_im, axis=0).astype(BF16)
        return x_re, x_im

    x_re, x_im = lax.fori_loop(0, ts // frames, step, (state[:, :ns], state[:, ns:]))
    state[:, :ns] = x_re
    state[:, ns:] = x_im
    y = jnp.dot(xb[...], cbd[...], preferred_element_type=F32) + d_ref[0] * ut[...]
    yt[...] = _gelu_tanh(y)
    for b in range(bsz):
        o_ref[b] = yt[pl.ds(b, ts, stride=bsz), :]


def _s5(main, u_col_block, a_re, a_im, b_re, b_im, c_re, c_im, d, log_step, *, ts=128):
    bsz, seq, _ = main.shape
    assert bsz == SUBLANES
    tg, ns, nt = S5_TILE_GROUPS, S5_TILE_STATE, S5_TILES
    eye = jnp.eye(tg, dtype=F32)

    def expand_b(b):
        bt = b.reshape(nt, tg, S5_STATE, S5_GROUP)
        return jnp.einsum('ngph,gk->nghkp', bt, eye).reshape(nt, LANES, ns)

    def expand_c(c):
        ct = c.reshape(nt, tg, S5_GROUP, S5_STATE)
        return jnp.einsum('nghp,gk->ngpkh', ct, eye).reshape(nt, ns, LANES)

    chan = lambda a: a.reshape(nt, 1, ns)
    ls = jnp.broadcast_to(log_step[:, None], (S5_GROUPS, S5_STATE))
    pspec = pl.BlockSpec((1, 1, ns), lambda c, t: (c, 0, 0))
    bspec = pl.BlockSpec((1, LANES, ns), lambda c, t: (c, 0, 0))
    cspec = pl.BlockSpec((1, ns, LANES), lambda c, t: (c, 0, 0))
    return pl.pallas_call(
        _s5_kernel,
        grid=(nt, seq // ts),
        in_specs=[pl.BlockSpec((bsz, ts, LANES), lambda c, t: (0, t, u_col_block + c)),
                  pspec, pspec, pspec, bspec, bspec, cspec, cspec,
                  pl.BlockSpec((1, 1, LANES), lambda c, t: (c, 0, 0))],
        out_specs=pl.BlockSpec((bsz, ts, LANES), lambda c, t: (0, t, c)),
        out_shape=jax.ShapeDtypeStruct((bsz, seq, S5_WIDTH), F32),
        scratch_shapes=[pltpu.VMEM((bsz * ts, LANES), F32),
                        pltpu.VMEM((bsz * ts, 2 * ns), F32),
                        pltpu.VMEM((bsz * ts, 2 * ns), BF16),
                        pltpu.VMEM((bsz * ts, LANES), F32),
                        pltpu.VMEM((bsz, 2 * ns), F32),
                        pltpu.VMEM((LANES, 2 * ns), BF16),
                        pltpu.VMEM((2 * ns, LANES), BF16),
                        pltpu.VMEM((bsz, 2 * ns), F32)],
        compiler_params=_params(("parallel", "arbitrary")),
        name="s5",
    )(main, chan(a_re), chan(a_im), chan(ls), expand_b(b_re), expand_b(b_im), expand_c(c_re), expand_c(c_im),
      d.reshape(nt, 1, LANES))


def _glu_kernel(y_ref, w_ref, b_ref, o_ref):
    y = y_ref[...]
    gate = jnp.dot(y.astype(BF16), w_ref[...], preferred_element_type=F32) + b_ref[...]
    o_ref[...] = (y * jax.nn.sigmoid(gate)).astype(o_ref.dtype)


def _glu(y2, w, b, *, tm=512):
    t, n = y2.shape
    return pl.pallas_call(
        _glu_kernel,
        grid=(t // tm,),
        in_specs=[pl.BlockSpec((tm, n), lambda i: (i, 0)),
                  pl.BlockSpec((n, n), lambda i: (0, 0)),
                  pl.BlockSpec((1, n), lambda i: (0, 0))],
        out_specs=pl.BlockSpec((tm, n), lambda i: (i, 0)),
        out_shape=jax.ShapeDtypeStruct((t, n), BF16),
        compiler_params=_params(("parallel",)),
        name="s5_glu",
    )(y2, w, b.reshape(1, n))


def _gla_scan(q_at, k_at, v_at, bs, gate_ref, nw_ref, o_ref, st_s):
    n_chunks, dv, dk = st_s.shape
    incl, _ = _tri_masks()
    nw = nw_ref[...]

    def chunk_rows(it):
        cs = [it * GLA_GROUP + i for i in range(GLA_GROUP)]
        return cs, [pl.ds(pl.multiple_of(c * CHUNK, CHUNK), CHUNK) for c in cs]

    def states(it, state_t):
        cs, rows = chunk_rows(it)
        b = [bs[r, :] for r in rows]
        b_last = [bi[CHUNK - 1:CHUNK, :] for bi in b]
        inc = [_dot_tn(v_at(r), k_at(r) * jnp.exp(bl - bi)) for r, bl, bi in zip(rows, b_last, b)]
        for c, bl, ic in zip(cs, b_last, inc):
            st_s[c] = state_t.astype(BF16)
            state_t = jnp.exp(bl) * state_t + ic
        return state_t

    lax.fori_loop(0, n_chunks // GLA_GROUP, states, jnp.zeros((dv, dk), F32))

    def outputs(it, carry):
        cs, rows = chunk_rows(it)
        q = [q_at(r) for r in rows]
        b = [bs[r, :] for r in rows]
        b_mid = [bi[CHUNK // 2 - 1:CHUNK // 2, :] for bi in b]
        att = [jnp.where(incl, _dot_nt(qi * jnp.exp(bi - bm), k_at(r) * jnp.exp(bm - bi)), 0.0)
               for qi, bi, bm, r in zip(q, b, b_mid, rows)]
        o_inter = [_dot_nt(qi * jnp.exp(bi), st_s[c]) for qi, bi, c in zip(q, b, cs)]
        o = [_dot(ai, v_at(r)) + oi for ai, r, oi in zip(att, rows, o_inter)]
        for r, oi in zip(rows, o):
            o_ref[0, r, :] = (_rms(oi, nw) * _silu(gate_ref[0, r, :])).astype(o_ref.dtype)
        return carry

    lax.fori_loop(0, n_chunks // GLA_GROUP, outputs, 0)


def _gla_kernel(q_ref, k_ref, v_ref, r_ref, lr_ref, w2_ref, gb_ref, nw_ref, o_ref, bs, st_s):
    gate = _dot(lr_ref[0], w2_ref[...]) + gb_ref[...]
    log_a = -_softplus(-gate) / GLA_TAU
    bs[...] = _chunk_cumsum(log_a)
    _gla_scan(lambda r: q_ref[0, r, :] * GLA_HEAD_K ** -0.5, lambda r: k_ref[0, r, :], lambda r: v_ref[0, r, :],
              bs, r_ref, nw_ref, o_ref, st_s)


def _hgrn2_kernel(q_ref, f_ref, i_ref, g_ref, lbl_ref, nw_ref, o_ref, ks, bs, st_s, *, layer):
    logits = lbl_ref[...]
    e = jnp.exp(logits - jnp.max(logits, axis=0, keepdims=True))
    p = e / jnp.sum(e, axis=0, keepdims=True)
    lb_first = p[0:1, :]
    lb_layer = lb_first
    for i in range(1, layer + 1):
        lb_layer = lb_layer + p[i:i + 1, :]
    lb = lb_layer - lb_first
    z_f = f_ref[0]
    ks[...] = (1.0 - lb) * jax.nn.sigmoid(-z_f)
    bs[...] = _chunk_cumsum(jnp.log(lb + (1.0 - lb) * jax.nn.sigmoid(z_f)))
    _gla_scan(lambda r: q_ref[0, r, :], lambda r: ks[r, :], lambda r: i_ref[0, r, :],
              bs, g_ref, nw_ref, o_ref, st_s)


def _gla(main, small, w2, gate_b, norm_w):
    bsz, seq, _ = main.shape
    dk, dv = GLA_HEAD_K, GLA_HEAD_V
    w2p = jnp.concatenate([w2, jnp.zeros((LANES - GLA_GATE_RANK, GLA_KEY_WIDTH), F32)], axis=0).astype(BF16)
    kspec = lambda off: pl.BlockSpec((1, seq, dk), lambda b, h: (b, 0, off + h))
    vspec = lambda off: pl.BlockSpec((1, seq, dv), lambda b, h: (b, 0, off + h))
    return pl.pallas_call(
        _gla_kernel,
        grid=(bsz, GLA_HEADS),
        in_specs=[kspec(0), kspec(GLA_HEADS), vspec(GLA_HEADS), vspec(2 * GLA_HEADS),
                  pl.BlockSpec((1, seq, LANES), lambda b, h: (b, 0, 0)),
                  pl.BlockSpec((LANES, dk), lambda b, h: (0, h)),
                  pl.BlockSpec((1, dk), lambda b, h: (0, h)),
                  pl.BlockSpec((1, dv), lambda b, h: (0, 0))],
        out_specs=pl.BlockSpec((1, seq, dv), lambda b, h: (b, 0, h)),
        out_shape=jax.ShapeDtypeStruct((bsz, seq, GLA_WIDTH), BF16),
        scratch_shapes=[pltpu.VMEM((seq, dk), F32), pltpu.VMEM((seq // CHUNK, dv, dk), BF16)],
        compiler_params=_params(("parallel", "arbitrary")),
        name="gla",
    )(main, main, main, main, small, w2p, gate_b.reshape(1, GLA_KEY_WIDTH), norm_w.reshape(1, dv))


def _hgrn2(main, lb_logits, layer, norm_w):
    bsz, seq, _ = main.shape
    d = HG_EXPAND
    base = (2 * GLA_KEY_WIDTH + 2 * GLA_WIDTH) // d
    col = lambda off: pl.BlockSpec((1, seq, d), lambda b, h: (b, 0, base + off + h))
    depth = lb_logits.shape[0]
    return pl.pallas_call(
        functools.partial(_hgrn2_kernel, layer=layer),
        grid=(bsz, HG_HEADS),
        in_specs=[col(0), col(HG_HEADS), col(2 * HG_HEADS), col(3 * HG_HEADS),
                  pl.BlockSpec((depth, d), lambda b, h: (0, h)),
                  pl.BlockSpec((1, d), lambda b, h: (0, 0))],
        out_specs=pl.BlockSpec((1, seq, d), lambda b, h: (b, 0, h)),
        out_shape=jax.ShapeDtypeStruct((bsz, seq, HG_WIDTH), BF16),
        scratch_shapes=[pltpu.VMEM((seq, d), F32)] * 2 + [pltpu.VMEM((seq // CHUNK, d, d), BF16)],
        compiler_params=_params(("parallel", "arbitrary")),
        name="hgrn2",
    )(main, main, main, main, lb_logits, norm_w.reshape(1, d))


def _pad_cols(w):
    return jnp.concatenate([w, jnp.zeros((w.shape[0], LANES - w.shape[1]), w.dtype)], axis=1)


def _even_mixer(x2, bsz, seq, mix_norm, w_in, conv_w, a_log, dt_bias, dn_norm_w, s5_a_re, s5_a_im, s5_b_re, s5_b_im,
                s5_c_re, s5_c_im, s5_d, s5_log_step, s5_glu_w, s5_glu_b, w_out):
    n_qkvz = 4 * DN_WIDTH
    n_ab = 2 * DN_HEADS
    w_in = w_in.astype(BF16)
    w_main = jnp.concatenate([w_in[:, :n_qkvz], w_in[:, n_qkvz + n_ab:]], axis=1)
    w_small = _pad_cols(w_in[:, n_qkvz:n_qkvz + n_ab])
    main, small = _inproj(x2, mix_norm, w_main, w_small)
    main = main.reshape(bsz, seq, -1)
    small = small.reshape(bsz, seq, LANES)
    y_a = _deltanet(main, small, conv_w, a_log, dt_bias, dn_norm_w)
    y_s5 = _s5(main, n_qkvz // LANES, s5_a_re, s5_a_im, s5_b_re, s5_b_im, s5_c_re, s5_c_im, s5_d, s5_log_step)
    y_b = _glu(y_s5.reshape(bsz * seq, S5_WIDTH), s5_glu_w.astype(BF16), s5_glu_b)
    return _outproj(x2, y_a.reshape(bsz * seq, DN_WIDTH), y_b, w_out.astype(BF16))


def _odd_mixer(x2, bsz, seq, layer, lb_logits, mix_norm, w_in, gate_w2, gate_b, gla_norm_w, hg_norm_w, w_out):
    n_c = 2 * GLA_KEY_WIDTH + 2 * GLA_WIDTH
    w_in = w_in.astype(BF16)
    w_main = jnp.concatenate([w_in[:, :n_c], w_in[:, n_c + GLA_GATE_RANK:]], axis=1)
    w_small = _pad_cols(w_in[:, n_c:n_c + GLA_GATE_RANK])
    main, small = _inproj(x2, mix_norm, w_main, w_small)
    main = main.reshape(bsz, seq, -1)
    small = small.reshape(bsz, seq, LANES)
    y_c = _gla(main, small, gate_w2, gate_b, gla_norm_w)
    y_d = _hgrn2(main, lb_logits, layer, hg_norm_w)
    return _outproj(x2, y_c.reshape(bsz * seq, GLA_WIDTH), y_d.reshape(bsz * seq, HG_WIDTH), w_out.astype(BF16))


def kernel(x, l0_ffn1_norm, l0_ffn1_w_gate, l0_ffn1_w_up, l0_ffn1_w_down, l0_mix_norm, l0_w_in, l0_dn_conv_w, l0_dn_a_log, l0_dn_dt_bias, l0_dn_norm_w, l0_s5_a_re, l0_s5_a_im, l0_s5_b_re, l0_s5_b_im, l0_s5_c_re, l0_s5_c_im, l0_s5_d, l0_s5_log_step, l0_s5_glu_w, l0_s5_glu_b, l0_w_out, l0_ffn2_norm, l0_ffn2_w_gate, l0_ffn2_w_up, l0_ffn2_w_down, l1_ffn1_norm, l1_ffn1_w_gate, l1_ffn1_w_up, l1_ffn1_w_down, l1_mix_norm, l1_w_in, l1_gla_gate_w2, l1_gla_gate_b, l1_gla_norm_w, l1_hg_norm_w, l1_w_out, l1_ffn2_norm, l1_ffn2_w_gate, l1_ffn2_w_up, l1_ffn2_w_down, hgrn_lb_logits, final_norm):
    bsz, seq, d = x.shape
    x2 = x.reshape(bsz * seq, d)

    def ffn(x2, nw, wg, wu, wd, final=False):
        return _ffn(x2, nw, wg.astype(BF16), wu.astype(BF16), wd.astype(BF16), final_norm, final=final)

    x2 = ffn(x2, l0_ffn1_norm, l0_ffn1_w_gate, l0_ffn1_w_up, l0_ffn1_w_down)
    x2 = _even_mixer(x2, bsz, seq, l0_mix_norm, l0_w_in, l0_dn_conv_w, l0_dn_a_log, l0_dn_dt_bias, l0_dn_norm_w,
                     l0_s5_a_re, l0_s5_a_im, l0_s5_b_re, l0_s5_b_im, l0_s5_c_re, l0_s5_c_im, l0_s5_d, l0_s5_log_step,
                     l0_s5_glu_w, l0_s5_glu_b, l0_w_out)
    x2 = ffn(x2, l0_ffn2_norm, l0_ffn2_w_gate, l0_ffn2_w_up, l0_ffn2_w_down)
    x2 = ffn(x2, l1_ffn1_norm, l1_ffn1_w_gate, l1_ffn1_w_up, l1_ffn1_w_down)
    x2 = _odd_mixer(x2, bsz, seq, 1, hgrn_lb_logits, l1_mix_norm, l1_w_in, l1_gla_gate_w2, l1_gla_gate_b,
                    l1_gla_norm_w, l1_hg_norm_w, l1_w_out)
    x2 = ffn(x2, l1_ffn2_norm, l1_ffn2_w_gate, l1_ffn2_w_up, l1_ffn2_w_down, final=True)
    return x2.reshape(bsz, seq, d)
```

```python
import functools

import jax
import jax.numpy as jnp
from jax import lax
from jax.experimental import pallas as pl
from jax.experimental.pallas import tpu as pltpu

F32 = jnp.float32
BF16 = jnp.bfloat16

D_MODEL = 2048
CHUNK = 64
NORM_EPS = 1e-6
D_FF = 5632
FFN_RES = 0.5
DN_HEADS = 8
DN_HEAD_DIM = 128
DN_WIDTH = DN_HEADS * DN_HEAD_DIM
DN_CONV = 4
S5_WIDTH = D_MODEL - DN_WIDTH
S5_GROUP = 16
S5_GROUPS = S5_WIDTH // S5_GROUP
S5_STATE = 64
GLA_HEADS = 4
GLA_WIDTH = D_MODEL // 2
GLA_KEY_WIDTH = GLA_WIDTH // 2
GLA_HEAD_K = GLA_KEY_WIDTH // GLA_HEADS
GLA_HEAD_V = GLA_WIDTH // GLA_HEADS
GLA_GATE_RANK = 16
GLA_TAU = 16.0
HG_WIDTH = D_MODEL - GLA_WIDTH
HG_EXPAND = 128
HG_HEADS = HG_WIDTH // HG_EXPAND

LANES = 128
SUBLANES = 8
BF16_ROWS = 16
VMEM_LIMIT = 56 * 1024 * 1024

S5_TILE_GROUPS = LANES // S5_GROUP
S5_TILE_STATE = S5_TILE_GROUPS * S5_STATE
S5_TILES = S5_WIDTH // LANES

DN_PAIR = 2
DN_GROUP = 2
DN_ROWS = DN_GROUP * CHUNK
DN_GROUPS_PER_ITER = 8
N_SPLIT = 3
SEL_ROWS = 16
GLA_GROUP = 16


def _params(sem):
    return pltpu.CompilerParams(dimension_semantics=sem, vmem_limit_bytes=VMEM_LIMIT)


def _rms(x, w):
    return x * lax.rsqrt(jnp.mean(x * x, axis=-1, keepdims=True) + NORM_EPS) * w


def _silu(x):
    return x * jax.nn.sigmoid(x)


def _softplus(x):
    return jnp.maximum(x, 0.0) + jnp.log1p(jnp.exp(-jnp.abs(x)))


def _dot(a, b):
    return jnp.dot(a.astype(BF16), b.astype(BF16), preferred_element_type=F32)


def _dot_nt(a, b):
    return lax.dot_general(a.astype(BF16), b.astype(BF16), (((1,), (1,)), ((), ())), preferred_element_type=F32)


def _dot_tn(a, b):
    return lax.dot_general(a.astype(BF16), b.astype(BF16), (((0,), (0,)), ((), ())), preferred_element_type=F32)


def _chunk_cumsum(x):
    pos = lax.broadcasted_iota(jnp.int32, x.shape, 0) % CHUNK
    shift = 1
    while shift < CHUNK:
        x = x + jnp.where(pos >= shift, pltpu.roll(x, shift, axis=0), 0.0)
        shift *= 2
    return x


def _tri_masks(n=CHUNK):
    row = lax.broadcasted_iota(jnp.int32, (n, n), 0)
    col = lax.broadcasted_iota(jnp.int32, (n, n), 1)
    same = (row // CHUNK) == (col // CHUNK)
    return same & (row >= col), same & (row > col)


def _ffn_kernel(x_ref, nw_ref, wg_ref, wu_ref, wd_ref, fw_ref, o_ref, xn_ref, *, n_ff_tiles, final):
    j = pl.program_id(1)

    @pl.when(j == 0)
    def _():
        x = x_ref[...]
        xn_ref[...] = _rms(x, nw_ref[...]).astype(BF16)
        o_ref[...] = x

    xn = xn_ref[...]
    g = jnp.dot(xn, wg_ref[...], preferred_element_type=F32)
    u = jnp.dot(xn, wu_ref[...], preferred_element_type=F32)
    h = (FFN_RES * (_silu(g) * u)).astype(BF16)
    o_ref[...] += jnp.dot(h, wd_ref[...], preferred_element_type=F32)

    if final:
        @pl.when(j == n_ff_tiles - 1)
        def _():
            o_ref[...] = _rms(o_ref[...], fw_ref[...])


def _ffn(x2, nw, wg, wu, wd, fw, *, final, tm=1024, tf=512):
    t, d = x2.shape
    f = wg.shape[1]
    return pl.pallas_call(
        functools.partial(_ffn_kernel, n_ff_tiles=f // tf, final=final),
        grid=(t // tm, f // tf),
        in_specs=[pl.BlockSpec((tm, d), lambda i, j: (i, 0)),
                  pl.BlockSpec((1, d), lambda i, j: (0, 0)),
                  pl.BlockSpec((d, tf), lambda i, j: (0, j)),
                  pl.BlockSpec((d, tf), lambda i, j: (0, j)),
                  pl.BlockSpec((tf, d), lambda i, j: (j, 0)),
                  pl.BlockSpec((1, d), lambda i, j: (0, 0))],
        out_specs=pl.BlockSpec((tm, d), lambda i, j: (i, 0)),
        out_shape=jax.ShapeDtypeStruct((t, d), F32),
        scratch_shapes=[pltpu.VMEM((tm, d), BF16)],
        compiler_params=_params(("parallel", "arbitrary")),
        name="ffn",
    )(x2, nw.reshape(1, d), wg, wu, wd, fw.reshape(1, d))


def _inproj_kernel(x_ref, nw_ref, w_ref, ws_ref, o_ref, os_ref, xn_ref):
    @pl.when(pl.program_id(1) == 0)
    def _():
        xn = _rms(x_ref[...], nw_ref[...]).astype(BF16)
        xn_ref[...] = xn
        os_ref[...] = jnp.dot(xn, ws_ref[...], preferred_element_type=F32)

    o_ref[...] = jnp.dot(xn_ref[...], w_ref[...], preferred_element_type=F32)


def _inproj(x2, nw, w_main, w_small, *, tm=1024, tn=1024):
    t, d = x2.shape
    n = w_main.shape[1]
    return pl.pallas_call(
        _inproj_kernel,
        grid=(t // tm, n // tn),
        in_specs=[pl.BlockSpec((tm, d), lambda i, j: (i, 0)),
                  pl.BlockSpec((1, d), lambda i, j: (0, 0)),
                  pl.BlockSpec((d, tn), lambda i, j: (0, j)),
                  pl.BlockSpec((d, LANES), lambda i, j: (0, 0))],
        out_specs=[pl.BlockSpec((tm, tn), lambda i, j: (i, j)),
                   pl.BlockSpec((tm, LANES), lambda i, j: (i, 0))],
        out_shape=[jax.ShapeDtypeStruct((t, n), F32), jax.ShapeDtypeStruct((t, LANES), F32)],
        scratch_shapes=[pltpu.VMEM((tm, d), BF16)],
        compiler_params=_params(("parallel", "arbitrary")),
        name="inproj",
    )(x2, nw.reshape(1, d), w_main, w_small)


def _outproj_kernel(x_ref, ya_ref, yb_ref, wa_ref, wb_ref, o_ref):
    o_ref[...] = (x_ref[...] + jnp.dot(ya_ref[...], wa_ref[...], preferred_element_type=F32)
                  + jnp.dot(yb_ref[...], wb_ref[...], preferred_element_type=F32))


def _outproj(x2, ya, yb, w, *, tm=512):
    t, d = x2.shape
    ka = ya.shape[1]
    kb = yb.shape[1]
    assert ka == kb
    return pl.pallas_call(
        _outproj_kernel,
        grid=(t // tm,),
        in_specs=[pl.BlockSpec((tm, d), lambda i: (i, 0)),
                  pl.BlockSpec((tm, ka), lambda i: (i, 0)),
                  pl.BlockSpec((tm, kb), lambda i: (i, 0)),
                  pl.BlockSpec((ka, d), lambda i: (0, 0)),
                  pl.BlockSpec((kb, d), lambda i: (1, 0))],
        out_specs=pl.BlockSpec((tm, d), lambda i: (i, 0)),
        out_shape=jax.ShapeDtypeStruct((t, d), F32),
        compiler_params=_params(("parallel",)),
        name="outproj",
    )(x2, ya, yb, w, w)


def _causal_conv(x, w, xpad):
    seq = x.shape[0]
    xpad[SUBLANES:, :] = x
    y = x * w[DN_CONV - 1:DN_CONV, :]
    for back in range(1, DN_CONV):
        y = y + xpad[pl.ds(SUBLANES - back, seq), :] * w[DN_CONV - 1 - back:DN_CONV - back, :]
    return y


def _split3(x):
    parts = []
    for _ in range(N_SPLIT):
        p = x.astype(BF16).astype(F32)
        parts.append(p)
        x = x - p
    return parts


def _l2norm(x):
    return x * lax.rsqrt(jnp.sum(x * x, axis=-1, keepdims=True) + NORM_EPS)


def _deltanet_kernel(q_ref, k_ref, v_ref, z_ref, at_ref, bt_ref, cwq_ref, cwk_ref, cwv_ref,
                     alog1_ref, dtb1_ref, nw_ref, cum_ref, sel_ref, o_ref,
                     xpad, qs, ks, vs, gts, parts_s, qm_s, n_s, op_s, cd_s):
    seq = q_ref.shape[1]
    n_chunks = seq // CHUNK
    d = DN_HEAD_DIM
    incl, strict = _tri_masks(DN_ROWS)
    nw = nw_ref[...]
    sel = sel_ref[...]
    xpad[:SUBLANES, :] = jnp.zeros((SUBLANES, d), F32)

    for hh in range(DN_PAIR):
        cols = slice(hh * d, (hh + 1) * d)
        qs[...] = _l2norm(_silu(_causal_conv(q_ref[0, :, cols], cwq_ref[:, cols], xpad))) * DN_HEAD_DIM ** -0.5
        ks[...] = _l2norm(_silu(_causal_conv(k_ref[0, :, cols], cwk_ref[:, cols], xpad)))
        vs[...] = _silu(_causal_conv(v_ref[0, :, cols], cwv_ref[:, cols], xpad))
        g_t = -jnp.exp(alog1_ref[hh]) * _softplus(at_ref[0, hh] + dtb1_ref[hh])
        gc_t = jnp.dot(g_t, cum_ref[...], preferred_element_type=F32, precision=lax.Precision.HIGHEST)
        gts[...] = gc_t
        for i, part in enumerate(_split3(gc_t) + _split3(jax.nn.sigmoid(bt_ref[0, hh]))):
            parts_s[i] = part

        def prep(it, carry):
            groups = [it * DN_GROUPS_PER_ITER + g for g in range(DN_GROUPS_PER_ITER)]
            rows = [pl.ds(pl.multiple_of(gi * DN_ROWS, DN_ROWS), DN_ROWS) for gi in groups]
            q = [qs[r, :] for r in rows]
            k = [ks[r, :] for r in rows]
            src = [jnp.concatenate([parts_s[i, pl.ds(gi, 1), :] for i in range(2 * N_SPLIT)]
                                   + [jnp.zeros((SEL_ROWS - 2 * N_SPLIT, DN_ROWS), F32)], axis=0) for gi in groups]
            colb = [_dot_tn(s, sel) for s in src]
            gcb = [c[:, :d] for c in colb]
            beta = [c[:, d:] for c in colb]
            decay = [jnp.exp(jnp.where(incl, jnp.concatenate([gc] * (DN_ROWS // d), axis=1) - gts[pl.ds(gi, 1), :],
                                       -jnp.inf)) for gc, gi in zip(gcb, groups)]
            kb = [ki * bi for ki, bi in zip(k, beta)]
            a = [jnp.where(strict, _dot_nt(kbi, ki) * dc, 0.0) for kbi, ki, dc in zip(kb, k, decay)]
            eg = [jnp.exp(gc) for gc in gcb]
            x = [jnp.concatenate([kbi * egi, vs[r, :] * bi], axis=1) for kbi, egi, r, bi in zip(kb, eg, rows, beta)]
            pb = [ai.astype(BF16) for ai in a]
            x = [xi - _dot(pi, xi) for pi, xi in zip(pb, x)]
            for _ in range(CHUNK.bit_length() - 2):
                pb = [jnp.dot(pi, pi, preferred_element_type=F32).astype(BF16) for pi in pb]
                x = [xi + _dot(pi, xi) for pi, xi in zip(pb, x)]
            xb = [xi.astype(BF16) for xi in x]
            qk = [jnp.where(incl, _dot_nt(qi, ki) * dc, 0.0) for qi, ki, dc in zip(q, k, decay)]
            qx = [_dot(qki, xi) for qki, xi in zip(qk, xb)]
            for g in range(DN_GROUPS_PER_ITER):
                last = [gcb[g][(i + 1) * CHUNK - 1:(i + 1) * CHUNK, :] for i in range(DN_GROUP)]
                g_last = jnp.concatenate([jnp.broadcast_to(li, (CHUNK, d)) for li in last], axis=0)
                kd = (k[g] * jnp.exp(g_last - gcb[g])).astype(BF16)
                qp = (q[g] * eg[g] - qx[g][:, :d]).astype(BF16)
                for i in range(DN_GROUP):
                    c = groups[g] * DN_GROUP + i
                    cr = slice(i * CHUNK, (i + 1) * CHUNK)
                    kx = _dot_tn(kd[cr, :], xb[g][cr, :])
                    qm_s[hh, c, :CHUNK, :] = qp[cr, :]
                    qm_s[hh, c, CHUNK:, :] = kx[:, :d].astype(BF16)
                    n_s[hh, c] = kx[:, d:]
                    cd_s[hh, pl.ds(c, 1), :] = jnp.exp(last[i])
                op_s[hh, rows[g], :] = qx[g][:, d:]
            return carry

        lax.fori_loop(0, seq // (DN_ROWS * DN_GROUPS_PER_ITER), prep, 0)

    def step(c, states):
        r = pl.ds(pl.multiple_of(c * CHUNK, CHUNK), CHUNK)
        new = []
        for hh in range(DN_PAIR):
            cols = slice(hh * d, (hh + 1) * d)
            prod = jnp.dot(qm_s[hh, c], states[hh].astype(BF16), preferred_element_type=F32)
            o = prod[:CHUNK, :] + op_s[hh, r, :]
            o_ref[0, r, cols] = (_rms(o, nw) * _silu(z_ref[0, r, cols])).astype(o_ref.dtype)
            new.append(cd_s[hh, pl.ds(c, 1), :] * states[hh] + n_s[hh, c] - prod[CHUNK:, :])
        return tuple(new)

    lax.fori_loop(0, n_chunks, step, tuple(jnp.zeros((d, d), F32) for _ in range(DN_PAIR)))


def _deltanet(main, small, conv_w, a_log, dt_bias, norm_w):
    bsz, seq, _ = main.shape
    n_chunks = seq // CHUNK
    d = DN_HEAD_DIM
    pw = DN_PAIR * d
    n_pairs = DN_HEADS // DN_PAIR
    n_groups = seq // DN_ROWS
    assert seq % (DN_ROWS * DN_GROUPS_PER_ITER) == 0 and DN_ROWS % d == 0
    ab_t = small[:, :, :2 * DN_HEADS].transpose(0, 2, 1).reshape(bsz, 2 * DN_HEADS, n_groups, DN_ROWS)
    cw = conv_w.reshape(DN_CONV, 3 * DN_WIDTH)
    cum = jnp.kron(jnp.eye(DN_GROUP, dtype=F32), jnp.triu(jnp.ones((CHUNK, CHUNK), F32)))
    sel = jnp.kron((jnp.arange(SEL_ROWS)[:, None] // N_SPLIT == jnp.arange(2)[None, :]).astype(BF16),
                   jnp.ones((1, d), BF16))
    col = lambda off: pl.BlockSpec((1, seq, pw), lambda b, h: (b, 0, off + h))
    cwspec = lambda off: pl.BlockSpec((DN_CONV, pw), lambda b, h: (0, off + h))
    tspec = lambda off: pl.BlockSpec((1, DN_PAIR, n_groups, DN_ROWS), lambda b, h: (b, off + h, 0, 0))
    one = pl.BlockSpec((DN_PAIR, 1, 1), lambda b, h: (h, 0, 0))
    return pl.pallas_call(
        _deltanet_kernel,
        grid=(bsz, n_pairs),
        in_specs=[col(0), col(n_pairs), col(2 * n_pairs), col(3 * n_pairs),
                  tspec(0), tspec(n_pairs),
                  cwspec(0), cwspec(n_pairs), cwspec(2 * n_pairs),
                  one, one,
                  pl.BlockSpec((1, d), lambda b, h: (0, 0)),
                  pl.BlockSpec((DN_ROWS, DN_ROWS), lambda b, h: (0, 0)),
                  pl.BlockSpec((SEL_ROWS, 2 * d), lambda b, h: (0, 0))],
        out_specs=pl.BlockSpec((1, seq, pw), lambda b, h: (b, 0, h)),
        out_shape=jax.ShapeDtypeStruct((bsz, seq, DN_WIDTH), BF16),
        scratch_shapes=[pltpu.VMEM((seq + SUBLANES, d), F32)] + [pltpu.VMEM((seq, d), F32)] * 3 + [
                        pltpu.VMEM((n_groups, DN_ROWS), F32),
                        pltpu.VMEM((2 * N_SPLIT, n_groups, DN_ROWS), F32),
                        pltpu.VMEM((DN_PAIR, n_chunks, CHUNK + d, d), BF16),
                        pltpu.VMEM((DN_PAIR, n_chunks, d, d), F32),
                        pltpu.VMEM((DN_PAIR, seq, d), F32),
                        pltpu.VMEM((DN_PAIR, n_chunks, d), F32)],
        compiler_params=_params(("parallel", "arbitrary")),
        name="deltanet",
    )(main, main, main, main, ab_t, ab_t, cw, cw, cw,
      a_log.reshape(DN_HEADS, 1, 1), dt_bias.reshape(DN_HEADS, 1, 1), norm_w.reshape(1, d), cum, sel)


def _gelu_tanh(x):
    return x * (0.5 * (1.0 + jnp.tanh(0.7978845608028654 * (x + 0.044715 * (x * x * x)))))


def _s5_kernel(u_ref, are_ref, aim_ref, ls_ref, bre_ref, bim_ref, cre_ref, cim_ref, d_ref, o_ref,
               ut, xs, xb, yt, state, bbd, cbd, lam):
    ns = S5_TILE_STATE
    bsz, ts, _ = u_ref.shape

    @pl.when(pl.program_id(1) == 0)
    def _():
        a_re = are_ref[0]
        a_im = aim_ref[0]
        dt = jnp.exp(ls_ref[0])
        mag = jnp.exp(a_re * dt)
        l_re = mag * jnp.cos(a_im * dt)
        l_im = mag * jnp.sin(a_im * dt)
        den = a_re * a_re + a_im * a_im
        c_re = ((l_re - 1.0) * a_re + l_im * a_im) / den
        c_im = (l_im * a_re - (l_re - 1.0) * a_im) / den
        b_re = bre_ref[0]
        b_im = bim_ref[0]
        bbd[:, :ns] = (c_re * b_re - c_im * b_im).astype(BF16)
        bbd[:, ns:] = (c_re * b_im + c_im * b_re).astype(BF16)
        cbd[:ns, :] = cre_ref[0].astype(BF16)
        cbd[ns:, :] = (-cim_ref[0]).astype(BF16)
        lam[:, :ns] = jnp.broadcast_to(l_re, (bsz, ns))
        lam[:, ns:] = jnp.broadcast_to(l_im, (bsz, ns))
        state[...] = jnp.zeros_like(state)

    for b in range(bsz):
        ut[pl.ds(b, ts, stride=bsz), :] = u_ref[b]
    xs[...] = jnp.dot(ut[...].astype(BF16), bbd[...], preferred_element_type=F32)
    l_re = lam[:, :ns]
    l_im = lam[:, ns:]

    frames = BF16_ROWS // bsz

    def step(i, carry):
        x_re, x_im = carry
        r = pl.ds(pl.multiple_of(i * BF16_ROWS, BF16_ROWS), BF16_ROWS)
        bu = xs[r, :]
        res_re, res_im = [], []
        for f in range(frames):
            rows = slice(f * bsz, (f + 1) * bsz)
            x_re, x_im = (l_re * x_re - l_im * x_im + bu[rows, :ns], l_re * x_im + l_im * x_re + bu[rows, ns:])
            res_re.append(x_re)
            res_im.append(x_im)
        xb[r, :ns] = jnp.concatenate(res_re, axis=0).astype(BF16)
        xb[r, ns:] = jnp.concatenate(res_im, axis=0).astype(BF16)
        return x_re, x_im

    x_re, x_im = lax.fori_loop(0, ts // frames, step, (state[:, :ns], state[:, ns:]))
    state[:, :ns] = x_re
    state[:, ns:] = x_im
    y = jnp.dot(xb[...], cbd[...], preferred_element_type=F32) + d_ref[0] * ut[...]
    yt[...] = _gelu_tanh(y)
    for b in range(bsz):
        o_ref[b] = yt[pl.ds(b, ts, stride=bsz), :]


def _s5(main, u_col_block, a_re, a_im, b_re, b_im, c_re, c_im, d, log_step, *, ts=256):
    bsz, seq, _ = main.shape
    assert bsz == SUBLANES
    tg, ns, nt = S5_TILE_GROUPS, S5_TILE_STATE, S5_TILES
    eye = jnp.eye(tg, dtype=F32)

    def expand_b(b):
        bt = b.reshape(nt, tg, S5_STATE, S5_GROUP)
        return jnp.einsum('ngph,gk->nghkp', bt, eye).reshape(nt, LANES, ns)

    def expand_c(c):
        ct = c.reshape(nt, tg, S5_GROUP, S5_STATE)
        return jnp.einsum('nghp,gk->ngpkh', ct, eye).reshape(nt, ns, LANES)

    chan = lambda a: a.reshape(nt, 1, ns)
    ls = jnp.broadcast_to(log_step[:, None], (S5_GROUPS, S5_STATE))
    pspec = pl.BlockSpec((1, 1, ns), lambda c, t: (c, 0, 0))
    bspec = pl.BlockSpec((1, LANES, ns), lambda c, t: (c, 0, 0))
    cspec = pl.BlockSpec((1, ns, LANES), lambda c, t: (c, 0, 0))
    return pl.pallas_call(
        _s5_kernel,
        grid=(nt, seq // ts),
        in_specs=[pl.BlockSpec((bsz, ts, LANES), lambda c, t: (0, t, u_col_block + c)),
                  pspec, pspec, pspec, bspec, bspec, cspec, cspec,
                  pl.BlockSpec((1, 1, LANES), lambda c, t: (c, 0, 0))],
        out_specs=pl.BlockSpec((bsz, ts, LANES), lambda c, t: (0, t, c)),
        out_shape=jax.ShapeDtypeStruct((bsz, seq, S5_WIDTH), F32),
        scratch_shapes=[pltpu.VMEM((bsz * ts, LANES), F32),
                        pltpu.VMEM((bsz * ts, 2 * ns), F32),
                        pltpu.VMEM((bsz * ts, 2 * ns), BF16),
                        pltpu.VMEM((bsz * ts, LANES), F32),
                        pltpu.VMEM((bsz, 2 * ns), F32),
                        pltpu.VMEM((LANES, 2 * ns), BF16),
                        pltpu.VMEM((2 * ns, LANES), BF16),
                        pltpu.VMEM((bsz, 2 * ns), F32)],
        compiler_params=_params(("parallel", "arbitrary")),
        name="s5",
    )(main, chan(a_re), chan(a_im), chan(ls), expand_b(b_re), expand_b(b_im), expand_c(c_re), expand_c(c_im),
      d.reshape(nt, 1, LANES))


def _glu_kernel(y_ref, w_ref, b_ref, o_ref):
    y = y_ref[...]
    gate = jnp.dot(y.astype(BF16), w_ref[...], preferred_element_type=F32) + b_ref[...]
    o_ref[...] = (y * jax.nn.sigmoid(gate)).astype(o_ref.dtype)


def _glu(y2, w, b, *, tm=512):
    t, n = y2.shape
    return pl.pallas_call(
        _glu_kernel,
        grid=(t // tm,),
        in_specs=[pl.BlockSpec((tm, n), lambda i: (i, 0)),
                  pl.BlockSpec((n, n), lambda i: (0, 0)),
                  pl.BlockSpec((1, n), lambda i: (0, 0))],
        out_specs=pl.BlockSpec((tm, n), lambda i: (i, 0)),
        out_shape=jax.ShapeDtypeStruct((t, n), BF16),
        compiler_params=_params(("parallel",)),
        name="s5_glu",
    )(y2, w, b.reshape(1, n))


def _gla_scan(q_at, k_at, v_at, bs, gate_ref, nw_ref, o_ref, st_s):
    n_chunks, dv, dk = st_s.shape
    incl, _ = _tri_masks()
    nw = nw_ref[...]

    def chunk_rows(it):
        cs = [it * GLA_GROUP + i for i in range(GLA_GROUP)]
        return cs, [pl.ds(pl.multiple_of(c * CHUNK, CHUNK), CHUNK) for c in cs]

    def states(it, state_t):
        cs, rows = chunk_rows(it)
        b = [bs[r, :] for r in rows]
        b_last = [bi[CHUNK - 1:CHUNK, :] for bi in b]
        inc = [_dot_tn(v_at(r), k_at(r) * jnp.exp(bl - bi)) for r, bl, bi in zip(rows, b_last, b)]
        for c, bl, ic in zip(cs, b_last, inc):
            st_s[c] = state_t.astype(BF16)
            state_t = jnp.exp(bl) * state_t + ic
        return state_t

    lax.fori_loop(0, n_chunks // GLA_GROUP, states, jnp.zeros((dv, dk), F32))

    def outputs(it, carry):
        cs, rows = chunk_rows(it)
        q = [q_at(r) for r in rows]
        b = [bs[r, :] for r in rows]
        b_mid = [bi[CHUNK // 2 - 1:CHUNK // 2, :] for bi in b]
        att = [jnp.where(incl, _dot_nt(qi * jnp.exp(bi - bm), k_at(r) * jnp.exp(bm - bi)), 0.0)
               for qi, bi, bm, r in zip(q, b, b_mid, rows)]
        o_inter = [_dot_nt(qi * jnp.exp(bi), st_s[c]) for qi, bi, c in zip(q, b, cs)]
        o = [_dot(ai, v_at(r)) + oi for ai, r, oi in zip(att, rows, o_inter)]
        for r, oi in zip(rows, o):
            o_ref[0, r, :] = (_rms(oi, nw) * _silu(gate_ref[0, r, :])).astype(o_ref.dtype)
        return carry

    lax.fori_loop(0, n_chunks // GLA_GROUP, outputs, 0)


def _gla_kernel(q_ref, k_ref, v_ref, r_ref, lr_ref, w2_ref, gb_ref, nw_ref, o_ref, bs, st_s):
    gate = _dot(lr_ref[0], w2_ref[...]) + gb_ref[...]
    log_a = -_softplus(-gate) / GLA_TAU
    bs[...] = _chunk_cumsum(log_a)
    _gla_scan(lambda r: q_ref[0, r, :] * GLA_HEAD_K ** -0.5, lambda r: k_ref[0, r, :], lambda r: v_ref[0, r, :],
              bs, r_ref, nw_ref, o_ref, st_s)


def _hgrn2_kernel(q_ref, f_ref, i_ref, g_ref, lbl_ref, nw_ref, o_ref, ks, bs, st_s, *, layer):
    logits = lbl_ref[...]
    e = jnp.exp(logits - jnp.max(logits, axis=0, keepdims=True))
    p = e / jnp.sum(e, axis=0, keepdims=True)
    lb_first = p[0:1, :]
    lb_layer = lb_first
    for i in range(1, layer + 1):
        lb_layer = lb_layer + p[i:i + 1, :]
    lb = lb_layer - lb_first
    z_f = f_ref[0]
    e_z = jnp.exp(-jnp.abs(z_f))
    big = 1.0 / (1.0 + e_z)
    small = e_z * big
    sig_pos = jnp.where(z_f >= 0.0, big, small)
    sig_neg = jnp.where(z_f >= 0.0, small, big)
    ks[...] = (1.0 - lb) * sig_neg
    bs[...] = _chunk_cumsum(jnp.log(lb + (1.0 - lb) * sig_pos))
    _gla_scan(lambda r: q_ref[0, r, :], lambda r: ks[r, :], lambda r: i_ref[0, r, :],
              bs, g_ref, nw_ref, o_ref, st_s)


def _gla(main, small, w2, gate_b, norm_w):
    bsz, seq, _ = main.shape
    dk, dv = GLA_HEAD_K, GLA_HEAD_V
    w2p = jnp.concatenate([w2, jnp.zeros((LANES - GLA_GATE_RANK, GLA_KEY_WIDTH), F32)], axis=0).astype(BF16)
    kspec = lambda off: pl.BlockSpec((1, seq, dk), lambda b, h: (b, 0, off + h))
    vspec = lambda off: pl.BlockSpec((1, seq, dv), lambda b, h: (b, 0, off + h))
    return pl.pallas_call(
        _gla_kernel,
        grid=(bsz, GLA_HEADS),
        in_specs=[kspec(0), kspec(GLA_HEADS), vspec(GLA_HEADS), vspec(2 * GLA_HEADS),
                  pl.BlockSpec((1, seq, LANES), lambda b, h: (b, 0, 0)),
                  pl.BlockSpec((LANES, dk), lambda b, h: (0, h)),
                  pl.BlockSpec((1, dk), lambda b, h: (0, h)),
                  pl.BlockSpec((1, dv), lambda b, h: (0, 0))],
        out_specs=pl.BlockSpec((1, seq, dv), lambda b, h: (b, 0, h)),
        out_shape=jax.ShapeDtypeStruct((bsz, seq, GLA_WIDTH), BF16),
        scratch_shapes=[pltpu.VMEM((seq, dk), F32), pltpu.VMEM((seq // CHUNK, dv, dk), BF16)],
        compiler_params=_params(("parallel", "arbitrary")),
        name="gla",
    )(main, main, main, main, small, w2p, gate_b.reshape(1, GLA_KEY_WIDTH), norm_w.reshape(1, dv))


def _hgrn2(main, lb_logits, layer, norm_w):
    bsz, seq, _ = main.shape
    d = HG_EXPAND
    base = (2 * GLA_KEY_WIDTH + 2 * GLA_WIDTH) // d
    col = lambda off: pl.BlockSpec((1, seq, d), lambda b, h: (b, 0, base + off + h))
    depth = lb_logits.shape[0]
    return pl.pallas_call(
        functools.partial(_hgrn2_kernel, layer=layer),
        grid=(bsz, HG_HEADS),
        in_specs=[col(0), col(HG_HEADS), col(2 * HG_HEADS), col(3 * HG_HEADS),
                  pl.BlockSpec((depth, d), lambda b, h: (0, h)),
                  pl.BlockSpec((1, d), lambda b, h: (0, 0))],
        out_specs=pl.BlockSpec((1, seq, d), lambda b, h: (b, 0, h)),
        out_shape=jax.ShapeDtypeStruct((bsz, seq, HG_WIDTH), BF16),
        scratch_shapes=[pltpu.VMEM((seq, d), F32)] * 2 + [pltpu.VMEM((seq // CHUNK, d, d), BF16)],
        compiler_params=_params(("parallel", "arbitrary")),
        name="hgrn2",
    )(main, main, main, main, lb_logits, norm_w.reshape(1, d))


def _pad_cols(w):
    return jnp.concatenate([w, jnp.zeros((w.shape[0], LANES - w.shape[1]), w.dtype)], axis=1)


def _even_mixer(x2, bsz, seq, mix_norm, w_in, conv_w, a_log, dt_bias, dn_norm_w, s5_a_re, s5_a_im, s5_b_re, s5_b_im,
                s5_c_re, s5_c_im, s5_d, s5_log_step, s5_glu_w, s5_glu_b, w_out):
    n_qkvz = 4 * DN_WIDTH
    n_ab = 2 * DN_HEADS
    w_in = w_in.astype(BF16)
    w_main = jnp.concatenate([w_in[:, :n_qkvz], w_in[:, n_qkvz + n_ab:]], axis=1)
    w_small = _pad_cols(w_in[:, n_qkvz:n_qkvz + n_ab])
    main, small = _inproj(x2, mix_norm, w_main, w_small)
    main = main.reshape(bsz, seq, -1)
    small = small.reshape(bsz, seq, LANES)
    y_a = _deltanet(main, small, conv_w, a_log, dt_bias, dn_norm_w)
    y_s5 = _s5(main, n_qkvz // LANES, s5_a_re, s5_a_im, s5_b_re, s5_b_im, s5_c_re, s5_c_im, s5_d, s5_log_step)
    y_b = _glu(y_s5.reshape(bsz * seq, S5_WIDTH), s5_glu_w.astype(BF16), s5_glu_b)
    return _outproj(x2, y_a.reshape(bsz * seq, DN_WIDTH), y_b, w_out.astype(BF16))


def _odd_mixer(x2, bsz, seq, layer, lb_logits, mix_norm, w_in, gate_w2, gate_b, gla_norm_w, hg_norm_w, w_out):
    n_c = 2 * GLA_KEY_WIDTH + 2 * GLA_WIDTH
    w_in = w_in.astype(BF16)
    w_main = jnp.concatenate([w_in[:, :n_c], w_in[:, n_c + GLA_GATE_RANK:]], axis=1)
    w_small = _pad_cols(w_in[:, n_c:n_c + GLA_GATE_RANK])
    main, small = _inproj(x2, mix_norm, w_main, w_small)
    main = main.reshape(bsz, seq, -1)
    small = small.reshape(bsz, seq, LANES)
    y_c = _gla(main, small, gate_w2, gate_b, gla_norm_w)
    y_d = _hgrn2(main, lb_logits, layer, hg_norm_w)
    return _outproj(x2, y_c.reshape(bsz * seq, GLA_WIDTH), y_d.reshape(bsz * seq, HG_WIDTH), w_out.astype(BF16))


def kernel(x, l0_ffn1_norm, l0_ffn1_w_gate, l0_ffn1_w_up, l0_ffn1_w_down, l0_mix_norm, l0_w_in, l0_dn_conv_w, l0_dn_a_log, l0_dn_dt_bias, l0_dn_norm_w, l0_s5_a_re, l0_s5_a_im, l0_s5_b_re, l0_s5_b_im, l0_s5_c_re, l0_s5_c_im, l0_s5_d, l0_s5_log_step, l0_s5_glu_w, l0_s5_glu_b, l0_w_out, l0_ffn2_norm, l0_ffn2_w_gate, l0_ffn2_w_up, l0_ffn2_w_down, l1_ffn1_norm, l1_ffn1_w_gate, l1_ffn1_w_up, l1_ffn1_w_down, l1_mix_norm, l1_w_in, l1_gla_gate_w2, l1_gla_gate_b, l1_gla_norm_w, l1_hg_norm_w, l1_w_out, l1_ffn2_norm, l1_ffn2_w_gate, l1_ffn2_w_up, l1_ffn2_w_down, hgrn_lb_logits, final_norm):
    bsz, seq, d = x.shape
    x2 = x.reshape(bsz * seq, d)

    def ffn(x2, nw, wg, wu, wd, final=False):
        return _ffn(x2, nw, wg.astype(BF16), wu.astype(BF16), wd.astype(BF16), final_norm, final=final)

    x2 = ffn(x2, l0_ffn1_norm, l0_ffn1_w_gate, l0_ffn1_w_up, l0_ffn1_w_down)
    x2 = _even_mixer(x2, bsz, seq, l0_mix_norm, l0_w_in, l0_dn_conv_w, l0_dn_a_log, l0_dn_dt_bias, l0_dn_norm_w,
                     l0_s5_a_re, l0_s5_a_im, l0_s5_b_re, l0_s5_b_im, l0_s5_c_re, l0_s5_c_im, l0_s5_d, l0_s5_log_step,
                     l0_s5_glu_w, l0_s5_glu_b, l0_w_out)
    x2 = ffn(x2, l0_ffn2_norm, l0_ffn2_w_gate, l0_ffn2_w_up, l0_ffn2_w_down)
    x2 = ffn(x2, l1_ffn1_norm, l1_ffn1_w_gate, l1_ffn1_w_up, l1_ffn1_w_down)
    x2 = _odd_mixer(x2, bsz, seq, 1, hgrn_lb_logits, l1_mix_norm, l1_w_in, l1_gla_gate_w2, l1_gla_gate_b,
                    l1_gla_norm_w, l1_hg_norm_w, l1_w_out)
    x2 = ffn(x2, l1_ffn2_norm, l1_ffn2_w_gate, l1_ffn2_w_up, l1_ffn2_w_down, final=True)
    return x2.reshape(bsz, seq, d)
```

```python
import functools

import jax
import jax.numpy as jnp
from jax import lax
from jax.experimental import pallas as pl
from jax.experimental.pallas import tpu as pltpu

F32 = jnp.float32
BF16 = jnp.bfloat16

D_MODEL = 2048
CHUNK = 64
NORM_EPS = 1e-6
D_FF = 5632
FFN_RES = 0.5
DN_HEADS = 8
DN_HEAD_DIM = 128
DN_WIDTH = DN_HEADS * DN_HEAD_DIM
DN_CONV = 4
S5_WIDTH = D_MODEL - DN_WIDTH
S5_GROUP = 16
S5_GROUPS = S5_WIDTH // S5_GROUP
S5_STATE = 64
GLA_HEADS = 4
GLA_WIDTH = D_MODEL // 2
GLA_KEY_WIDTH = GLA_WIDTH // 2
GLA_HEAD_K = GLA_KEY_WIDTH // GLA_HEADS
GLA_HEAD_V = GLA_WIDTH // GLA_HEADS
GLA_GATE_RANK = 16
GLA_TAU = 16.0
HG_WIDTH = D_MODEL - GLA_WIDTH
HG_EXPAND = 128
HG_HEADS = HG_WIDTH // HG_EXPAND

LANES = 128
SUBLANES = 8
BF16_ROWS = 16
VMEM_LIMIT = 56 * 1024 * 1024
FFN_VMEM_LIMIT = 58 * 1024 * 1024

S5_TILE_GROUPS = LANES // S5_GROUP
S5_TILE_STATE = S5_TILE_GROUPS * S5_STATE
S5_TILES = S5_WIDTH // LANES

DN_PAIR = 2
DN_GROUP = 2
DN_ROWS = DN_GROUP * CHUNK
DN_GROUPS_PER_ITER = 8
N_SPLIT = 3
SEL_ROWS = 16
GLA_GROUP = 16


def _params(sem, vmem_limit=VMEM_LIMIT):
    return pltpu.CompilerParams(dimension_semantics=sem, vmem_limit_bytes=vmem_limit)


def _rms(x, w):
    return x * lax.rsqrt(jnp.mean(x * x, axis=-1, keepdims=True) + NORM_EPS) * w


def _silu(x):
    return x * jax.nn.sigmoid(x)


def _softplus(x):
    return jnp.maximum(x, 0.0) + jnp.log1p(jnp.exp(-jnp.abs(x)))


def _dot(a, b):
    return jnp.dot(a.astype(BF16), b.astype(BF16), preferred_element_type=F32)


def _dot_nt(a, b):
    return lax.dot_general(a.astype(BF16), b.astype(BF16), (((1,), (1,)), ((), ())), preferred_element_type=F32)


def _dot_tn(a, b):
    return lax.dot_general(a.astype(BF16), b.astype(BF16), (((0,), (0,)), ((), ())), preferred_element_type=F32)


def _chunk_cumsum(x):
    pos = lax.broadcasted_iota(jnp.int32, x.shape, 0) % CHUNK
    shift = 1
    while shift < CHUNK:
        x = x + jnp.where(pos >= shift, pltpu.roll(x, shift, axis=0), 0.0)
        shift *= 2
    return x


def _tri_masks(n=CHUNK):
    row = lax.broadcasted_iota(jnp.int32, (n, n), 0)
    col = lax.broadcasted_iota(jnp.int32, (n, n), 1)
    same = (row // CHUNK) == (col // CHUNK)
    return same & (row >= col), same & (row > col)


def _ffn_kernel(x_ref, nw_ref, wg_ref, wu_ref, wd_ref, fw_ref, *rest, n_ff_tiles, final, n_cast):
    cast_in = rest[:n_cast]
    o_ref = rest[n_cast]
    cast_out = rest[n_cast + 1:2 * n_cast + 1]
    xn_ref = rest[2 * n_cast + 1]
    j = pl.program_id(1)
    for src, dst in zip(cast_in, cast_out):
        dst[...] = src[...].astype(BF16)

    @pl.when(j == 0)
    def _():
        x = x_ref[...]
        xn_ref[...] = _rms(x, nw_ref[...]).astype(BF16)
        o_ref[...] = x

    xn = xn_ref[...]
    g = jnp.dot(xn, wg_ref[...], preferred_element_type=F32)
    u = jnp.dot(xn, wu_ref[...], preferred_element_type=F32)
    h = (FFN_RES * (_silu(g) * u)).astype(BF16)
    o_ref[...] += jnp.dot(h, wd_ref[...], preferred_element_type=F32)

    if final:
        @pl.when(j == n_ff_tiles - 1)
        def _():
            o_ref[...] = _rms(o_ref[...], fw_ref[...])


def _ffn(x2, nw, wg, wu, wd, fw, *, final, cast_next=None, tm=1024, tf=512):
    t, d = x2.shape
    f = wg.shape[1]
    n_row, n_ff = t // tm, f // tf
    in_specs = [pl.BlockSpec((tm, d), lambda i, j: (i, 0)),
                pl.BlockSpec((1, d), lambda i, j: (0, 0)),
                pl.BlockSpec((d, tf), lambda i, j: (0, j)),
                pl.BlockSpec((d, tf), lambda i, j: (0, j)),
                pl.BlockSpec((tf, d), lambda i, j: (j, 0)),
                pl.BlockSpec((1, d), lambda i, j: (0, 0))]
    out_specs = [pl.BlockSpec((tm, d), lambda i, j: (i, 0))]
    out_shape = [jax.ShapeDtypeStruct((t, d), F32)]
    operands = [x2, nw.reshape(1, d), wg, wu, wd, fw.reshape(1, d)]
    if cast_next is not None:
        assert all(w.shape == s for w, s in zip(cast_next, ((d, f), (d, f), (f, d)))) and d % (n_row * LANES) == 0
        up_spec = pl.BlockSpec((d // n_row, tf), lambda i, j: (i, j))
        down_spec = pl.BlockSpec((tf, d // n_row), lambda i, j: (j, i))
        in_specs += [up_spec, up_spec, down_spec]
        out_specs += [up_spec, up_spec, down_spec]
        out_shape += [jax.ShapeDtypeStruct(w.shape, BF16) for w in cast_next]
        operands += list(cast_next)
    res = pl.pallas_call(
        functools.partial(_ffn_kernel, n_ff_tiles=n_ff, final=final, n_cast=0 if cast_next is None else 3),
        grid=(n_row, n_ff),
        in_specs=in_specs,
        out_specs=out_specs,
        out_shape=out_shape,
        scratch_shapes=[pltpu.VMEM((tm, d), BF16)],
        compiler_params=_params(("parallel", "arbitrary"), FFN_VMEM_LIMIT),
        name="ffn",
    )(*operands)
    return res[0], tuple(res[1:])


def _inproj_kernel(x_ref, nw_ref, w_ref, ws_ref, o_ref, os_ref, xn_ref):
    @pl.when(pl.program_id(1) == 0)
    def _():
        xn = _rms(x_ref[...], nw_ref[...]).astype(BF16)
        xn_ref[...] = xn
        os_ref[...] = jnp.dot(xn, ws_ref[...], preferred_element_type=F32)

    o_ref[...] = jnp.dot(xn_ref[...], w_ref[...], preferred_element_type=F32)


def _inproj(x2, nw, w_main, w_small, *, tm=1024, tn=1024):
    t, d = x2.shape
    n = w_main.shape[1]
    return pl.pallas_call(
        _inproj_kernel,
        grid=(t // tm, n // tn),
        in_specs=[pl.BlockSpec((tm, d), lambda i, j: (i, 0)),
                  pl.BlockSpec((1, d), lambda i, j: (0, 0)),
                  pl.BlockSpec((d, tn), lambda i, j: (0, j)),
                  pl.BlockSpec((d, LANES), lambda i, j: (0, 0))],
        out_specs=[pl.BlockSpec((tm, tn), lambda i, j: (i, j)),
                   pl.BlockSpec((tm, LANES), lambda i, j: (i, 0))],
        out_shape=[jax.ShapeDtypeStruct((t, n), F32), jax.ShapeDtypeStruct((t, LANES), F32)],
        scratch_shapes=[pltpu.VMEM((tm, d), BF16)],
        compiler_params=_params(("parallel", "arbitrary")),
        name="inproj",
    )(x2, nw.reshape(1, d), w_main, w_small)


def _outproj_kernel(x_ref, ya_ref, yb_ref, wa_ref, wb_ref, o_ref):
    o_ref[...] = (x_ref[...] + jnp.dot(ya_ref[...], wa_ref[...], preferred_element_type=F32)
                  + jnp.dot(yb_ref[...], wb_ref[...], preferred_element_type=F32))


def _outproj(x2, ya, yb, w, *, tm=512):
    t, d = x2.shape
    ka = ya.shape[1]
    kb = yb.shape[1]
    assert ka == kb
    return pl.pallas_call(
        _outproj_kernel,
        grid=(t // tm,),
        in_specs=[pl.BlockSpec((tm, d), lambda i: (i, 0)),
                  pl.BlockSpec((tm, ka), lambda i: (i, 0)),
                  pl.BlockSpec((tm, kb), lambda i: (i, 0)),
                  pl.BlockSpec((ka, d), lambda i: (0, 0)),
                  pl.BlockSpec((kb, d), lambda i: (1, 0))],
        out_specs=pl.BlockSpec((tm, d), lambda i: (i, 0)),
        out_shape=jax.ShapeDtypeStruct((t, d), F32),
        compiler_params=_params(("parallel",)),
        name="outproj",
    )(x2, ya, yb, w, w)


def _causal_conv(x, w, xpad):
    seq = x.shape[0]
    xpad[SUBLANES:, :] = x
    y = x * w[DN_CONV - 1:DN_CONV, :]
    for back in range(1, DN_CONV):
        y = y + xpad[pl.ds(SUBLANES - back, seq), :] * w[DN_CONV - 1 - back:DN_CONV - back, :]
    return y


def _split3(x):
    parts = []
    for _ in range(N_SPLIT):
        p = x.astype(BF16).astype(F32)
        parts.append(p)
        x = x - p
    return parts


def _l2norm(x):
    return x * lax.rsqrt(jnp.sum(x * x, axis=-1, keepdims=True) + NORM_EPS)


def _deltanet_kernel(q_ref, k_ref, v_ref, z_ref, at_ref, bt_ref, cwq_ref, cwk_ref, cwv_ref,
                     alog1_ref, dtb1_ref, nw_ref, cum_ref, sel_ref, o_ref,
                     xpad, qs, ks, vs, gts, parts_s, qm_s, n_s, op_s, cd_s):
    seq = q_ref.shape[1]
    n_chunks = seq // CHUNK
    d = DN_HEAD_DIM
    incl, strict = _tri_masks(DN_ROWS)
    nw = nw_ref[...]
    sel = sel_ref[...]
    xpad[:SUBLANES, :] = jnp.zeros((SUBLANES, d), F32)

    for hh in range(DN_PAIR):
        cols = slice(hh * d, (hh + 1) * d)
        qs[...] = _l2norm(_silu(_causal_conv(q_ref[0, :, cols], cwq_ref[:, cols], xpad))) * DN_HEAD_DIM ** -0.5
        ks[...] = _l2norm(_silu(_causal_conv(k_ref[0, :, cols], cwk_ref[:, cols], xpad)))
        vs[...] = _silu(_causal_conv(v_ref[0, :, cols], cwv_ref[:, cols], xpad))
        g_t = -jnp.exp(alog1_ref[hh]) * _softplus(at_ref[0, hh] + dtb1_ref[hh])
        gc_t = jnp.dot(g_t, cum_ref[...], preferred_element_type=F32, precision=lax.Precision.HIGHEST)
        gts[...] = gc_t
        for i, part in enumerate(_split3(gc_t) + _split3(jax.nn.sigmoid(bt_ref[0, hh]))):
            parts_s[i] = part

        def prep(it, carry):
            groups = [it * DN_GROUPS_PER_ITER + g for g in range(DN_GROUPS_PER_ITER)]
            rows = [pl.ds(pl.multiple_of(gi * DN_ROWS, DN_ROWS), DN_ROWS) for gi in groups]
            q = [qs[r, :] for r in rows]
            k = [ks[r, :] for r in rows]
            src = [jnp.concatenate([parts_s[i, pl.ds(gi, 1), :] for i in range(2 * N_SPLIT)]
                                   + [jnp.zeros((SEL_ROWS - 2 * N_SPLIT, DN_ROWS), F32)], axis=0) for gi in groups]
            colb = [_dot_tn(s, sel) for s in src]
            gcb = [c[:, :d] for c in colb]
            beta = [c[:, d:] for c in colb]
            decay = [jnp.exp(jnp.where(incl, jnp.concatenate([gc] * (DN_ROWS // d), axis=1) - gts[pl.ds(gi, 1), :],
                                       -jnp.inf)) for gc, gi in zip(gcb, groups)]
            kb = [ki * bi for ki, bi in zip(k, beta)]
            a = [jnp.where(strict, _dot_nt(kbi, ki) * dc, 0.0) for kbi, ki, dc in zip(kb, k, decay)]
            eg = [jnp.exp(gc) for gc in gcb]
            x = [jnp.concatenate([kbi * egi, vs[r, :] * bi], axis=1) for kbi, egi, r, bi in zip(kb, eg, rows, beta)]
            pb = [ai.astype(BF16) for ai in a]
            x = [xi - _dot(pi, xi) for pi, xi in zip(pb, x)]
            for _ in range(CHUNK.bit_length() - 2):
                pb = [jnp.dot(pi, pi, preferred_element_type=F32).astype(BF16) for pi in pb]
                x = [xi + _dot(pi, xi) for pi, xi in zip(pb, x)]
            xb = [xi.astype(BF16) for xi in x]
            qk = [jnp.where(incl, _dot_nt(qi, ki) * dc, 0.0) for qi, ki, dc in zip(q, k, decay)]
            qx = [_dot(qki, xi) for qki, xi in zip(qk, xb)]
            for g in range(DN_GROUPS_PER_ITER):
                last = [gcb[g][(i + 1) * CHUNK - 1:(i + 1) * CHUNK, :] for i in range(DN_GROUP)]
                g_last = jnp.concatenate([jnp.broadcast_to(li, (CHUNK, d)) for li in last], axis=0)
                kd = (k[g] * jnp.exp(g_last - gcb[g])).astype(BF16)
                qp = (q[g] * eg[g] - qx[g][:, :d]).astype(BF16)
                for i in range(DN_GROUP):
                    c = groups[g] * DN_GROUP + i
                    cr = slice(i * CHUNK, (i + 1) * CHUNK)
                    kx = _dot_tn(kd[cr, :], xb[g][cr, :])
                    qm_s[hh, c, :CHUNK, :] = qp[cr, :]
                    qm_s[hh, c, CHUNK:, :] = kx[:, :d].astype(BF16)
                    n_s[hh, c] = kx[:, d:]
                    cd_s[hh, pl.ds(c, 1), :] = jnp.exp(last[i])
                op_s[hh, rows[g], :] = qx[g][:, d:]
            return carry

        lax.fori_loop(0, seq // (DN_ROWS * DN_GROUPS_PER_ITER), prep, 0)

    def step(c, states):
        r = pl.ds(pl.multiple_of(c * CHUNK, CHUNK), CHUNK)
        new = []
        for hh in range(DN_PAIR):
            cols = slice(hh * d, (hh + 1) * d)
            prod = jnp.dot(qm_s[hh, c], states[hh].astype(BF16), preferred_element_type=F32)
            o = prod[:CHUNK, :] + op_s[hh, r, :]
            o_ref[0, r, cols] = (_rms(o, nw) * _silu(z_ref[0, r, cols])).astype(o_ref.dtype)
            new.append(cd_s[hh, pl.ds(c, 1), :] * states[hh] + n_s[hh, c] - prod[CHUNK:, :])
        return tuple(new)

    lax.fori_loop(0, n_chunks, step, tuple(jnp.zeros((d, d), F32) for _ in range(DN_PAIR)))


def _deltanet(main, small, conv_w, a_log, dt_bias, norm_w):
    bsz, seq, _ = main.shape
    n_chunks = seq // CHUNK
    d = DN_HEAD_DIM
    pw = DN_PAIR * d
    n_pairs = DN_HEADS // DN_PAIR
    n_groups = seq // DN_ROWS
    assert seq % (DN_ROWS * DN_GROUPS_PER_ITER) == 0 and DN_ROWS % d == 0
    ab_t = small[:, :, :2 * DN_HEADS].transpose(0, 2, 1).reshape(bsz, 2 * DN_HEADS, n_groups, DN_ROWS)
    cw = conv_w.reshape(DN_CONV, 3 * DN_WIDTH)
    cum = jnp.kron(jnp.eye(DN_GROUP, dtype=F32), jnp.triu(jnp.ones((CHUNK, CHUNK), F32)))
    sel = jnp.kron((jnp.arange(SEL_ROWS)[:, None] // N_SPLIT == jnp.arange(2)[None, :]).astype(BF16),
                   jnp.ones((1, d), BF16))
    col = lambda off: pl.BlockSpec((1, seq, pw), lambda b, h: (b, 0, off + h))
    cwspec = lambda off: pl.BlockSpec((DN_CONV, pw), lambda b, h: (0, off + h))
    tspec = lambda off: pl.BlockSpec((1, DN_PAIR, n_groups, DN_ROWS), lambda b, h: (b, off + h, 0, 0))
    one = pl.BlockSpec((DN_PAIR, 1, 1), lambda b, h: (h, 0, 0))
    return pl.pallas_call(
        _deltanet_kernel,
        grid=(bsz, n_pairs),
        in_specs=[col(0), col(n_pairs), col(2 * n_pairs), col(3 * n_pairs),
                  tspec(0), tspec(n_pairs),
                  cwspec(0), cwspec(n_pairs), cwspec(2 * n_pairs),
                  one, one,
                  pl.BlockSpec((1, d), lambda b, h: (0, 0)),
                  pl.BlockSpec((DN_ROWS, DN_ROWS), lambda b, h: (0, 0)),
                  pl.BlockSpec((SEL_ROWS, 2 * d), lambda b, h: (0, 0))],
        out_specs=pl.BlockSpec((1, seq, pw), lambda b, h: (b, 0, h)),
        out_shape=jax.ShapeDtypeStruct((bsz, seq, DN_WIDTH), BF16),
        scratch_shapes=[pltpu.VMEM((seq + SUBLANES, d), F32)] + [pltpu.VMEM((seq, d), F32)] * 3 + [
                        pltpu.VMEM((n_groups, DN_ROWS), F32),
                        pltpu.VMEM((2 * N_SPLIT, n_groups, DN_ROWS), F32),
                        pltpu.VMEM((DN_PAIR, n_chunks, CHUNK + d, d), BF16),
                        pltpu.VMEM((DN_PAIR, n_chunks, d, d), F32),
                        pltpu.VMEM((DN_PAIR, seq, d), F32),
                        pltpu.VMEM((DN_PAIR, n_chunks, d), F32)],
        compiler_params=_params(("parallel", "arbitrary")),
        name="deltanet",
    )(main, main, main, main, ab_t, ab_t, cw, cw, cw,
      a_log.reshape(DN_HEADS, 1, 1), dt_bias.reshape(DN_HEADS, 1, 1), norm_w.reshape(1, d), cum, sel)


def _gelu_tanh(x):
    return x * (0.5 * (1.0 + jnp.tanh(0.7978845608028654 * (x + 0.044715 * (x * x * x)))))


def _s5_kernel(u_ref, are_ref, aim_ref, ls_ref, bre_ref, bim_ref, cre_ref, cim_ref, d_ref, o_ref,
               ut, xs, xb, yt, state, bbd, cbd, lam):
    ns = S5_TILE_STATE
    bsz, ts, _ = u_ref.shape

    @pl.when(pl.program_id(1) == 0)
    def _():
        a_re = are_ref[0]
        a_im = aim_ref[0]
        dt = jnp.exp(ls_ref[0])
        mag = jnp.exp(a_re * dt)
        l_re = mag * jnp.cos(a_im * dt)
        l_im = mag * jnp.sin(a_im * dt)
        den = a_re * a_re + a_im * a_im
        c_re = ((l_re - 1.0) * a_re + l_im * a_im) / den
        c_im = (l_im * a_re - (l_re - 1.0) * a_im) / den
        b_re = bre_ref[0]
        b_im = bim_ref[0]
        bbd[:, :ns] = (c_re * b_re - c_im * b_im).astype(BF16)
        bbd[:, ns:] = (c_re * b_im + c_im * b_re).astype(BF16)
        cbd[:ns, :] = cre_ref[0].astype(BF16)
        cbd[ns:, :] = (-cim_ref[0]).astype(BF16)
        lam[:, :ns] = jnp.broadcast_to(l_re, (bsz, ns))
        lam[:, ns:] = jnp.broadcast_to(l_im, (bsz, ns))
        state[...] = jnp.zeros_like(state)

    for b in range(bsz):
        ut[pl.ds(b, ts, stride=bsz), :] = u_ref[b]
    xs[...] = jnp.dot(ut[...].astype(BF16), bbd[...], preferred_element_type=F32)
    l_re = lam[:, :ns]
    l_im = lam[:, ns:]

    frames = BF16_ROWS // bsz

    def step(i, carry):
        x_re, x_im = carry
        r = pl.ds(pl.multiple_of(i * BF16_ROWS, BF16_ROWS), BF16_ROWS)
        bu = xs[r, :]
        res_re, res_im = [], []
        for f in range(frames):
            rows = slice(f * bsz, (f + 1) * bsz)
            x_re, x_im = (l_re * x_re - l_im * x_im + bu[rows, :ns], l_re * x_im + l_im * x_re + bu[rows, ns:])
            res_re.append(x_re)
            res_im.append(x_im)
        xb[r, :ns] = jnp.concatenate(res_re, axis=0).astype(BF16)
        xb[r, ns:] = jnp.concatenate(res_im, axis=0).astype(BF16)
        return x_re, x_im

    x_re, x_im = lax.fori_loop(0, ts // frames, step, (state[:, :ns], state[:, ns:]))
    state[:, :ns] = x_re
    state[:, ns:] = x_im
    y = jnp.dot(xb[...], cbd[...], preferred_element_type=F32) + d_ref[0] * ut[...]
    yt[...] = _gelu_tanh(y)
    for b in range(bsz):
        o_ref[b] = yt[pl.ds(b, ts, stride=bsz), :]


def _s5(main, u_col_block, a_re, a_im, b_re, b_im, c_re, c_im, d, log_step, *, ts=256):
    bsz, seq, _ = main.shape
    assert bsz == SUBLANES
    tg, ns, nt = S5_TILE_GROUPS, S5_TILE_STATE, S5_TILES
    eye = jnp.eye(tg, dtype=F32)

    def expand_b(b):
        bt = b.reshape(nt, tg, S5_STATE, S5_GROUP)
        return jnp.einsum('ngph,gk->nghkp', bt, eye).reshape(nt, LANES, ns)

    def expand_c(c):
        ct = c.reshape(nt, tg, S5_GROUP, S5_STATE)
        return jnp.einsum('nghp,gk->ngpkh', ct, eye).reshape(nt, ns, LANES)

    chan = lambda a: a.reshape(nt, 1, ns)
    ls = jnp.broadcast_to(log_step[:, None], (S5_GROUPS, S5_STATE))
    pspec = pl.BlockSpec((1, 1, ns), lambda c, t: (c, 0, 0))
    bspec = pl.BlockSpec((1, LANES, ns), lambda c, t: (c, 0, 0))
    cspec = pl.BlockSpec((1, ns, LANES), lambda c, t: (c, 0, 0))
    return pl.pallas_call(
        _s5_kernel,
        grid=(nt, seq // ts),
        in_specs=[pl.BlockSpec((bsz, ts, LANES), lambda c, t: (0, t, u_col_block + c)),
                  pspec, pspec, pspec, bspec, bspec, cspec, cspec,
                  pl.BlockSpec((1, 1, LANES), lambda c, t: (c, 0, 0))],
        out_specs=pl.BlockSpec((bsz, ts, LANES), lambda c, t: (0, t, c)),
        out_shape=jax.ShapeDtypeStruct((bsz, seq, S5_WIDTH), F32),
        scratch_shapes=[pltpu.VMEM((bsz * ts, LANES), F32),
                        pltpu.VMEM((bsz * ts, 2 * ns), F32),
                        pltpu.VMEM((bsz * ts, 2 * ns), BF16),
                        pltpu.VMEM((bsz * ts, LANES), F32),
                        pltpu.VMEM((bsz, 2 * ns), F32),
                        pltpu.VMEM((LANES, 2 * ns), BF16),
                        pltpu.VMEM((2 * ns, LANES), BF16),
                        pltpu.VMEM((bsz, 2 * ns), F32)],
        compiler_params=_params(("parallel", "arbitrary")),
        name="s5",
    )(main, chan(a_re), chan(a_im), chan(ls), expand_b(b_re), expand_b(b_im), expand_c(c_re), expand_c(c_im),
      d.reshape(nt, 1, LANES))


def _glu_kernel(y_ref, w_ref, b_ref, o_ref):
    y = y_ref[...]
    gate = jnp.dot(y.astype(BF16), w_ref[...], preferred_element_type=F32) + b_ref[...]
    o_ref[...] = (y * jax.nn.sigmoid(gate)).astype(o_ref.dtype)


def _glu(y2, w, b, *, tm=512):
    t, n = y2.shape
    return pl.pallas_call(
        _glu_kernel,
        grid=(t // tm,),
        in_specs=[pl.BlockSpec((tm, n), lambda i: (i, 0)),
                  pl.BlockSpec((n, n), lambda i: (0, 0)),
                  pl.BlockSpec((1, n), lambda i: (0, 0))],
        out_specs=pl.BlockSpec((tm, n), lambda i: (i, 0)),
        out_shape=jax.ShapeDtypeStruct((t, n), BF16),
        compiler_params=_params(("parallel",)),
        name="s5_glu",
    )(y2, w, b.reshape(1, n))


def _gla_scan(q_at, k_at, v_at, bs, gate_ref, nw_ref, o_ref, st_s):
    n_chunks, dv, dk = st_s.shape
    incl, _ = _tri_masks()
    nw = nw_ref[...]

    def chunk_rows(it):
        cs = [it * GLA_GROUP + i for i in range(GLA_GROUP)]
        return cs, [pl.ds(pl.multiple_of(c * CHUNK, CHUNK), CHUNK) for c in cs]

    def states(it, state_t):
        cs, rows = chunk_rows(it)
        b = [bs[r, :] for r in rows]
        b_last = [bi[CHUNK - 1:CHUNK, :] for bi in b]
        inc = [_dot_tn(v_at(r), k_at(r) * jnp.exp(bl - bi)) for r, bl, bi in zip(rows, b_last, b)]
        for c, bl, ic in zip(cs, b_last, inc):
            st_s[c] = state_t.astype(BF16)
            state_t = jnp.exp(bl) * state_t + ic
        return state_t

    lax.fori_loop(0, n_chunks // GLA_GROUP, states, jnp.zeros((dv, dk), F32))

    def outputs(it, carry):
        cs, rows = chunk_rows(it)
        q = [q_at(r) for r in rows]
        b = [bs[r, :] for r in rows]
        b_mid = [bi[CHUNK // 2 - 1:CHUNK // 2, :] for bi in b]
        att = [jnp.where(incl, _dot_nt(qi * jnp.exp(bi - bm), k_at(r) * jnp.exp(bm - bi)), 0.0)
               for qi, bi, bm, r in zip(q, b, b_mid, rows)]
        o_inter = [_dot_nt(qi * jnp.exp(bi), st_s[c]) for qi, bi, c in zip(q, b, cs)]
        o = [_dot(ai, v_at(r)) + oi for ai, r, oi in zip(att, rows, o_inter)]
        for r, oi in zip(rows, o):
            o_ref[0, r, :] = (_rms(oi, nw) * _silu(gate_ref[0, r, :])).astype(o_ref.dtype)
        return carry

    lax.fori_loop(0, n_chunks // GLA_GROUP, outputs, 0)


def _gla_kernel(q_ref, k_ref, v_ref, r_ref, lr_ref, w2_ref, gb_ref, nw_ref, o_ref, bs, st_s):
    gate = _dot(lr_ref[0], w2_ref[...]) + gb_ref[...]
    log_a = -_softplus(-gate) / GLA_TAU
    bs[...] = _chunk_cumsum(log_a)
    _gla_scan(lambda r: q_ref[0, r, :] * GLA_HEAD_K ** -0.5, lambda r: k_ref[0, r, :], lambda r: v_ref[0, r, :],
              bs, r_ref, nw_ref, o_ref, st_s)


def _hgrn2_kernel(q_ref, f_ref, i_ref, g_ref, lbl_ref, nw_ref, o_ref, ks, bs, st_s, *, layer):
    logits = lbl_ref[...]
    e = jnp.exp(logits - jnp.max(logits, axis=0, keepdims=True))
    p = e / jnp.sum(e, axis=0, keepdims=True)
    lb_first = p[0:1, :]
    lb_layer = lb_first
    for i in range(1, layer + 1):
        lb_layer = lb_layer + p[i:i + 1, :]
    lb = lb_layer - lb_first
    z_f = f_ref[0]
    e_z = jnp.exp(-jnp.abs(z_f))
    big = 1.0 / (1.0 + e_z)
    small = e_z * big
    sig_pos = jnp.where(z_f >= 0.0, big, small)
    sig_neg = jnp.where(z_f >= 0.0, small, big)
    ks[...] = (1.0 - lb) * sig_neg
    bs[...] = _chunk_cumsum(jnp.log(lb + (1.0 - lb) * sig_pos))
    _gla_scan(lambda r: q_ref[0, r, :], lambda r: ks[r, :], lambda r: i_ref[0, r, :],
              bs, g_ref, nw_ref, o_ref, st_s)


def _gla(main, small, w2, gate_b, norm_w):
    bsz, seq, _ = main.shape
    dk, dv = GLA_HEAD_K, GLA_HEAD_V
    w2p = jnp.concatenate([w2, jnp.zeros((LANES - GLA_GATE_RANK, GLA_KEY_WIDTH), F32)], axis=0).astype(BF16)
    kspec = lambda off: pl.BlockSpec((1, seq, dk), lambda b, h: (b, 0, off + h))
    vspec = lambda off: pl.BlockSpec((1, seq, dv), lambda b, h: (b, 0, off + h))
    return pl.pallas_call(
        _gla_kernel,
        grid=(bsz, GLA_HEADS),
        in_specs=[kspec(0), kspec(GLA_HEADS), vspec(GLA_HEADS), vspec(2 * GLA_HEADS),
                  pl.BlockSpec((1, seq, LANES), lambda b, h: (b, 0, 0)),
                  pl.BlockSpec((LANES, dk), lambda b, h: (0, h)),
                  pl.BlockSpec((1, dk), lambda b, h: (0, h)),
                  pl.BlockSpec((1, dv), lambda b, h: (0, 0))],
        out_specs=pl.BlockSpec((1, seq, dv), lambda b, h: (b, 0, h)),
        out_shape=jax.ShapeDtypeStruct((bsz, seq, GLA_WIDTH), BF16),
        scratch_shapes=[pltpu.VMEM((seq, dk), F32), pltpu.VMEM((seq // CHUNK, dv, dk), BF16)],
        compiler_params=_params(("parallel", "arbitrary")),
        name="gla",
    )(main, main, main, main, small, w2p, gate_b.reshape(1, GLA_KEY_WIDTH), norm_w.reshape(1, dv))


def _hgrn2(main, lb_logits, layer, norm_w):
    bsz, seq, _ = main.shape
    d = HG_EXPAND
    base = (2 * GLA_KEY_WIDTH + 2 * GLA_WIDTH) // d
    col = lambda off: pl.BlockSpec((1, seq, d), lambda b, h: (b, 0, base + off + h))
    depth = lb_logits.shape[0]
    return pl.pallas_call(
        functools.partial(_hgrn2_kernel, layer=layer),
        grid=(bsz, HG_HEADS),
        in_specs=[col(0), col(HG_HEADS), col(2 * HG_HEADS), col(3 * HG_HEADS),
                  pl.BlockSpec((depth, d), lambda b, h: (0, h)),
                  pl.BlockSpec((1, d), lambda b, h: (0, 0))],
        out_specs=pl.BlockSpec((1, seq, d), lambda b, h: (b, 0, h)),
        out_shape=jax.ShapeDtypeStruct((bsz, seq, HG_WIDTH), BF16),
        scratch_shapes=[pltpu.VMEM((seq, d), F32)] * 2 + [pltpu.VMEM((seq // CHUNK, d, d), BF16)],
        compiler_params=_params(("parallel", "arbitrary")),
        name="hgrn2",
    )(main, main, main, main, lb_logits, norm_w.reshape(1, d))


def _pad_cols(w):
    return jnp.concatenate([w, jnp.zeros((w.shape[0], LANES - w.shape[1]), w.dtype)], axis=1)


def _even_mixer(x2, bsz, seq, mix_norm, w_in, conv_w, a_log, dt_bias, dn_norm_w, s5_a_re, s5_a_im, s5_b_re, s5_b_im,
                s5_c_re, s5_c_im, s5_d, s5_log_step, s5_glu_w, s5_glu_b, w_out):
    n_qkvz = 4 * DN_WIDTH
    n_ab = 2 * DN_HEADS
    w_in = w_in.astype(BF16)
    w_main = jnp.concatenate([w_in[:, :n_qkvz], w_in[:, n_qkvz + n_ab:]], axis=1)
    w_small = _pad_cols(w_in[:, n_qkvz:n_qkvz + n_ab])
    main, small = _inproj(x2, mix_norm, w_main, w_small)
    main = main.reshape(bsz, seq, -1)
    small = small.reshape(bsz, seq, LANES)
    y_a = _deltanet(main, small, conv_w, a_log, dt_bias, dn_norm_w)
    y_s5 = _s5(main, n_qkvz // LANES, s5_a_re, s5_a_im, s5_b_re, s5_b_im, s5_c_re, s5_c_im, s5_d, s5_log_step)
    y_b = _glu(y_s5.reshape(bsz * seq, S5_WIDTH), s5_glu_w.astype(BF16), s5_glu_b)
    return _outproj(x2, y_a.reshape(bsz * seq, DN_WIDTH), y_b, w_out.astype(BF16))


def _odd_mixer(x2, bsz, seq, layer, lb_logits, mix_norm, w_in, gate_w2, gate_b, gla_norm_w, hg_norm_w, w_out):
    n_c = 2 * GLA_KEY_WIDTH + 2 * GLA_WIDTH
    w_in = w_in.astype(BF16)
    w_main = jnp.concatenate([w_in[:, :n_c], w_in[:, n_c + GLA_GATE_RANK:]], axis=1)
    w_small = _pad_cols(w_in[:, n_c:n_c + GLA_GATE_RANK])
    main, small = _inproj(x2, mix_norm, w_main, w_small)
    main = main.reshape(bsz, seq, -1)
    small = small.reshape(bsz, seq, LANES)
    y_c = _gla(main, small, gate_w2, gate_b, gla_norm_w)
    y_d = _hgrn2(main, lb_logits, layer, hg_norm_w)
    return _outproj(x2, y_c.reshape(bsz * seq, GLA_WIDTH), y_d.reshape(bsz * seq, HG_WIDTH), w_out.astype(BF16))


def kernel(x, l0_ffn1_norm, l0_ffn1_w_gate, l0_ffn1_w_up, l0_ffn1_w_down, l0_mix_norm, l0_w_in, l0_dn_conv_w, l0_dn_a_log, l0_dn_dt_bias, l0_dn_norm_w, l0_s5_a_re, l0_s5_a_im, l0_s5_b_re, l0_s5_b_im, l0_s5_c_re, l0_s5_c_im, l0_s5_d, l0_s5_log_step, l0_s5_glu_w, l0_s5_glu_b, l0_w_out, l0_ffn2_norm, l0_ffn2_w_gate, l0_ffn2_w_up, l0_ffn2_w_down, l1_ffn1_norm, l1_ffn1_w_gate, l1_ffn1_w_up, l1_ffn1_w_down, l1_mix_norm, l1_w_in, l1_gla_gate_w2, l1_gla_gate_b, l1_gla_norm_w, l1_hg_norm_w, l1_w_out, l1_ffn2_norm, l1_ffn2_w_gate, l1_ffn2_w_up, l1_ffn2_w_down, hgrn_lb_logits, final_norm):
    bsz, seq, d = x.shape
    x2 = x.reshape(bsz * seq, d)

    w_l0f1 = tuple(w.astype(BF16) for w in (l0_ffn1_w_gate, l0_ffn1_w_up, l0_ffn1_w_down))
    x2, w_l0f2 = _ffn(x2, l0_ffn1_norm, *w_l0f1, final_norm, final=False,
                      cast_next=(l0_ffn2_w_gate, l0_ffn2_w_up, l0_ffn2_w_down))
    x2 = _even_mixer(x2, bsz, seq, l0_mix_norm, l0_w_in, l0_dn_conv_w, l0_dn_a_log, l0_dn_dt_bias, l0_dn_norm_w,
                     l0_s5_a_re, l0_s5_a_im, l0_s5_b_re, l0_s5_b_im, l0_s5_c_re, l0_s5_c_im, l0_s5_d, l0_s5_log_step,
                     l0_s5_glu_w, l0_s5_glu_b, l0_w_out)
    x2, w_l1f1 = _ffn(x2, l0_ffn2_norm, *w_l0f2, final_norm, final=False,
                      cast_next=(l1_ffn1_w_gate, l1_ffn1_w_up, l1_ffn1_w_down))
    x2, w_l1f2 = _ffn(x2, l1_ffn1_norm, *w_l1f1, final_norm, final=False,
                      cast_next=(l1_ffn2_w_gate, l1_ffn2_w_up, l1_ffn2_w_down))
    x2 = _odd_mixer(x2, bsz, seq, 1, hgrn_lb_logits, l1_mix_norm, l1_w_in, l1_gla_gate_w2, l1_gla_gate_b,
                    l1_gla_norm_w, l1_hg_norm_w, l1_w_out)
    x2, _ = _ffn(x2, l1_ffn2_norm, *w_l1f2, final_norm, final=True)
    return x2.reshape(bsz, seq, d)
```

```python
import functools

import jax
import jax.numpy as jnp
from jax import lax
from jax.experimental import pallas as pl
from jax.experimental.pallas import tpu as pltpu

F32 = jnp.float32
BF16 = jnp.bfloat16

D_MODEL = 2048
CHUNK = 64
NORM_EPS = 1e-6
D_FF = 5632
FFN_RES = 0.5
DN_HEADS = 8
DN_HEAD_DIM = 128
DN_WIDTH = DN_HEADS * DN_HEAD_DIM
DN_CONV = 4
S5_WIDTH = D_MODEL - DN_WIDTH
S5_GROUP = 16
S5_GROUPS = S5_WIDTH // S5_GROUP
S5_STATE = 64
GLA_HEADS = 4
GLA_WIDTH = D_MODEL // 2
GLA_KEY_WIDTH = GLA_WIDTH // 2
GLA_HEAD_K = GLA_KEY_WIDTH // GLA_HEADS
GLA_HEAD_V = GLA_WIDTH // GLA_HEADS
GLA_GATE_RANK = 16
GLA_TAU = 16.0
HG_WIDTH = D_MODEL - GLA_WIDTH
HG_EXPAND = 128
HG_HEADS = HG_WIDTH // HG_EXPAND

LANES = 128
SUBLANES = 8
BF16_ROWS = 16
VMEM_LIMIT = 56 * 1024 * 1024
FFN_VMEM_LIMIT = 58 * 1024 * 1024

S5_TILE_GROUPS = LANES // S5_GROUP
S5_TILE_STATE = S5_TILE_GROUPS * S5_STATE
S5_TILES = S5_WIDTH // LANES
S5_SCAN_ROWS = 4 * BF16_ROWS

DN_PAIR = 2
DN_GROUP = 2
DN_ROWS = DN_GROUP * CHUNK
DN_GROUPS_PER_ITER = 8
N_SPLIT = 3
SEL_ROWS = BF16_ROWS
GLA_GROUP = 16


def _params(sem, vmem_limit=VMEM_LIMIT):
    return pltpu.CompilerParams(dimension_semantics=sem, vmem_limit_bytes=vmem_limit)


def _rms(x, w):
    return x * lax.rsqrt(jnp.mean(x * x, axis=-1, keepdims=True) + NORM_EPS) * w


def _silu(x):
    return x * jax.nn.sigmoid(x)


def _softplus(x):
    return jnp.maximum(x, 0.0) + jnp.log1p(jnp.exp(-jnp.abs(x)))


def _dot(a, b):
    return jnp.dot(a.astype(BF16), b.astype(BF16), preferred_element_type=F32)


def _dot_nt(a, b):
    return lax.dot_general(a.astype(BF16), b.astype(BF16), (((1,), (1,)), ((), ())), preferred_element_type=F32)


def _dot_tn(a, b):
    return lax.dot_general(a.astype(BF16), b.astype(BF16), (((0,), (0,)), ((), ())), preferred_element_type=F32)


def _chunk_cumsum(x):
    pos = lax.broadcasted_iota(jnp.int32, x.shape, 0) % CHUNK
    shift = 1
    while shift < CHUNK:
        x = x + jnp.where(pos >= shift, pltpu.roll(x, shift, axis=0), 0.0)
        shift *= 2
    return x


def _tri_masks(n=CHUNK):
    row = lax.broadcasted_iota(jnp.int32, (n, n), 0)
    col = lax.broadcasted_iota(jnp.int32, (n, n), 1)
    same = (row // CHUNK) == (col // CHUNK)
    return same & (row >= col), same & (row > col)


def _ffn_kernel(x_ref, nw_ref, wg_ref, wu_ref, wd_ref, fw_ref, *rest, n_ff_tiles, final, n_cast):
    cast_in = rest[:n_cast]
    o_ref = rest[n_cast]
    cast_out = rest[n_cast + 1:2 * n_cast + 1]
    xn_ref = rest[2 * n_cast + 1]
    j = pl.program_id(1)
    for src, dst in zip(cast_in, cast_out):
        dst[...] = src[...].astype(BF16)

    @pl.when(j == 0)
    def _():
        x = x_ref[...]
        xn_ref[...] = _rms(x, nw_ref[...]).astype(BF16)
        o_ref[...] = x

    xn = xn_ref[...]
    g = jnp.dot(xn, wg_ref[...], preferred_element_type=F32)
    u = jnp.dot(xn, wu_ref[...], preferred_element_type=F32)
    h = (FFN_RES * (_silu(g) * u)).astype(BF16)
    o_ref[...] += jnp.dot(h, wd_ref[...], preferred_element_type=F32)

    if final:
        @pl.when(j == n_ff_tiles - 1)
        def _():
            o_ref[...] = _rms(o_ref[...], fw_ref[...])


def _ffn(x2, nw, wg, wu, wd, fw, *, final, cast_next=None, tm=1024, tf=512):
    t, d = x2.shape
    f = wg.shape[1]
    n_row, n_ff = t // tm, f // tf
    in_specs = [pl.BlockSpec((tm, d), lambda i, j: (i, 0)),
                pl.BlockSpec((1, d), lambda i, j: (0, 0)),
                pl.BlockSpec((d, tf), lambda i, j: (0, j)),
                pl.BlockSpec((d, tf), lambda i, j: (0, j)),
                pl.BlockSpec((tf, d), lambda i, j: (j, 0)),
                pl.BlockSpec((1, d), lambda i, j: (0, 0))]
    out_specs = [pl.BlockSpec((tm, d), lambda i, j: (i, 0))]
    out_shape = [jax.ShapeDtypeStruct((t, d), F32)]
    operands = [x2, nw.reshape(1, d), wg, wu, wd, fw.reshape(1, d)]
    if cast_next is not None:
        assert all(w.shape == s for w, s in zip(cast_next, ((d, f), (d, f), (f, d)))) and d % (n_row * LANES) == 0
        up_spec = pl.BlockSpec((d // n_row, tf), lambda i, j: (i, j))
        down_spec = pl.BlockSpec((tf, d // n_row), lambda i, j: (j, i))
        in_specs += [up_spec, up_spec, down_spec]
        out_specs += [up_spec, up_spec, down_spec]
        out_shape += [jax.ShapeDtypeStruct(w.shape, BF16) for w in cast_next]
        operands += list(cast_next)
    res = pl.pallas_call(
        functools.partial(_ffn_kernel, n_ff_tiles=n_ff, final=final, n_cast=0 if cast_next is None else 3),
        grid=(n_row, n_ff),
        in_specs=in_specs,
        out_specs=out_specs,
        out_shape=out_shape,
        scratch_shapes=[pltpu.VMEM((tm, d), BF16)],
        compiler_params=_params(("parallel", "arbitrary"), FFN_VMEM_LIMIT),
        name="ffn",
    )(*operands)
    return res[0], tuple(res[1:])


def _inproj_kernel(x_ref, nw_ref, w_ref, ws_ref, o_ref, os_ref, xn_ref):
    @pl.when(pl.program_id(1) == 0)
    def _():
        xn = _rms(x_ref[...], nw_ref[...]).astype(BF16)
        xn_ref[...] = xn
        os_ref[...] = jnp.dot(xn, ws_ref[...], preferred_element_type=F32)

    o_ref[...] = jnp.dot(xn_ref[...], w_ref[...], preferred_element_type=F32)


def _inproj(x2, nw, w_main, w_small, *, tm=1024, tn=1024):
    t, d = x2.shape
    n = w_main.shape[1]
    return pl.pallas_call(
        _inproj_kernel,
        grid=(t // tm, n // tn),
        in_specs=[pl.BlockSpec((tm, d), lambda i, j: (i, 0)),
                  pl.BlockSpec((1, d), lambda i, j: (0, 0)),
                  pl.BlockSpec((d, tn), lambda i, j: (0, j)),
                  pl.BlockSpec((d, LANES), lambda i, j: (0, 0))],
        out_specs=[pl.BlockSpec((tm, tn), lambda i, j: (i, j)),
                   pl.BlockSpec((tm, LANES), lambda i, j: (i, 0))],
        out_shape=[jax.ShapeDtypeStruct((t, n), F32), jax.ShapeDtypeStruct((t, LANES), F32)],
        scratch_shapes=[pltpu.VMEM((tm, d), BF16)],
        compiler_params=_params(("parallel", "arbitrary")),
        name="inproj",
    )(x2, nw.reshape(1, d), w_main, w_small)


def _outproj_kernel(x_ref, ya_ref, yb_ref, wa_ref, wb_ref, o_ref):
    o_ref[...] = (x_ref[...] + jnp.dot(ya_ref[...], wa_ref[...], preferred_element_type=F32)
                  + jnp.dot(yb_ref[...], wb_ref[...], preferred_element_type=F32))


def _outproj(x2, ya, yb, w, *, tm=512):
    t, d = x2.shape
    ka = ya.shape[1]
    kb = yb.shape[1]
    assert ka == kb
    return pl.pallas_call(
        _outproj_kernel,
        grid=(t // tm,),
        in_specs=[pl.BlockSpec((tm, d), lambda i: (i, 0)),
                  pl.BlockSpec((tm, ka), lambda i: (i, 0)),
                  pl.BlockSpec((tm, kb), lambda i: (i, 0)),
                  pl.BlockSpec((ka, d), lambda i: (0, 0)),
                  pl.BlockSpec((kb, d), lambda i: (1, 0))],
        out_specs=pl.BlockSpec((tm, d), lambda i: (i, 0)),
        out_shape=jax.ShapeDtypeStruct((t, d), F32),
        compiler_params=_params(("parallel",)),
        name="outproj",
    )(x2, ya, yb, w, w)


def _causal_conv(x, w, xpad):
    seq = x.shape[0]
    xpad[SUBLANES:, :] = x
    y = x * w[DN_CONV - 1:DN_CONV, :]
    for back in range(1, DN_CONV):
        y = y + xpad[pl.ds(SUBLANES - back, seq), :] * w[DN_CONV - 1 - back:DN_CONV - back, :]
    return y


def _split3(x):
    parts = []
    for _ in range(N_SPLIT):
        p = x.astype(BF16).astype(F32)
        parts.append(p)
        x = x - p
    return parts


def _l2norm(x):
    return x * lax.rsqrt(jnp.sum(x * x, axis=-1, keepdims=True) + NORM_EPS)


def _deltanet_kernel(q_ref, k_ref, v_ref, z_ref, at_ref, bt_ref, cwq_ref, cwk_ref, cwv_ref,
                     alog1_ref, dtb1_ref, nw_ref, cum_ref, sel_ref, o_ref,
                     xpad, qs, ks, vs, gts, parts_s, qm_s, n_s, op_s, cd_s):
    seq = q_ref.shape[1]
    n_chunks = seq // CHUNK
    d = DN_HEAD_DIM
    incl, strict = _tri_masks(DN_ROWS)
    nw = nw_ref[...]
    sel = sel_ref[...]
    xpad[:SUBLANES, :] = jnp.zeros((SUBLANES, d), F32)

    for hh in range(DN_PAIR):
        cols = slice(hh * d, (hh + 1) * d)
        qs[...] = _l2norm(_silu(_causal_conv(q_ref[0, :, cols], cwq_ref[:, cols], xpad))) * DN_HEAD_DIM ** -0.5
        ks[...] = _l2norm(_silu(_causal_conv(k_ref[0, :, cols], cwk_ref[:, cols], xpad)))
        vs[...] = _silu(_causal_conv(v_ref[0, :, cols], cwv_ref[:, cols], xpad))
        g_t = -jnp.exp(alog1_ref[hh]) * _softplus(at_ref[0, hh] + dtb1_ref[hh])
        gc_t = jnp.dot(g_t, cum_ref[...], preferred_element_type=F32, precision=lax.Precision.HIGHEST)
        gts[...] = gc_t
        for i, part in enumerate(_split3(gc_t) + _split3(jax.nn.sigmoid(bt_ref[0, hh]))):
            parts_s[i] = part

        def prep(it, carry):
            groups = [it * DN_GROUPS_PER_ITER + g for g in range(DN_GROUPS_PER_ITER)]
            rows = [pl.ds(pl.multiple_of(gi * DN_ROWS, DN_ROWS), DN_ROWS) for gi in groups]
            q = [qs[r, :] for r in rows]
            k = [ks[r, :] for r in rows]
            src = [jnp.concatenate([parts_s[i, pl.ds(gi, 1), :] for i in range(2 * N_SPLIT)]
                                   + [jnp.zeros((SEL_ROWS - 2 * N_SPLIT, DN_ROWS), F32)], axis=0) for gi in groups]
            colb = [_dot_tn(s, sel) for s in src]
            gcb = [c[:, :d] for c in colb]
            beta = [c[:, d:] for c in colb]
            decay = [jnp.exp(jnp.where(incl, jnp.concatenate([gc] * (DN_ROWS // d), axis=1) - gts[pl.ds(gi, 1), :],
                                       -jnp.inf)) for gc, gi in zip(gcb, groups)]
            kb = [ki * bi for ki, bi in zip(k, beta)]
            a = [jnp.where(strict, _dot_nt(kbi, ki) * dc, 0.0) for kbi, ki, dc in zip(kb, k, decay)]
            eg = [jnp.exp(gc) for gc in gcb]
            x = [jnp.concatenate([kbi * egi, vs[r, :] * bi], axis=1) for kbi, egi, r, bi in zip(kb, eg, rows, beta)]
            pb = [ai.astype(BF16) for ai in a]
            x = [xi - _dot(pi, xi) for pi, xi in zip(pb, x)]
            for _ in range(CHUNK.bit_length() - 2):
                pb = [jnp.dot(pi, pi, preferred_element_type=F32).astype(BF16) for pi in pb]
                x = [xi + _dot(pi, xi) for pi, xi in zip(pb, x)]
            xb = [xi.astype(BF16) for xi in x]
            qk = [jnp.where(incl, _dot_nt(qi, ki) * dc, 0.0) for qi, ki, dc in zip(q, k, decay)]
            qx = [_dot(qki, xi) for qki, xi in zip(qk, xb)]
            for g in range(DN_GROUPS_PER_ITER):
                last = [gcb[g][(i + 1) * CHUNK - 1:(i + 1) * CHUNK, :] for i in range(DN_GROUP)]
                g_last = jnp.concatenate([jnp.broadcast_to(li, (CHUNK, d)) for li in last], axis=0)
                kd = (k[g] * jnp.exp(g_last - gcb[g])).astype(BF16)
                qp = (q[g] * eg[g] - qx[g][:, :d]).astype(BF16)
                for i in range(DN_GROUP):
                    c = groups[g] * DN_GROUP + i
                    cr = slice(i * CHUNK, (i + 1) * CHUNK)
                    kx = _dot_tn(kd[cr, :], xb[g][cr, :])
                    qm_s[hh, c, :CHUNK, :] = qp[cr, :]
                    qm_s[hh, c, CHUNK:, :] = kx[:, :d].astype(BF16)
                    n_s[hh, c] = kx[:, d:]
                    cd_s[hh, pl.ds(c, 1), :] = jnp.exp(last[i])
                op_s[hh, rows[g], :] = qx[g][:, d:]
            return carry

        lax.fori_loop(0, seq // (DN_ROWS * DN_GROUPS_PER_ITER), prep, 0)

    def emit(c, outs):
        r = pl.ds(pl.multiple_of(c * CHUNK, CHUNK), CHUNK)
        for hh in range(DN_PAIR):
            cols = slice(hh * d, (hh + 1) * d)
            o = outs[hh] + op_s[hh, r, :]
            o_ref[0, r, cols] = (_rms(o, nw) * _silu(z_ref[0, r, cols])).astype(o_ref.dtype)

    def step(c, carry):
        states, outs = carry
        prods = [jnp.dot(qm_s[hh, c], states[hh].astype(BF16), preferred_element_type=F32) for hh in range(DN_PAIR)]
        emit(jnp.maximum(c - 1, 0), outs)
        new = tuple(cd_s[hh, pl.ds(c, 1), :] * states[hh] + n_s[hh, c] - prods[hh][CHUNK:, :]
                    for hh in range(DN_PAIR))
        return new, tuple(p[:CHUNK, :] for p in prods)

    zeros = lambda rows: tuple(jnp.zeros((rows, d), F32) for _ in range(DN_PAIR))
    _, outs = lax.fori_loop(0, n_chunks, step, (zeros(d), zeros(CHUNK)))
    emit(n_chunks - 1, outs)


def _deltanet(main, small, conv_w, a_log, dt_bias, norm_w):
    bsz, seq, _ = main.shape
    n_chunks = seq // CHUNK
    d = DN_HEAD_DIM
    pw = DN_PAIR * d
    n_pairs = DN_HEADS // DN_PAIR
    n_groups = seq // DN_ROWS
    assert seq % (DN_ROWS * DN_GROUPS_PER_ITER) == 0 and DN_ROWS % d == 0
    ab_t = small[:, :, :2 * DN_HEADS].transpose(0, 2, 1).reshape(bsz, 2 * DN_HEADS, n_groups, DN_ROWS)
    cw = conv_w.reshape(DN_CONV, 3 * DN_WIDTH)
    cum = jnp.kron(jnp.eye(DN_GROUP, dtype=F32), jnp.triu(jnp.ones((CHUNK, CHUNK), F32)))
    sel = jnp.kron((jnp.arange(SEL_ROWS)[:, None] // N_SPLIT == jnp.arange(2)[None, :]).astype(BF16),
                   jnp.ones((1, d), BF16))
    col = lambda off: pl.BlockSpec((1, seq, pw), lambda b, h: (b, 0, off + h))
    cwspec = lambda off: pl.BlockSpec((DN_CONV, pw), lambda b, h: (0, off + h))
    tspec = lambda off: pl.BlockSpec((1, DN_PAIR, n_groups, DN_ROWS), lambda b, h: (b, off + h, 0, 0))
    one = pl.BlockSpec((DN_PAIR, 1, 1), lambda b, h: (h, 0, 0))
    return pl.pallas_call(
        _deltanet_kernel,
        grid=(bsz, n_pairs),
        in_specs=[col(0), col(n_pairs), col(2 * n_pairs), col(3 * n_pairs),
                  tspec(0), tspec(n_pairs),
                  cwspec(0), cwspec(n_pairs), cwspec(2 * n_pairs),
                  one, one,
                  pl.BlockSpec((1, d), lambda b, h: (0, 0)),
                  pl.BlockSpec((DN_ROWS, DN_ROWS), lambda b, h: (0, 0)),
                  pl.BlockSpec((SEL_ROWS, 2 * d), lambda b, h: (0, 0))],
        out_specs=pl.BlockSpec((1, seq, pw), lambda b, h: (b, 0, h)),
        out_shape=jax.ShapeDtypeStruct((bsz, seq, DN_WIDTH), BF16),
        scratch_shapes=[pltpu.VMEM((seq + SUBLANES, d), F32)] + [pltpu.VMEM((seq, d), F32)] * 3 + [
                        pltpu.VMEM((n_groups, DN_ROWS), F32),
                        pltpu.VMEM((2 * N_SPLIT, n_groups, DN_ROWS), F32),
                        pltpu.VMEM((DN_PAIR, n_chunks, CHUNK + d, d), BF16),
                        pltpu.VMEM((DN_PAIR, n_chunks, d, d), F32),
                        pltpu.VMEM((DN_PAIR, seq, d), F32),
                        pltpu.VMEM((DN_PAIR, n_chunks, d), F32)],
        compiler_params=_params(("parallel", "arbitrary")),
        name="deltanet",
    )(main, main, main, main, ab_t, ab_t, cw, cw, cw,
      a_log.reshape(DN_HEADS, 1, 1), dt_bias.reshape(DN_HEADS, 1, 1), norm_w.reshape(1, d), cum, sel)


def _gelu_tanh(x):
    return x * (0.5 * (1.0 + jnp.tanh(0.7978845608028654 * (x + 0.044715 * (x * x * x)))))


def _s5_kernel(u_ref, are_ref, aim_ref, ls_ref, bre_ref, bim_ref, cre_ref, cim_ref, d_ref, o_ref,
               ut, xs, xb, yt, state, bbd, cbd, lam):
    ns = S5_TILE_STATE
    bsz, ts, _ = u_ref.shape

    @pl.when(pl.program_id(1) == 0)
    def _():
        a_re = are_ref[0]
        a_im = aim_ref[0]
        dt = jnp.exp(ls_ref[0])
        mag = jnp.exp(a_re * dt)
        l_re = mag * jnp.cos(a_im * dt)
        l_im = mag * jnp.sin(a_im * dt)
        den = a_re * a_re + a_im * a_im
        c_re = ((l_re - 1.0) * a_re + l_im * a_im) / den
        c_im = (l_im * a_re - (l_re - 1.0) * a_im) / den
        b_re = bre_ref[0]
        b_im = bim_ref[0]
        bbd[:, :ns] = (c_re * b_re - c_im * b_im).astype(BF16)
        bbd[:, ns:] = (c_re * b_im + c_im * b_re).astype(BF16)
        cbd[:ns, :] = cre_ref[0].astype(BF16)
        cbd[ns:, :] = (-cim_ref[0]).astype(BF16)
        lam[:, :ns] = jnp.broadcast_to(l_re, (bsz, ns))
        lam[:, ns:] = jnp.broadcast_to(l_im, (bsz, ns))
        state[...] = jnp.zeros_like(state)

    for b in range(bsz):
        ut[pl.ds(b, ts, stride=bsz), :] = u_ref[b]
    xs[...] = jnp.dot(ut[...].astype(BF16), bbd[...], preferred_element_type=F32)
    l_re = lam[:, :ns]
    l_im = lam[:, ns:]

    frames = S5_SCAN_ROWS // bsz

    def step(i, carry):
        x_re, x_im = carry
        r = pl.ds(pl.multiple_of(i * S5_SCAN_ROWS, S5_SCAN_ROWS), S5_SCAN_ROWS)
        bu = xs[r, :]
        res_re, res_im = [], []
        for f in range(frames):
            rows = slice(f * bsz, (f + 1) * bsz)
            x_re, x_im = (l_re * x_re - l_im * x_im + bu[rows, :ns], l_re * x_im + l_im * x_re + bu[rows, ns:])
            res_re.append(x_re)
            res_im.append(x_im)
        xb[r, :ns] = jnp.concatenate(res_re, axis=0).astype(BF16)
        xb[r, ns:] = jnp.concatenate(res_im, axis=0).astype(BF16)
        return x_re, x_im

    x_re, x_im = lax.fori_loop(0, ts // frames, step, (state[:, :ns], state[:, ns:]))
    state[:, :ns] = x_re
    state[:, ns:] = x_im
    y = jnp.dot(xb[...], cbd[...], preferred_element_type=F32) + d_ref[0] * ut[...]
    yt[...] = _gelu_tanh(y)
    for b in range(bsz):
        o_ref[b] = yt[pl.ds(b, ts, stride=bsz), :]


def _s5(main, u_col_block, a_re, a_im, b_re, b_im, c_re, c_im, d, log_step, *, ts=256):
    bsz, seq, _ = main.shape
    assert bsz == SUBLANES
    tg, ns, nt = S5_TILE_GROUPS, S5_TILE_STATE, S5_TILES
    eye = jnp.eye(tg, dtype=F32)

    def expand_b(b):
        bt = b.reshape(nt, tg, S5_STATE, S5_GROUP)
        return jnp.einsum('ngph,gk->nghkp', bt, eye).reshape(nt, LANES, ns)

    def expand_c(c):
        ct = c.reshape(nt, tg, S5_GROUP, S5_STATE)
        return jnp.einsum('nghp,gk->ngpkh', ct, eye).reshape(nt, ns, LANES)

    chan = lambda a: a.reshape(nt, 1, ns)
    ls = jnp.broadcast_to(log_step[:, None], (S5_GROUPS, S5_STATE))
    pspec = pl.BlockSpec((1, 1, ns), lambda c, t: (c, 0, 0))
    bspec = pl.BlockSpec((1, LANES, ns), lambda c, t: (c, 0, 0))
    cspec = pl.BlockSpec((1, ns, LANES), lambda c, t: (c, 0, 0))
    return pl.pallas_call(
        _s5_kernel,
        grid=(nt, seq // ts),
        in_specs=[pl.BlockSpec((bsz, ts, LANES), lambda c, t: (0, t, u_col_block + c)),
                  pspec, pspec, pspec, bspec, bspec, cspec, cspec,
                  pl.BlockSpec((1, 1, LANES), lambda c, t: (c, 0, 0))],
        out_specs=pl.BlockSpec((bsz, ts, LANES), lambda c, t: (0, t, c)),
        out_shape=jax.ShapeDtypeStruct((bsz, seq, S5_WIDTH), F32),
        scratch_shapes=[pltpu.VMEM((bsz * ts, LANES), F32),
                        pltpu.VMEM((bsz * ts, 2 * ns), F32),
                        pltpu.VMEM((bsz * ts, 2 * ns), BF16),
                        pltpu.VMEM((bsz * ts, LANES), F32),
                        pltpu.VMEM((bsz, 2 * ns), F32),
                        pltpu.VMEM((LANES, 2 * ns), BF16),
                        pltpu.VMEM((2 * ns, LANES), BF16),
                        pltpu.VMEM((bsz, 2 * ns), F32)],
        compiler_params=_params(("parallel", "arbitrary")),
        name="s5",
    )(main, chan(a_re), chan(a_im), chan(ls), expand_b(b_re), expand_b(b_im), expand_c(c_re), expand_c(c_im),
      d.reshape(nt, 1, LANES))


def _glu_kernel(y_ref, w_ref, b_ref, o_ref):
    y = y_ref[...]
    gate = jnp.dot(y.astype(BF16), w_ref[...], preferred_element_type=F32) + b_ref[...]
    o_ref[...] = (y * jax.nn.sigmoid(gate)).astype(o_ref.dtype)


def _glu(y2, w, b, *, tm=512):
    t, n = y2.shape
    return pl.pallas_call(
        _glu_kernel,
        grid=(t // tm,),
        in_specs=[pl.BlockSpec((tm, n), lambda i: (i, 0)),
                  pl.BlockSpec((n, n), lambda i: (0, 0)),
                  pl.BlockSpec((1, n), lambda i: (0, 0))],
        out_specs=pl.BlockSpec((tm, n), lambda i: (i, 0)),
        out_shape=jax.ShapeDtypeStruct((t, n), BF16),
        compiler_params=_params(("parallel",)),
        name="s5_glu",
    )(y2, w, b.reshape(1, n))


def _gla_scan(q_at, k_at, v_at, bs, gate_ref, nw_ref, o_ref, st_s):
    n_chunks, dv, dk = st_s.shape
    incl, _ = _tri_masks()
    nw = nw_ref[...]

    def chunk_rows(it):
        cs = [it * GLA_GROUP + i for i in range(GLA_GROUP)]
        return cs, [pl.ds(pl.multiple_of(c * CHUNK, CHUNK), CHUNK) for c in cs]

    def states(it, state_t):
        cs, rows = chunk_rows(it)
        b = [bs[r, :] for r in rows]
        b_last = [bi[CHUNK - 1:CHUNK, :] for bi in b]
        inc = [_dot_tn(v_at(r), k_at(r) * jnp.exp(bl - bi)) for r, bl, bi in zip(rows, b_last, b)]
        for c, bl, ic in zip(cs, b_last, inc):
            st_s[c] = state_t.astype(BF16)
            state_t = jnp.exp(bl) * state_t + ic
        return state_t

    lax.fori_loop(0, n_chunks // GLA_GROUP, states, jnp.zeros((dv, dk), F32))

    def outputs(it, carry):
        cs, rows = chunk_rows(it)
        q = [q_at(r) for r in rows]
        b = [bs[r, :] for r in rows]
        b_mid = [bi[CHUNK // 2 - 1:CHUNK // 2, :] for bi in b]
        att = [jnp.where(incl, _dot_nt(qi * jnp.exp(bi - bm), k_at(r) * jnp.exp(bm - bi)), 0.0)
               for qi, bi, bm, r in zip(q, b, b_mid, rows)]
        o_inter = [_dot_nt(qi * jnp.exp(bi), st_s[c]) for qi, bi, c in zip(q, b, cs)]
        o = [_dot(ai, v_at(r)) + oi for ai, r, oi in zip(att, rows, o_inter)]
        for r, oi in zip(rows, o):
            o_ref[0, r, :] = (_rms(oi, nw) * _silu(gate_ref[0, r, :])).astype(o_ref.dtype)
        return carry

    lax.fori_loop(0, n_chunks // GLA_GROUP, outputs, 0)


def _gla_kernel(q_ref, k_ref, v_ref, r_ref, lr_ref, w2_ref, gb_ref, nw_ref, o_ref, bs, st_s):
    gate = _dot(lr_ref[0], w2_ref[...]) + gb_ref[...]
    log_a = -_softplus(-gate) / GLA_TAU
    bs[...] = _chunk_cumsum(log_a)
    _gla_scan(lambda r: q_ref[0, r, :] * GLA_HEAD_K ** -0.5, lambda r: k_ref[0, r, :], lambda r: v_ref[0, r, :],
              bs, r_ref, nw_ref, o_ref, st_s)


def _hgrn2_kernel(q_ref, f_ref, i_ref, g_ref, lbl_ref, nw_ref, o_ref, ks, bs, st_s, *, layer):
    logits = lbl_ref[...]
    e = jnp.exp(logits - jnp.max(logits, axis=0, keepdims=True))
    p = e / jnp.sum(e, axis=0, keepdims=True)
    lb_first = p[0:1, :]
    lb_layer = lb_first
    for i in range(1, layer + 1):
        lb_layer = lb_layer + p[i:i + 1, :]
    lb = lb_layer - lb_first
    z_f = f_ref[0]
    e_z = jnp.exp(-jnp.abs(z_f))
    big = 1.0 / (1.0 + e_z)
    small = e_z * big
    sig_pos = jnp.where(z_f >= 0.0, big, small)
    sig_neg = jnp.where(z_f >= 0.0, small, big)
    ks[...] = (1.0 - lb) * sig_neg
    bs[...] = _chunk_cumsum(jnp.log(lb + (1.0 - lb) * sig_pos))
    _gla_scan(lambda r: q_ref[0, r, :], lambda r: ks[r, :], lambda r: i_ref[0, r, :],
              bs, g_ref, nw_ref, o_ref, st_s)


def _gla(main, small, w2, gate_b, norm_w):
    bsz, seq, _ = main.shape
    dk, dv = GLA_HEAD_K, GLA_HEAD_V
    w2p = jnp.concatenate([w2, jnp.zeros((LANES - GLA_GATE_RANK, GLA_KEY_WIDTH), F32)], axis=0).astype(BF16)
    kspec = lambda off: pl.BlockSpec((1, seq, dk), lambda b, h: (b, 0, off + h))
    vspec = lambda off: pl.BlockSpec((1, seq, dv), lambda b, h: (b, 0, off + h))
    return pl.pallas_call(
        _gla_kernel,
        grid=(bsz, GLA_HEADS),
        in_specs=[kspec(0), kspec(GLA_HEADS), vspec(GLA_HEADS), vspec(2 * GLA_HEADS),
                  pl.BlockSpec((1, seq, LANES), lambda b, h: (b, 0, 0)),
                  pl.BlockSpec((LANES, dk), lambda b, h: (0, h)),
                  pl.BlockSpec((1, dk), lambda b, h: (0, h)),
                  pl.BlockSpec((1, dv), lambda b, h: (0, 0))],
        out_specs=pl.BlockSpec((1, seq, dv), lambda b, h: (b, 0, h)),
        out_shape=jax.ShapeDtypeStruct((bsz, seq, GLA_WIDTH), BF16),
        scratch_shapes=[pltpu.VMEM((seq, dk), F32), pltpu.VMEM((seq // CHUNK, dv, dk), BF16)],
        compiler_params=_params(("parallel", "arbitrary")),
        name="gla",
    )(main, main, main, main, small, w2p, gate_b.reshape(1, GLA_KEY_WIDTH), norm_w.reshape(1, dv))


def _hgrn2(main, lb_logits, layer, norm_w):
    bsz, seq, _ = main.shape
    d = HG_EXPAND
    base = (2 * GLA_KEY_WIDTH + 2 * GLA_WIDTH) // d
    col = lambda off: pl.BlockSpec((1, seq, d), lambda b, h: (b, 0, base + off + h))
    depth = lb_logits.shape[0]
    return pl.pallas_call(
        functools.partial(_hgrn2_kernel, layer=layer),
        grid=(bsz, HG_HEADS),
        in_specs=[col(0), col(HG_HEADS), col(2 * HG_HEADS), col(3 * HG_HEADS),
                  pl.BlockSpec((depth, d), lambda b, h: (0, h)),
                  pl.BlockSpec((1, d), lambda b, h: (0, 0))],
        out_specs=pl.BlockSpec((1, seq, d), lambda b, h: (b, 0, h)),
        out_shape=jax.ShapeDtypeStruct((bsz, seq, HG_WIDTH), BF16),
        scratch_shapes=[pltpu.VMEM((seq, d), F32)] * 2 + [pltpu.VMEM((seq // CHUNK, d, d), BF16)],
        compiler_params=_params(("parallel", "arbitrary")),
        name="hgrn2",
    )(main, main, main, main, lb_logits, norm_w.reshape(1, d))


def _pad_cols(w):
    return jnp.concatenate([w, jnp.zeros((w.shape[0], LANES - w.shape[1]), w.dtype)], axis=1)


def _even_mixer(x2, bsz, seq, mix_norm, w_in, conv_w, a_log, dt_bias, dn_norm_w, s5_a_re, s5_a_im, s5_b_re, s5_b_im,
                s5_c_re, s5_c_im, s5_d, s5_log_step, s5_glu_w, s5_glu_b, w_out):
    n_qkvz = 4 * DN_WIDTH
    n_ab = 2 * DN_HEADS
    w_in = w_in.astype(BF16)
    w_main = jnp.concatenate([w_in[:, :n_qkvz], w_in[:, n_qkvz + n_ab:]], axis=1)
    w_small = _pad_cols(w_in[:, n_qkvz:n_qkvz + n_ab])
    main, small = _inproj(x2, mix_norm, w_main, w_small)
    main = main.reshape(bsz, seq, -1)
    small = small.reshape(bsz, seq, LANES)
    y_a = _deltanet(main, small, conv_w, a_log, dt_bias, dn_norm_w)
    y_s5 = _s5(main, n_qkvz // LANES, s5_a_re, s5_a_im, s5_b_re, s5_b_im, s5_c_re, s5_c_im, s5_d, s5_log_step)
    y_b = _glu(y_s5.reshape(bsz * seq, S5_WIDTH), s5_glu_w.astype(BF16), s5_glu_b)
    return _outproj(x2, y_a.reshape(bsz * seq, DN_WIDTH), y_b, w_out.astype(BF16))


def _odd_mixer(x2, bsz, seq, layer, lb_logits, mix_norm, w_in, gate_w2, gate_b, gla_norm_w, hg_norm_w, w_out):
    n_c = 2 * GLA_KEY_WIDTH + 2 * GLA_WIDTH
    w_in = w_in.astype(BF16)
    w_main = jnp.concatenate([w_in[:, :n_c], w_in[:, n_c + GLA_GATE_RANK:]], axis=1)
    w_small = _pad_cols(w_in[:, n_c:n_c + GLA_GATE_RANK])
    main, small = _inproj(x2, mix_norm, w_main, w_small)
    main = main.reshape(bsz, seq, -1)
    small = small.reshape(bsz, seq, LANES)
    y_c = _gla(main, small, gate_w2, gate_b, gla_norm_w)
    y_d = _hgrn2(main, lb_logits, layer, hg_norm_w)
    return _outproj(x2, y_c.reshape(bsz * seq, GLA_WIDTH), y_d.reshape(bsz * seq, HG_WIDTH), w_out.astype(BF16))


def kernel(x, l0_ffn1_norm, l0_ffn1_w_gate, l0_ffn1_w_up, l0_ffn1_w_down, l0_mix_norm, l0_w_in, l0_dn_conv_w, l0_dn_a_log, l0_dn_dt_bias, l0_dn_norm_w, l0_s5_a_re, l0_s5_a_im, l0_s5_b_re, l0_s5_b_im, l0_s5_c_re, l0_s5_c_im, l0_s5_d, l0_s5_log_step, l0_s5_glu_w, l0_s5_glu_b, l0_w_out, l0_ffn2_norm, l0_ffn2_w_gate, l0_ffn2_w_up, l0_ffn2_w_down, l1_ffn1_norm, l1_ffn1_w_gate, l1_ffn1_w_up, l1_ffn1_w_down, l1_mix_norm, l1_w_in, l1_gla_gate_w2, l1_gla_gate_b, l1_gla_norm_w, l1_hg_norm_w, l1_w_out, l1_ffn2_norm, l1_ffn2_w_gate, l1_ffn2_w_up, l1_ffn2_w_down, hgrn_lb_logits, final_norm):
    bsz, seq, d = x.shape
    x2 = x.reshape(bsz * seq, d)

    w_l0f1 = tuple(w.astype(BF16) for w in (l0_ffn1_w_gate, l0_ffn1_w_up, l0_ffn1_w_down))
    x2, w_l0f2 = _ffn(x2, l0_ffn1_norm, *w_l0f1, final_norm, final=False,
                      cast_next=(l0_ffn2_w_gate, l0_ffn2_w_up, l0_ffn2_w_down))
    x2 = _even_mixer(x2, bsz, seq, l0_mix_norm, l0_w_in, l0_dn_conv_w, l0_dn_a_log, l0_dn_dt_bias, l0_dn_norm_w,
                     l0_s5_a_re, l0_s5_a_im, l0_s5_b_re, l0_s5_b_im, l0_s5_c_re, l0_s5_c_im, l0_s5_d, l0_s5_log_step,
                     l0_s5_glu_w, l0_s5_glu_b, l0_w_out)
    x2, w_l1f1 = _ffn(x2, l0_ffn2_norm, *w_l0f2, final_norm, final=False,
                      cast_next=(l1_ffn1_w_gate, l1_ffn1_w_up, l1_ffn1_w_down))
    x2, w_l1f2 = _ffn(x2, l1_ffn1_norm, *w_l1f1, final_norm, final=False,
                      cast_next=(l1_ffn2_w_gate, l1_ffn2_w_up, l1_ffn2_w_down))
    x2 = _odd_mixer(x2, bsz, seq, 1, hgrn_lb_logits, l1_mix_norm, l1_w_in, l1_gla_gate_w2, l1_gla_gate_b,
                    l1_gla_norm_w, l1_hg_norm_w, l1_w_out)
    x2, _ = _ffn(x2, l1_ffn2_norm, *w_l1f2, final_norm, final=True)
    return x2.reshape(bsz, seq, d)
```

```python
import functools

import jax
import jax.numpy as jnp
from jax import lax
from jax.experimental import pallas as pl
from jax.experimental.pallas import tpu as pltpu

F32 = jnp.float32
BF16 = jnp.bfloat16

D_MODEL = 2048
CHUNK = 64
NORM_EPS = 1e-6
D_FF = 5632
FFN_RES = 0.5
DN_HEADS = 8
DN_HEAD_DIM = 128
DN_WIDTH = DN_HEADS * DN_HEAD_DIM
DN_CONV = 4
S5_WIDTH = D_MODEL - DN_WIDTH
S5_GROUP = 16
S5_GROUPS = S5_WIDTH // S5_GROUP
S5_STATE = 64
GLA_HEADS = 4
GLA_WIDTH = D_MODEL // 2
GLA_KEY_WIDTH = GLA_WIDTH // 2
GLA_HEAD_K = GLA_KEY_WIDTH // GLA_HEADS
GLA_HEAD_V = GLA_WIDTH // GLA_HEADS
GLA_GATE_RANK = 16
GLA_TAU = 16.0
HG_WIDTH = D_MODEL - GLA_WIDTH
HG_EXPAND = 128
HG_HEADS = HG_WIDTH // HG_EXPAND

LANES = 128
SUBLANES = 8
BF16_ROWS = 16
VMEM_LIMIT = 56 * 1024 * 1024
FFN_VMEM_LIMIT = 58 * 1024 * 1024

S5_TILE_GROUPS = LANES // S5_GROUP
S5_TILE_STATE = S5_TILE_GROUPS * S5_STATE
S5_TILES = S5_WIDTH // LANES
S5_SCAN_ROWS = 4 * BF16_ROWS

DN_PAIR = 2
DN_GROUP = 2
DN_ROWS = DN_GROUP * CHUNK
DN_GROUPS_PER_ITER = 16
N_SPLIT = 3
SEL_ROWS = BF16_ROWS
GLA_GROUP = 32


def _params(sem, vmem_limit=VMEM_LIMIT):
    return pltpu.CompilerParams(dimension_semantics=sem, vmem_limit_bytes=vmem_limit)


def _rms(x, w):
    return x * lax.rsqrt(jnp.mean(x * x, axis=-1, keepdims=True) + NORM_EPS) * w


def _silu(x):
    return x * jax.nn.sigmoid(x)


def _softplus(x):
    return jnp.maximum(x, 0.0) + jnp.log1p(jnp.exp(-jnp.abs(x)))


def _dot(a, b):
    return jnp.dot(a.astype(BF16), b.astype(BF16), preferred_element_type=F32)


def _dot_nt(a, b):
    return lax.dot_general(a.astype(BF16), b.astype(BF16), (((1,), (1,)), ((), ())), preferred_element_type=F32)


def _dot_tn(a, b):
    return lax.dot_general(a.astype(BF16), b.astype(BF16), (((0,), (0,)), ((), ())), preferred_element_type=F32)


def _chunk_cumsum(x):
    pos = lax.broadcasted_iota(jnp.int32, x.shape, 0) % CHUNK
    shift = 1
    while shift < CHUNK:
        x = x + jnp.where(pos >= shift, pltpu.roll(x, shift, axis=0), 0.0)
        shift *= 2
    return x


def _tri_masks(n=CHUNK):
    row = lax.broadcasted_iota(jnp.int32, (n, n), 0)
    col = lax.broadcasted_iota(jnp.int32, (n, n), 1)
    same = (row // CHUNK) == (col // CHUNK)
    return same & (row >= col), same & (row > col)


def _ffn_kernel(x_ref, nw_ref, wg_ref, wu_ref, wd_ref, fw_ref, *rest, n_ff_tiles, final, n_cast):
    cast_in = rest[:n_cast]
    o_ref = rest[n_cast]
    cast_out = rest[n_cast + 1:2 * n_cast + 1]
    xn_ref = rest[2 * n_cast + 1]
    j = pl.program_id(1)
    for src, dst in zip(cast_in, cast_out):
        dst[...] = src[...].astype(BF16)

    @pl.when(j == 0)
    def _():
        x = x_ref[...]
        xn_ref[...] = _rms(x, nw_ref[...]).astype(BF16)
        o_ref[...] = x

    xn = xn_ref[...]
    g = jnp.dot(xn, wg_ref[...], preferred_element_type=F32)
    u = jnp.dot(xn, wu_ref[...], preferred_element_type=F32)
    h = (FFN_RES * (_silu(g) * u)).astype(BF16)
    o_ref[...] += jnp.dot(h, wd_ref[...], preferred_element_type=F32)

    if final:
        @pl.when(j == n_ff_tiles - 1)
        def _():
            o_ref[...] = _rms(o_ref[...], fw_ref[...])


def _grid_tiling(rows, cols, n_i, n_j):
    for (r, c, index_map) in (((rows // n_i, cols // n_j, lambda i, j: (i, j))),
                              ((rows // n_j, cols // n_i, lambda i, j: (j, i)))):
        if r % BF16_ROWS == 0 and c % LANES == 0 and r * c * n_i * n_j == rows * cols:
            return pl.BlockSpec((r, c), index_map)
    raise ValueError("no tile-aligned one-tile-per-step cover")


def _ffn(x2, nw, wg, wu, wd, fw, *, final, cast_next=None, tm=1024, tf=512):
    t, d = x2.shape
    f = wg.shape[1]
    n_row, n_ff = t // tm, f // tf
    in_specs = [pl.BlockSpec((tm, d), lambda i, j: (i, 0)),
                pl.BlockSpec((1, d), lambda i, j: (0, 0)),
                pl.BlockSpec((d, tf), lambda i, j: (0, j)),
                pl.BlockSpec((d, tf), lambda i, j: (0, j)),
                pl.BlockSpec((tf, d), lambda i, j: (j, 0)),
                pl.BlockSpec((1, d), lambda i, j: (0, 0))]
    out_specs = [pl.BlockSpec((tm, d), lambda i, j: (i, 0))]
    out_shape = [jax.ShapeDtypeStruct((t, d), F32)]
    operands = [x2, nw.reshape(1, d), wg, wu, wd, fw.reshape(1, d)]
    if cast_next is not None:
        assert all(w.shape == s for w, s in zip(cast_next, ((d, f), (d, f), (f, d))))
        up_spec = _grid_tiling(d, f, n_row, n_ff)
        down_spec = _grid_tiling(f, d, n_row, n_ff)
        in_specs += [up_spec, up_spec, down_spec]
        out_specs += [up_spec, up_spec, down_spec]
        out_shape += [jax.ShapeDtypeStruct(w.shape, BF16) for w in cast_next]
        operands += list(cast_next)
    res = pl.pallas_call(
        functools.partial(_ffn_kernel, n_ff_tiles=n_ff, final=final, n_cast=0 if cast_next is None else 3),
        grid=(n_row, n_ff),
        in_specs=in_specs,
        out_specs=out_specs,
        out_shape=out_shape,
        scratch_shapes=[pltpu.VMEM((tm, d), BF16)],
        compiler_params=_params(("parallel", "arbitrary"), FFN_VMEM_LIMIT),
        name="ffn",
    )(*operands)
    return res[0], tuple(res[1:])


def _inproj_kernel(x_ref, nw_ref, w_ref, ws_ref, o_ref, os_ref, xn_ref):
    @pl.when(pl.program_id(1) == 0)
    def _():
        xn = _rms(x_ref[...], nw_ref[...]).astype(BF16)
        xn_ref[...] = xn
        os_ref[...] = jnp.dot(xn, ws_ref[...], preferred_element_type=F32)

    o_ref[...] = jnp.dot(xn_ref[...], w_ref[...], preferred_element_type=F32)


def _inproj(x2, nw, w_main, w_small, *, tm=1024, tn=1024):
    t, d = x2.shape
    n = w_main.shape[1]
    return pl.pallas_call(
        _inproj_kernel,
        grid=(t // tm, n // tn),
        in_specs=[pl.BlockSpec((tm, d), lambda i, j: (i, 0)),
                  pl.BlockSpec((1, d), lambda i, j: (0, 0)),
                  pl.BlockSpec((d, tn), lambda i, j: (0, j)),
                  pl.BlockSpec((d, LANES), lambda i, j: (0, 0))],
        out_specs=[pl.BlockSpec((tm, tn), lambda i, j: (i, j)),
                   pl.BlockSpec((tm, LANES), lambda i, j: (i, 0))],
        out_shape=[jax.ShapeDtypeStruct((t, n), F32), jax.ShapeDtypeStruct((t, LANES), F32)],
        scratch_shapes=[pltpu.VMEM((tm, d), BF16)],
        compiler_params=_params(("parallel", "arbitrary")),
        name="inproj",
    )(x2, nw.reshape(1, d), w_main, w_small)


def _outproj_kernel(x_ref, ya_ref, yb_ref, wa_ref, wb_ref, o_ref):
    o_ref[...] = (x_ref[...] + jnp.dot(ya_ref[...], wa_ref[...], preferred_element_type=F32)
                  + jnp.dot(yb_ref[...], wb_ref[...], preferred_element_type=F32))


def _outproj(x2, ya, yb, w, *, tm=512):
    t, d = x2.shape
    ka = ya.shape[1]
    kb = yb.shape[1]
    assert ka == kb
    return pl.pallas_call(
        _outproj_kernel,
        grid=(t // tm,),
        in_specs=[pl.BlockSpec((tm, d), lambda i: (i, 0)),
                  pl.BlockSpec((tm, ka), lambda i: (i, 0)),
                  pl.BlockSpec((tm, kb), lambda i: (i, 0)),
                  pl.BlockSpec((ka, d), lambda i: (0, 0)),
                  pl.BlockSpec((kb, d), lambda i: (1, 0))],
        out_specs=pl.BlockSpec((tm, d), lambda i: (i, 0)),
        out_shape=jax.ShapeDtypeStruct((t, d), F32),
        compiler_params=_params(("parallel",)),
        name="outproj",
    )(x2, ya, yb, w, w)


def _causal_conv(x, w, xpad):
    seq = x.shape[0]
    xpad[SUBLANES:, :] = x
    y = x * w[DN_CONV - 1:DN_CONV, :]
    for back in range(1, DN_CONV):
        y = y + xpad[pl.ds(SUBLANES - back, seq), :] * w[DN_CONV - 1 - back:DN_CONV - back, :]
    return y


def _split3(x):
    parts = []
    for _ in range(N_SPLIT):
        p = x.astype(BF16).astype(F32)
        parts.append(p)
        x = x - p
    return parts


def _l2norm(x):
    return x * lax.rsqrt(jnp.sum(x * x, axis=-1, keepdims=True) + NORM_EPS)


def _deltanet_kernel(q_ref, k_ref, v_ref, z_ref, at_ref, bt_ref, cwq_ref, cwk_ref, cwv_ref,
                     alog1_ref, dtb1_ref, nw_ref, cum_ref, sel_ref, o_ref,
                     xpad, qs, ks, vs, gts, parts_s, qm_s, n_s, op_s, cd_s):
    seq = q_ref.shape[1]
    n_chunks = seq // CHUNK
    d = DN_HEAD_DIM
    incl, strict = _tri_masks(DN_ROWS)
    nw = nw_ref[...]
    sel = sel_ref[...]
    xpad[:SUBLANES, :] = jnp.zeros((SUBLANES, d), F32)

    for hh in range(DN_PAIR):
        cols = slice(hh * d, (hh + 1) * d)
        qs[...] = _l2norm(_silu(_causal_conv(q_ref[0, :, cols], cwq_ref[:, cols], xpad))) * DN_HEAD_DIM ** -0.5
        ks[...] = _l2norm(_silu(_causal_conv(k_ref[0, :, cols], cwk_ref[:, cols], xpad)))
        vs[...] = _silu(_causal_conv(v_ref[0, :, cols], cwv_ref[:, cols], xpad))
        g_t = -jnp.exp(alog1_ref[hh]) * _softplus(at_ref[0, hh] + dtb1_ref[hh])
        gc_t = jnp.dot(g_t, cum_ref[...], preferred_element_type=F32, precision=lax.Precision.HIGHEST)
        gts[...] = gc_t
        for i, part in enumerate(_split3(gc_t) + _split3(jax.nn.sigmoid(bt_ref[0, hh]))):
            parts_s[i] = part

        def prep(it, carry):
            groups = [it * DN_GROUPS_PER_ITER + g for g in range(DN_GROUPS_PER_ITER)]
            rows = [pl.ds(pl.multiple_of(gi * DN_ROWS, DN_ROWS), DN_ROWS) for gi in groups]
            q = [qs[r, :] for r in rows]
            k = [ks[r, :] for r in rows]
            src = [jnp.concatenate([parts_s[i, pl.ds(gi, 1), :] for i in range(2 * N_SPLIT)]
                                   + [jnp.zeros((SEL_ROWS - 2 * N_SPLIT, DN_ROWS), F32)], axis=0) for gi in groups]
            colb = [_dot_tn(s, sel) for s in src]
            gcb = [c[:, :d] for c in colb]
            beta = [c[:, d:] for c in colb]
            decay = [jnp.exp(jnp.where(incl, jnp.concatenate([gc] * (DN_ROWS // d), axis=1) - gts[pl.ds(gi, 1), :],
                                       -jnp.inf)) for gc, gi in zip(gcb, groups)]
            kb = [ki * bi for ki, bi in zip(k, beta)]
            a = [jnp.where(strict, _dot_nt(kbi, ki) * dc, 0.0) for kbi, ki, dc in zip(kb, k, decay)]
            eg = [jnp.exp(gc) for gc in gcb]
            x = [jnp.concatenate([kbi * egi, vs[r, :] * bi], axis=1) for kbi, egi, r, bi in zip(kb, eg, rows, beta)]
            pb = [ai.astype(BF16) for ai in a]
            x = [xi - _dot(pi, xi) for pi, xi in zip(pb, x)]
            for _ in range(CHUNK.bit_length() - 2):
                pb = [jnp.dot(pi, pi, preferred_element_type=F32).astype(BF16) for pi in pb]
                x = [xi + _dot(pi, xi) for pi, xi in zip(pb, x)]
            xb = [xi.astype(BF16) for xi in x]
            qk = [jnp.where(incl, _dot_nt(qi, ki) * dc, 0.0) for qi, ki, dc in zip(q, k, decay)]
            qx = [_dot(qki, xi) for qki, xi in zip(qk, xb)]
            for g in range(DN_GROUPS_PER_ITER):
                last = [gcb[g][(i + 1) * CHUNK - 1:(i + 1) * CHUNK, :] for i in range(DN_GROUP)]
                g_last = jnp.concatenate([jnp.broadcast_to(li, (CHUNK, d)) for li in last], axis=0)
                kd = (k[g] * jnp.exp(g_last - gcb[g])).astype(BF16)
                qp = (q[g] * eg[g] - qx[g][:, :d]).astype(BF16)
                for i in range(DN_GROUP):
                    c = groups[g] * DN_GROUP + i
                    cr = slice(i * CHUNK, (i + 1) * CHUNK)
                    kx = _dot_tn(kd[cr, :], xb[g][cr, :])
                    qm_s[hh, c, :CHUNK, :] = qp[cr, :]
                    qm_s[hh, c, CHUNK:, :] = kx[:, :d].astype(BF16)
                    n_s[hh, c] = kx[:, d:]
                    cd_s[hh, pl.ds(c, 1), :] = jnp.exp(last[i])
                op_s[hh, rows[g], :] = qx[g][:, d:]
            return carry

        lax.fori_loop(0, seq // (DN_ROWS * DN_GROUPS_PER_ITER), prep, 0)

    def emit(c, outs):
        r = pl.ds(pl.multiple_of(c * CHUNK, CHUNK), CHUNK)
        for hh in range(DN_PAIR):
            cols = slice(hh * d, (hh + 1) * d)
            o = outs[hh] + op_s[hh, r, :]
            o_ref[0, r, cols] = (_rms(o, nw) * _silu(z_ref[0, r, cols])).astype(o_ref.dtype)

    def step(c, carry):
        states, outs = carry
        prods = [jnp.dot(qm_s[hh, c], states[hh].astype(BF16), preferred_element_type=F32) for hh in range(DN_PAIR)]
        emit(jnp.maximum(c - 1, 0), outs)
        new = tuple(cd_s[hh, pl.ds(c, 1), :] * states[hh] + n_s[hh, c] - prods[hh][CHUNK:, :]
                    for hh in range(DN_PAIR))
        return new, tuple(p[:CHUNK, :] for p in prods)

    zeros = lambda rows: tuple(jnp.zeros((rows, d), F32) for _ in range(DN_PAIR))
    _, outs = lax.fori_loop(0, n_chunks, step, (zeros(d), zeros(CHUNK)))
    emit(n_chunks - 1, outs)


def _deltanet(main, small, conv_w, a_log, dt_bias, norm_w):
    bsz, seq, _ = main.shape
    n_chunks = seq // CHUNK
    d = DN_HEAD_DIM
    pw = DN_PAIR * d
    n_pairs = DN_HEADS // DN_PAIR
    n_groups = seq // DN_ROWS
    assert seq % (DN_ROWS * DN_GROUPS_PER_ITER) == 0 and DN_ROWS % d == 0
    ab_t = small[:, :, :2 * DN_HEADS].transpose(0, 2, 1).reshape(bsz, 2 * DN_HEADS, n_groups, DN_ROWS)
    cw = conv_w.reshape(DN_CONV, 3 * DN_WIDTH)
    cum = jnp.kron(jnp.eye(DN_GROUP, dtype=F32), jnp.triu(jnp.ones((CHUNK, CHUNK), F32)))
    sel = jnp.kron((jnp.arange(SEL_ROWS)[:, None] // N_SPLIT == jnp.arange(2)[None, :]).astype(BF16),
                   jnp.ones((1, d), BF16))
    col = lambda off: pl.BlockSpec((1, seq, pw), lambda b, h: (b, 0, off + h))
    cwspec = lambda off: pl.BlockSpec((DN_CONV, pw), lambda b, h: (0, off + h))
    tspec = lambda off: pl.BlockSpec((1, DN_PAIR, n_groups, DN_ROWS), lambda b, h: (b, off + h, 0, 0))
    one = pl.BlockSpec((DN_PAIR, 1, 1), lambda b, h: (h, 0, 0))
    return pl.pallas_call(
        _deltanet_kernel,
        grid=(bsz, n_pairs),
        in_specs=[col(0), col(n_pairs), col(2 * n_pairs), col(3 * n_pairs),
                  tspec(0), tspec(n_pairs),
                  cwspec(0), cwspec(n_pairs), cwspec(2 * n_pairs),
                  one, one,
                  pl.BlockSpec((1, d), lambda b, h: (0, 0)),
                  pl.BlockSpec((DN_ROWS, DN_ROWS), lambda b, h: (0, 0)),
                  pl.BlockSpec((SEL_ROWS, 2 * d), lambda b, h: (0, 0))],
        out_specs=pl.BlockSpec((1, seq, pw), lambda b, h: (b, 0, h)),
        out_shape=jax.ShapeDtypeStruct((bsz, seq, DN_WIDTH), BF16),
        scratch_shapes=[pltpu.VMEM((seq + SUBLANES, d), F32)] + [pltpu.VMEM((seq, d), F32)] * 3 + [
                        pltpu.VMEM((n_groups, DN_ROWS), F32),
                        pltpu.VMEM((2 * N_SPLIT, n_groups, DN_ROWS), F32),
                        pltpu.VMEM((DN_PAIR, n_chunks, CHUNK + d, d), BF16),
                        pltpu.VMEM((DN_PAIR, n_chunks, d, d), F32),
                        pltpu.VMEM((DN_PAIR, seq, d), F32),
                        pltpu.VMEM((DN_PAIR, n_chunks, d), F32)],
        compiler_params=_params(("parallel", "arbitrary")),
        name="deltanet",
    )(main, main, main, main, ab_t, ab_t, cw, cw, cw,
      a_log.reshape(DN_HEADS, 1, 1), dt_bias.reshape(DN_HEADS, 1, 1), norm_w.reshape(1, d), cum, sel)


def _gelu_tanh(x):
    return x * (0.5 * (1.0 + jnp.tanh(0.7978845608028654 * (x + 0.044715 * (x * x * x)))))


def _s5_kernel(u_ref, are_ref, aim_ref, ls_ref, bre_ref, bim_ref, cre_ref, cim_ref, d_ref, o_ref,
               ut, xs, xb, yt, state, bbd, cbd, lam):
    ns = S5_TILE_STATE
    bsz, ts, _ = u_ref.shape

    @pl.when(pl.program_id(1) == 0)
    def _():
        a_re = are_ref[0]
        a_im = aim_ref[0]
        dt = jnp.exp(ls_ref[0])
        mag = jnp.exp(a_re * dt)
        l_re = mag * jnp.cos(a_im * dt)
        l_im = mag * jnp.sin(a_im * dt)
        den = a_re * a_re + a_im * a_im
        c_re = ((l_re - 1.0) * a_re + l_im * a_im) / den
        c_im = (l_im * a_re - (l_re - 1.0) * a_im) / den
        b_re = bre_ref[0]
        b_im = bim_ref[0]
        bbd[:, :ns] = (c_re * b_re - c_im * b_im).astype(BF16)
        bbd[:, ns:] = (c_re * b_im + c_im * b_re).astype(BF16)
        cbd[:ns, :] = cre_ref[0].astype(BF16)
        cbd[ns:, :] = (-cim_ref[0]).astype(BF16)
        lam[:, :ns] = jnp.broadcast_to(l_re, (bsz, ns))
        lam[:, ns:] = jnp.broadcast_to(l_im, (bsz, ns))
        state[...] = jnp.zeros_like(state)

    for b in range(bsz):
        ut[pl.ds(b, ts, stride=bsz), :] = u_ref[b]
    xs[...] = jnp.dot(ut[...].astype(BF16), bbd[...], preferred_element_type=F32)
    l_re = lam[:, :ns]
    l_im = lam[:, ns:]

    frames = S5_SCAN_ROWS // bsz

    def step(i, carry):
        x_re, x_im = carry
        r = pl.ds(pl.multiple_of(i * S5_SCAN_ROWS, S5_SCAN_ROWS), S5_SCAN_ROWS)
        bu = xs[r, :]
        res_re, res_im = [], []
        for f in range(frames):
            rows = slice(f * bsz, (f + 1) * bsz)
            x_re, x_im = (l_re * x_re - l_im * x_im + bu[rows, :ns], l_re * x_im + l_im * x_re + bu[rows, ns:])
            res_re.append(x_re)
            res_im.append(x_im)
        xb[r, :ns] = jnp.concatenate(res_re, axis=0).astype(BF16)
        xb[r, ns:] = jnp.concatenate(res_im, axis=0).astype(BF16)
        return x_re, x_im

    x_re, x_im = lax.fori_loop(0, ts // frames, step, (state[:, :ns], state[:, ns:]))
    state[:, :ns] = x_re
    state[:, ns:] = x_im
    y = jnp.dot(xb[...], cbd[...], preferred_element_type=F32) + d_ref[0] * ut[...]
    yt[...] = _gelu_tanh(y)
    for b in range(bsz):
        o_ref[b] = yt[pl.ds(b, ts, stride=bsz), :]


def _s5(main, u_col_block, a_re, a_im, b_re, b_im, c_re, c_im, d, log_step, *, ts=256):
    bsz, seq, _ = main.shape
    assert bsz == SUBLANES
    tg, ns, nt = S5_TILE_GROUPS, S5_TILE_STATE, S5_TILES
    eye = jnp.eye(tg, dtype=F32)

    def expand_b(b):
        bt = b.reshape(nt, tg, S5_STATE, S5_GROUP)
        return jnp.einsum('ngph,gk->nghkp', bt, eye).reshape(nt, LANES, ns)

    def expand_c(c):
        ct = c.reshape(nt, tg, S5_GROUP, S5_STATE)
        return jnp.einsum('nghp,gk->ngpkh', ct, eye).reshape(nt, ns, LANES)

    chan = lambda a: a.reshape(nt, 1, ns)
    ls = jnp.broadcast_to(log_step[:, None], (S5_GROUPS, S5_STATE))
    pspec = pl.BlockSpec((1, 1, ns), lambda c, t: (c, 0, 0))
    bspec = pl.BlockSpec((1, LANES, ns), lambda c, t: (c, 0, 0))
    cspec = pl.BlockSpec((1, ns, LANES), lambda c, t: (c, 0, 0))
    return pl.pallas_call(
        _s5_kernel,
        grid=(nt, seq // ts),
        in_specs=[pl.BlockSpec((bsz, ts, LANES), lambda c, t: (0, t, u_col_block + c)),
                  pspec, pspec, pspec, bspec, bspec, cspec, cspec,
                  pl.BlockSpec((1, 1, LANES), lambda c, t: (c, 0, 0))],
        out_specs=pl.BlockSpec((bsz, ts, LANES), lambda c, t: (0, t, c)),
        out_shape=jax.ShapeDtypeStruct((bsz, seq, S5_WIDTH), F32),
        scratch_shapes=[pltpu.VMEM((bsz * ts, LANES), F32),
                        pltpu.VMEM((bsz * ts, 2 * ns), F32),
                        pltpu.VMEM((bsz * ts, 2 * ns), BF16),
                        pltpu.VMEM((bsz * ts, LANES), F32),
                        pltpu.VMEM((bsz, 2 * ns), F32),
                        pltpu.VMEM((LANES, 2 * ns), BF16),
                        pltpu.VMEM((2 * ns, LANES), BF16),
                        pltpu.VMEM((bsz, 2 * ns), F32)],
        compiler_params=_params(("parallel", "arbitrary")),
        name="s5",
    )(main, chan(a_re), chan(a_im), chan(ls), expand_b(b_re), expand_b(b_im), expand_c(c_re), expand_c(c_im),
      d.reshape(nt, 1, LANES))


def _glu_kernel(y_ref, w_ref, b_ref, o_ref):
    y = y_ref[...]
    gate = jnp.dot(y.astype(BF16), w_ref[...], preferred_element_type=F32) + b_ref[...]
    o_ref[...] = (y * jax.nn.sigmoid(gate)).astype(o_ref.dtype)


def _glu(y2, w, b, *, tm=512):
    t, n = y2.shape
    return pl.pallas_call(
        _glu_kernel,
        grid=(t // tm,),
        in_specs=[pl.BlockSpec((tm, n), lambda i: (i, 0)),
                  pl.BlockSpec((n, n), lambda i: (0, 0)),
                  pl.BlockSpec((1, n), lambda i: (0, 0))],
        out_specs=pl.BlockSpec((tm, n), lambda i: (i, 0)),
        out_shape=jax.ShapeDtypeStruct((t, n), BF16),
        compiler_params=_params(("parallel",)),
        name="s5_glu",
    )(y2, w, b.reshape(1, n))


def _gla_scan(q_at, k_at, v_at, bs, gate_ref, nw_ref, o_ref, st_s):
    n_chunks, dv, dk = st_s.shape
    incl, _ = _tri_masks()
    nw = nw_ref[...]

    def chunk_rows(it):
        cs = [it * GLA_GROUP + i for i in range(GLA_GROUP)]
        return cs, [pl.ds(pl.multiple_of(c * CHUNK, CHUNK), CHUNK) for c in cs]

    def states(it, state_t):
        cs, rows = chunk_rows(it)
        b = [bs[r, :] for r in rows]
        b_last = [bi[CHUNK - 1:CHUNK, :] for bi in b]
        inc = [_dot_tn(v_at(r), k_at(r) * jnp.exp(bl - bi)) for r, bl, bi in zip(rows, b_last, b)]
        for c, bl, ic in zip(cs, b_last, inc):
            st_s[c] = state_t.astype(BF16)
            state_t = jnp.exp(bl) * state_t + ic
        return state_t

    lax.fori_loop(0, n_chunks // GLA_GROUP, states, jnp.zeros((dv, dk), F32))

    def outputs(it, carry):
        cs, rows = chunk_rows(it)
        q = [q_at(r) for r in rows]
        b = [bs[r, :] for r in rows]
        b_mid = [bi[CHUNK // 2 - 1:CHUNK // 2, :] for bi in b]
        att = [jnp.where(incl, _dot_nt(qi * jnp.exp(bi - bm), k_at(r) * jnp.exp(bm - bi)), 0.0)
               for qi, bi, bm, r in zip(q, b, b_mid, rows)]
        o_inter = [_dot_nt(qi * jnp.exp(bi), st_s[c]) for qi, bi, c in zip(q, b, cs)]
        o = [_dot(ai, v_at(r)) + oi for ai, r, oi in zip(att, rows, o_inter)]
        for r, oi in zip(rows, o):
            o_ref[0, r, :] = (_rms(oi, nw) * _silu(gate_ref[0, r, :])).astype(o_ref.dtype)
        return carry

    lax.fori_loop(0, n_chunks // GLA_GROUP, outputs, 0)


def _gla_kernel(q_ref, k_ref, v_ref, r_ref, lr_ref, w2_ref, gb_ref, nw_ref, o_ref, bs, st_s):
    gate = _dot(lr_ref[0], w2_ref[...]) + gb_ref[...]
    log_a = -_softplus(-gate) / GLA_TAU
    bs[...] = _chunk_cumsum(log_a)
    _gla_scan(lambda r: q_ref[0, r, :] * GLA_HEAD_K ** -0.5, lambda r: k_ref[0, r, :], lambda r: v_ref[0, r, :],
              bs, r_ref, nw_ref, o_ref, st_s)


def _hgrn2_kernel(q_ref, f_ref, i_ref, g_ref, lbl_ref, nw_ref, o_ref, ks, bs, st_s, *, layer):
    logits = lbl_ref[...]
    e = jnp.exp(logits - jnp.max(logits, axis=0, keepdims=True))
    p = e / jnp.sum(e, axis=0, keepdims=True)
    lb_first = p[0:1, :]
    lb_layer = lb_first
    for i in range(1, layer + 1):
        lb_layer = lb_layer + p[i:i + 1, :]
    lb = lb_layer - lb_first
    z_f = f_ref[0]
    e_z = jnp.exp(-jnp.abs(z_f))
    big = 1.0 / (1.0 + e_z)
    small = e_z * big
    sig_pos = jnp.where(z_f >= 0.0, big, small)
    sig_neg = jnp.where(z_f >= 0.0, small, big)
    ks[...] = (1.0 - lb) * sig_neg
    bs[...] = _chunk_cumsum(jnp.log(lb + (1.0 - lb) * sig_pos))
    _gla_scan(lambda r: q_ref[0, r, :], lambda r: ks[r, :], lambda r: i_ref[0, r, :],
              bs, g_ref, nw_ref, o_ref, st_s)


def _gla(main, small, w2, gate_b, norm_w):
    bsz, seq, _ = main.shape
    assert seq % (CHUNK * GLA_GROUP) == 0
    dk, dv = GLA_HEAD_K, GLA_HEAD_V
    w2p = jnp.concatenate([w2, jnp.zeros((LANES - GLA_GATE_RANK, GLA_KEY_WIDTH), F32)], axis=0).astype(BF16)
    kspec = lambda off: pl.BlockSpec((1, seq, dk), lambda b, h: (b, 0, off + h))
    vspec = lambda off: pl.BlockSpec((1, seq, dv), lambda b, h: (b, 0, off + h))
    return pl.pallas_call(
        _gla_kernel,
        grid=(bsz, GLA_HEADS),
        in_specs=[kspec(0), kspec(GLA_HEADS), vspec(GLA_HEADS), vspec(2 * GLA_HEADS),
                  pl.BlockSpec((1, seq, LANES), lambda b, h: (b, 0, 0)),
                  pl.BlockSpec((LANES, dk), lambda b, h: (0, h)),
                  pl.BlockSpec((1, dk), lambda b, h: (0, h)),
                  pl.BlockSpec((1, dv), lambda b, h: (0, 0))],
        out_specs=pl.BlockSpec((1, seq, dv), lambda b, h: (b, 0, h)),
        out_shape=jax.ShapeDtypeStruct((bsz, seq, GLA_WIDTH), BF16),
        scratch_shapes=[pltpu.VMEM((seq, dk), F32), pltpu.VMEM((seq // CHUNK, dv, dk), BF16)],
        compiler_params=_params(("parallel", "arbitrary")),
        name="gla",
    )(main, main, main, main, small, w2p, gate_b.reshape(1, GLA_KEY_WIDTH), norm_w.reshape(1, dv))


def _hgrn2(main, lb_logits, layer, norm_w):
    bsz, seq, _ = main.shape
    assert seq % (CHUNK * GLA_GROUP) == 0
    d = HG_EXPAND
    base = (2 * GLA_KEY_WIDTH + 2 * GLA_WIDTH) // d
    col = lambda off: pl.BlockSpec((1, seq, d), lambda b, h: (b, 0, base + off + h))
    depth = lb_logits.shape[0]
    return pl.pallas_call(
        functools.partial(_hgrn2_kernel, layer=layer),
        grid=(bsz, HG_HEADS),
        in_specs=[col(0), col(HG_HEADS), col(2 * HG_HEADS), col(3 * HG_HEADS),
                  pl.BlockSpec((depth, d), lambda b, h: (0, h)),
                  pl.BlockSpec((1, d), lambda b, h: (0, 0))],
        out_specs=pl.BlockSpec((1, seq, d), lambda b, h: (b, 0, h)),
        out_shape=jax.ShapeDtypeStruct((bsz, seq, HG_WIDTH), BF16),
        scratch_shapes=[pltpu.VMEM((seq, d), F32)] * 2 + [pltpu.VMEM((seq // CHUNK, d, d), BF16)],
        compiler_params=_params(("parallel", "arbitrary")),
        name="hgrn2",
    )(main, main, main, main, lb_logits, norm_w.reshape(1, d))


def _pad_cols(w):
    return jnp.concatenate([w, jnp.zeros((w.shape[0], LANES - w.shape[1]), w.dtype)], axis=1)


def _even_mixer(x2, bsz, seq, mix_norm, w_in, conv_w, a_log, dt_bias, dn_norm_w, s5_a_re, s5_a_im, s5_b_re, s5_b_im,
                s5_c_re, s5_c_im, s5_d, s5_log_step, s5_glu_w, s5_glu_b, w_out):
    n_qkvz = 4 * DN_WIDTH
    n_ab = 2 * DN_HEADS
    w_in = w_in.astype(BF16)
    w_main = jnp.concatenate([w_in[:, :n_qkvz], w_in[:, n_qkvz + n_ab:]], axis=1)
    w_small = _pad_cols(w_in[:, n_qkvz:n_qkvz + n_ab])
    main, small = _inproj(x2, mix_norm, w_main, w_small)
    main = main.reshape(bsz, seq, -1)
    small = small.reshape(bsz, seq, LANES)
    y_a = _deltanet(main, small, conv_w, a_log, dt_bias, dn_norm_w)
    y_s5 = _s5(main, n_qkvz // LANES, s5_a_re, s5_a_im, s5_b_re, s5_b_im, s5_c_re, s5_c_im, s5_d, s5_log_step)
    y_b = _glu(y_s5.reshape(bsz * seq, S5_WIDTH), s5_glu_w.astype(BF16), s5_glu_b)
    return _outproj(x2, y_a.reshape(bsz * seq, DN_WIDTH), y_b, w_out.astype(BF16))


def _odd_mixer(x2, bsz, seq, layer, lb_logits, mix_norm, w_in, gate_w2, gate_b, gla_norm_w, hg_norm_w, w_out):
    n_c = 2 * GLA_KEY_WIDTH + 2 * GLA_WIDTH
    w_in = w_in.astype(BF16)
    w_main = jnp.concatenate([w_in[:, :n_c], w_in[:, n_c + GLA_GATE_RANK:]], axis=1)
    w_small = _pad_cols(w_in[:, n_c:n_c + GLA_GATE_RANK])
    main, small = _inproj(x2, mix_norm, w_main, w_small)
    main = main.reshape(bsz, seq, -1)
    small = small.reshape(bsz, seq, LANES)
    y_c = _gla(main, small, gate_w2, gate_b, gla_norm_w)
    y_d = _hgrn2(main, lb_logits, layer, hg_norm_w)
    return _outproj(x2, y_c.reshape(bsz * seq, GLA_WIDTH), y_d.reshape(bsz * seq, HG_WIDTH), w_out.astype(BF16))


def kernel(x, l0_ffn1_norm, l0_ffn1_w_gate, l0_ffn1_w_up, l0_ffn1_w_down, l0_mix_norm, l0_w_in, l0_dn_conv_w, l0_dn_a_log, l0_dn_dt_bias, l0_dn_norm_w, l0_s5_a_re, l0_s5_a_im, l0_s5_b_re, l0_s5_b_im, l0_s5_c_re, l0_s5_c_im, l0_s5_d, l0_s5_log_step, l0_s5_glu_w, l0_s5_glu_b, l0_w_out, l0_ffn2_norm, l0_ffn2_w_gate, l0_ffn2_w_up, l0_ffn2_w_down, l1_ffn1_norm, l1_ffn1_w_gate, l1_ffn1_w_up, l1_ffn1_w_down, l1_mix_norm, l1_w_in, l1_gla_gate_w2, l1_gla_gate_b, l1_gla_norm_w, l1_hg_norm_w, l1_w_out, l1_ffn2_norm, l1_ffn2_w_gate, l1_ffn2_w_up, l1_ffn2_w_down, hgrn_lb_logits, final_norm):
    bsz, seq, d = x.shape
    x2 = x.reshape(bsz * seq, d)

    w_l0f1 = tuple(w.astype(BF16) for w in (l0_ffn1_w_gate, l0_ffn1_w_up, l0_ffn1_w_down))
    x2, w_l0f2 = _ffn(x2, l0_ffn1_norm, *w_l0f1, final_norm, final=False,
                      cast_next=(l0_ffn2_w_gate, l0_ffn2_w_up, l0_ffn2_w_down))
    x2 = _even_mixer(x2, bsz, seq, l0_mix_norm, l0_w_in, l0_dn_conv_w, l0_dn_a_log, l0_dn_dt_bias, l0_dn_norm_w,
                     l0_s5_a_re, l0_s5_a_im, l0_s5_b_re, l0_s5_b_im, l0_s5_c_re, l0_s5_c_im, l0_s5_d, l0_s5_log_step,
                     l0_s5_glu_w, l0_s5_glu_b, l0_w_out)
    x2, w_l1f1 = _ffn(x2, l0_ffn2_norm, *w_l0f2, final_norm, final=False,
                      cast_next=(l1_ffn1_w_gate, l1_ffn1_w_up, l1_ffn1_w_down))
    x2, w_l1f2 = _ffn(x2, l1_ffn1_norm, *w_l1f1, final_norm, final=False,
                      cast_next=(l1_ffn2_w_gate, l1_ffn2_w_up, l1_ffn2_w_down))
    x2 = _odd_mixer(x2, bsz, seq, 1, hgrn_lb_logits, l1_mix_norm, l1_w_in, l1_gla_gate_w2, l1_gla_gate_b,
                    l1_gla_norm_w, l1_hg_norm_w, l1_w_out)
    x2, _ = _ffn(x2, l1_ffn2_norm, *w_l1f2, final_norm, final=True)
    return x2.reshape(bsz, seq, d)
```

```python
import functools

import jax
import jax.numpy as jnp
from jax import lax
from jax.experimental import pallas as pl
from jax.experimental.pallas import tpu as pltpu

F32 = jnp.float32
BF16 = jnp.bfloat16

D_MODEL = 2048
CHUNK = 64
NORM_EPS = 1e-6
FFN_RES = 0.5
DN_HEADS = 8
DN_HEAD_DIM = 128
DN_WIDTH = DN_HEADS * DN_HEAD_DIM
DN_CONV = 4
S5_WIDTH = D_MODEL - DN_WIDTH
S5_GROUP = 16
S5_GROUPS = S5_WIDTH // S5_GROUP
S5_STATE = 64
GLA_HEADS = 4
GLA_WIDTH = D_MODEL // 2
GLA_KEY_WIDTH = GLA_WIDTH // 2
GLA_HEAD_K = GLA_KEY_WIDTH // GLA_HEADS
GLA_HEAD_V = GLA_WIDTH // GLA_HEADS
GLA_GATE_RANK = 16
GLA_TAU = 16.0
HG_WIDTH = D_MODEL - GLA_WIDTH
HG_EXPAND = 128
HG_HEADS = HG_WIDTH // HG_EXPAND

LANES = 128
SUBLANES = 8
BF16_ROWS = 16
VMEM_LIMIT = 56 * 1024 * 1024
FFN_VMEM_LIMIT = 58 * 1024 * 1024

S5_TILE_GROUPS = LANES // S5_GROUP
S5_TILE_STATE = S5_TILE_GROUPS * S5_STATE
S5_TILES = S5_WIDTH // LANES
S5_SCAN_ROWS = 4 * BF16_ROWS

DN_PAIR = 2
DN_GROUP = 2
DN_ROWS = DN_GROUP * CHUNK
DN_GROUPS_PER_ITER = 16
N_SPLIT = 3
SEL_ROWS = BF16_ROWS
GLA_GROUP = 32


def _params(sem, vmem_limit=VMEM_LIMIT):
    return pltpu.CompilerParams(dimension_semantics=sem, vmem_limit_bytes=vmem_limit)


def _rms(x, w):
    return x * lax.rsqrt(jnp.mean(x * x, axis=-1, keepdims=True) + NORM_EPS) * w


def _silu(x):
    return x * jax.nn.sigmoid(x)


def _softplus(x):
    return jnp.maximum(x, 0.0) + jnp.log1p(jnp.exp(-jnp.abs(x)))


def _dot(a, b):
    return jnp.dot(a.astype(BF16), b.astype(BF16), preferred_element_type=F32)


def _dot_nt(a, b):
    return lax.dot_general(a.astype(BF16), b.astype(BF16), (((1,), (1,)), ((), ())), preferred_element_type=F32)


def _dot_tn(a, b):
    return lax.dot_general(a.astype(BF16), b.astype(BF16), (((0,), (0,)), ((), ())), preferred_element_type=F32)


def _chunk_cumsum(x):
    pos = lax.broadcasted_iota(jnp.int32, x.shape, 0) % CHUNK
    shift = 1
    while shift < CHUNK:
        x = x + jnp.where(pos >= shift, pltpu.roll(x, shift, axis=0), 0.0)
        shift *= 2
    return x


def _tri_masks(n=CHUNK):
    row = lax.broadcasted_iota(jnp.int32, (n, n), 0)
    col = lax.broadcasted_iota(jnp.int32, (n, n), 1)
    same = (row // CHUNK) == (col // CHUNK)
    return same & (row >= col), same & (row > col)


def _ffn_kernel(x_ref, nw_ref, wg_ref, wu_ref, wd_ref, fw_ref, *rest, n_ff_tiles, final, n_cast):
    cast_in = rest[:n_cast]
    o_ref = rest[n_cast]
    cast_out = rest[n_cast + 1:2 * n_cast + 1]
    xn_ref = rest[2 * n_cast + 1]
    j = pl.program_id(1)
    for src, dst in zip(cast_in, cast_out):
        dst[...] = src[...].astype(BF16)

    @pl.when(j == 0)
    def _():
        x = x_ref[...]
        xn_ref[...] = _rms(x, nw_ref[...]).astype(BF16)
        o_ref[...] = x

    xn = xn_ref[...]
    g = jnp.dot(xn, wg_ref[...], preferred_element_type=F32)
    u = jnp.dot(xn, wu_ref[...], preferred_element_type=F32)
    h = (FFN_RES * (_silu(g) * u)).astype(BF16)
    o_ref[...] += jnp.dot(h, wd_ref[...], preferred_element_type=F32)

    if final:
        @pl.when(j == n_ff_tiles - 1)
        def _():
            o_ref[...] = _rms(o_ref[...], fw_ref[...])


def _grid_tiling(rows, cols, n_i, n_j):
    for (r, c, index_map) in (((rows // n_i, cols // n_j, lambda i, j: (i, j))),
                              ((rows // n_j, cols // n_i, lambda i, j: (j, i)))):
        if r % BF16_ROWS == 0 and c % LANES == 0 and r * c * n_i * n_j == rows * cols:
            return pl.BlockSpec((r, c), index_map)
    raise ValueError("no tile-aligned one-tile-per-step cover")


def _ffn(x2, nw, wg, wu, wd, fw, *, final, cast_next=None, tm=1024, tf=512):
    t, d = x2.shape
    f = wg.shape[1]
    n_row, n_ff = t // tm, f // tf
    in_specs = [pl.BlockSpec((tm, d), lambda i, j: (i, 0)),
                pl.BlockSpec((1, d), lambda i, j: (0, 0)),
                pl.BlockSpec((d, tf), lambda i, j: (0, j)),
                pl.BlockSpec((d, tf), lambda i, j: (0, j)),
                pl.BlockSpec((tf, d), lambda i, j: (j, 0)),
                pl.BlockSpec((1, d), lambda i, j: (0, 0))]
    out_specs = [pl.BlockSpec((tm, d), lambda i, j: (i, 0))]
    out_shape = [jax.ShapeDtypeStruct((t, d), F32)]
    operands = [x2, nw.reshape(1, d), wg, wu, wd, fw.reshape(1, d)]
    if cast_next is not None:
        assert all(w.shape == s for w, s in zip(cast_next, ((d, f), (d, f), (f, d))))
        up_spec = _grid_tiling(d, f, n_row, n_ff)
        down_spec = _grid_tiling(f, d, n_row, n_ff)
        in_specs += [up_spec, up_spec, down_spec]
        out_specs += [up_spec, up_spec, down_spec]
        out_shape += [jax.ShapeDtypeStruct(w.shape, BF16) for w in cast_next]
        operands += list(cast_next)
    res = pl.pallas_call(
        functools.partial(_ffn_kernel, n_ff_tiles=n_ff, final=final, n_cast=0 if cast_next is None else 3),
        grid=(n_row, n_ff),
        in_specs=in_specs,
        out_specs=out_specs,
        out_shape=out_shape,
        scratch_shapes=[pltpu.VMEM((tm, d), BF16)],
        compiler_params=_params(("parallel", "arbitrary"), FFN_VMEM_LIMIT),
        name="ffn",
    )(*operands)
    return res[0], tuple(res[1:])


def _inproj_kernel(x_ref, nw_ref, w_ref, ws_ref, o_ref, os_ref, xn_ref):
    @pl.when(pl.program_id(1) == 0)
    def _():
        xn = _rms(x_ref[...], nw_ref[...]).astype(BF16)
        xn_ref[...] = xn
        os_ref[...] = jnp.dot(xn, ws_ref[...], preferred_element_type=F32)

    o_ref[...] = jnp.dot(xn_ref[...], w_ref[...], preferred_element_type=F32)


def _inproj(x2, nw, w_main, w_small, *, tm=1024, tn=1024):
    t, d = x2.shape
    n = w_main.shape[1]
    return pl.pallas_call(
        _inproj_kernel,
        grid=(t // tm, n // tn),
        in_specs=[pl.BlockSpec((tm, d), lambda i, j: (i, 0)),
                  pl.BlockSpec((1, d), lambda i, j: (0, 0)),
                  pl.BlockSpec((d, tn), lambda i, j: (0, j)),
                  pl.BlockSpec((d, LANES), lambda i, j: (0, 0))],
        out_specs=[pl.BlockSpec((tm, tn), lambda i, j: (i, j)),
                   pl.BlockSpec((tm, LANES), lambda i, j: (i, 0))],
        out_shape=[jax.ShapeDtypeStruct((t, n), F32), jax.ShapeDtypeStruct((t, LANES), F32)],
        scratch_shapes=[pltpu.VMEM((tm, d), BF16)],
        compiler_params=_params(("parallel", "arbitrary")),
        name="inproj",
    )(x2, nw.reshape(1, d), w_main, w_small)


def _outproj_kernel(x_ref, ya_ref, yb_ref, wa_ref, wb_ref, o_ref):
    o_ref[...] = (x_ref[...] + jnp.dot(ya_ref[...], wa_ref[...], preferred_element_type=F32)
                  + jnp.dot(yb_ref[...], wb_ref[...], preferred_element_type=F32))


def _outproj(x2, ya, yb, w, *, tm=512):
    t, d = x2.shape
    ka = ya.shape[1]
    kb = yb.shape[1]
    assert ka == kb
    return pl.pallas_call(
        _outproj_kernel,
        grid=(t // tm,),
        in_specs=[pl.BlockSpec((tm, d), lambda i: (i, 0)),
                  pl.BlockSpec((tm, ka), lambda i: (i, 0)),
                  pl.BlockSpec((tm, kb), lambda i: (i, 0)),
                  pl.BlockSpec((ka, d), lambda i: (0, 0)),
                  pl.BlockSpec((kb, d), lambda i: (1, 0))],
        out_specs=pl.BlockSpec((tm, d), lambda i: (i, 0)),
        out_shape=jax.ShapeDtypeStruct((t, d), F32),
        compiler_params=_params(("parallel",)),
        name="outproj",
    )(x2, ya, yb, w, w)


def _causal_conv(x, w, xpad):
    seq = x.shape[0]
    xpad[SUBLANES:, :] = x
    y = x * w[DN_CONV - 1:DN_CONV, :]
    for back in range(1, DN_CONV):
        y = y + xpad[pl.ds(SUBLANES - back, seq), :] * w[DN_CONV - 1 - back:DN_CONV - back, :]
    return y


def _split3(x):
    parts = []
    for _ in range(N_SPLIT):
        p = x.astype(BF16).astype(F32)
        parts.append(p)
        x = x - p
    return parts


def _l2norm(x):
    return x * lax.rsqrt(jnp.sum(x * x, axis=-1, keepdims=True) + NORM_EPS)


def _deltanet_kernel(q_ref, k_ref, v_ref, z_ref, at_ref, bt_ref, cwq_ref, cwk_ref, cwv_ref,
                     alog1_ref, dtb1_ref, nw_ref, cum_ref, sel_ref, o_ref,
                     xpad, qs, ks, vs, gts, parts_s, qm_s, n_s, op_s, cd_s):
    seq = q_ref.shape[1]
    n_chunks = seq // CHUNK
    d = DN_HEAD_DIM
    incl, strict = _tri_masks(DN_ROWS)
    nw = nw_ref[...]
    sel = sel_ref[...]
    xpad[:SUBLANES, :] = jnp.zeros((SUBLANES, d), F32)

    for hh in range(DN_PAIR):
        cols = slice(hh * d, (hh + 1) * d)
        qs[...] = _l2norm(_silu(_causal_conv(q_ref[0, :, cols], cwq_ref[:, cols], xpad))) * DN_HEAD_DIM ** -0.5
        ks[...] = _l2norm(_silu(_causal_conv(k_ref[0, :, cols], cwk_ref[:, cols], xpad)))
        vs[...] = _silu(_causal_conv(v_ref[0, :, cols], cwv_ref[:, cols], xpad))
        g_t = -jnp.exp(alog1_ref[hh]) * _softplus(at_ref[0, hh] + dtb1_ref[hh])
        gc_t = jnp.dot(g_t, cum_ref[...], preferred_element_type=F32, precision=lax.Precision.HIGHEST)
        gts[...] = gc_t
        for i, part in enumerate(_split3(gc_t) + _split3(jax.nn.sigmoid(bt_ref[0, hh]))):
            parts_s[i] = part

        def prep(it, carry):
            groups = [it * DN_GROUPS_PER_ITER + g for g in range(DN_GROUPS_PER_ITER)]
            rows = [pl.ds(pl.multiple_of(gi * DN_ROWS, DN_ROWS), DN_ROWS) for gi in groups]
            q = [qs[r, :] for r in rows]
            k = [ks[r, :] for r in rows]
            src = [jnp.concatenate([parts_s[i, pl.ds(gi, 1), :] for i in range(2 * N_SPLIT)]
                                   + [jnp.zeros((SEL_ROWS - 2 * N_SPLIT, DN_ROWS), F32)], axis=0) for gi in groups]
            colb = [_dot_tn(s, sel) for s in src]
            gcb = [c[:, :d] for c in colb]
            beta = [c[:, d:] for c in colb]
            decay = [jnp.exp(jnp.where(incl, jnp.concatenate([gc] * (DN_ROWS // d), axis=1) - gts[pl.ds(gi, 1), :],
                                       -jnp.inf)) for gc, gi in zip(gcb, groups)]
            kb = [ki * bi for ki, bi in zip(k, beta)]
            a = [jnp.where(strict, _dot_nt(kbi, ki) * dc, 0.0) for kbi, ki, dc in zip(kb, k, decay)]
            eg = [jnp.exp(gc) for gc in gcb]
            x = [jnp.concatenate([kbi * egi, vs[r, :] * bi], axis=1) for kbi, egi, r, bi in zip(kb, eg, rows, beta)]
            pb = [ai.astype(BF16) for ai in a]
            x = [xi - _dot(pi, xi) for pi, xi in zip(pb, x)]
            for _ in range(CHUNK.bit_length() - 2):
                pb = [jnp.dot(pi, pi, preferred_element_type=F32).astype(BF16) for pi in pb]
                x = [xi + _dot(pi, xi) for pi, xi in zip(pb, x)]
            xb = [xi.astype(BF16) for xi in x]
            qk = [jnp.where(incl, _dot_nt(qi, ki) * dc, 0.0) for qi, ki, dc in zip(q, k, decay)]
            qx = [_dot(qki, xi) for qki, xi in zip(qk, xb)]
            for g in range(DN_GROUPS_PER_ITER):
                last = [gcb[g][(i + 1) * CHUNK - 1:(i + 1) * CHUNK, :] for i in range(DN_GROUP)]
                g_last = jnp.concatenate([jnp.broadcast_to(li, (CHUNK, d)) for li in last], axis=0)
                kd = (k[g] * jnp.exp(g_last - gcb[g])).astype(BF16)
                qp = (q[g] * eg[g] - qx[g][:, :d]).astype(BF16)
                for i in range(DN_GROUP):
                    c = groups[g] * DN_GROUP + i
                    cr = slice(i * CHUNK, (i + 1) * CHUNK)
                    kx = _dot_tn(kd[cr, :], xb[g][cr, :])
                    qm_s[hh, c, :CHUNK, :] = qp[cr, :]
                    qm_s[hh, c, CHUNK:, :] = kx[:, :d].astype(BF16)
                    n_s[hh, c] = kx[:, d:]
                    cd_s[hh, pl.ds(c, 1), :] = jnp.exp(last[i])
                op_s[hh, rows[g], :] = qx[g][:, d:]
            return carry

        lax.fori_loop(0, seq // (DN_ROWS * DN_GROUPS_PER_ITER), prep, 0)

    def emit(c, outs):
        r = pl.ds(pl.multiple_of(c * CHUNK, CHUNK), CHUNK)
        for hh in range(DN_PAIR):
            cols = slice(hh * d, (hh + 1) * d)
            o = outs[hh] + op_s[hh, r, :]
            o_ref[0, r, cols] = (_rms(o, nw) * _silu(z_ref[0, r, cols])).astype(o_ref.dtype)

    def step(c, carry):
        states, outs = carry
        prods = [jnp.dot(qm_s[hh, c], states[hh].astype(BF16), preferred_element_type=F32) for hh in range(DN_PAIR)]
        emit(jnp.maximum(c - 1, 0), outs)
        new = tuple(cd_s[hh, pl.ds(c, 1), :] * states[hh] + n_s[hh, c] - prods[hh][CHUNK:, :]
                    for hh in range(DN_PAIR))
        return new, tuple(p[:CHUNK, :] for p in prods)

    zeros = lambda rows: tuple(jnp.zeros((rows, d), F32) for _ in range(DN_PAIR))
    _, outs = lax.fori_loop(0, n_chunks, step, (zeros(d), zeros(CHUNK)))
    emit(n_chunks - 1, outs)


def _deltanet(main, small, conv_w, a_log, dt_bias, norm_w):
    bsz, seq, _ = main.shape
    n_chunks = seq // CHUNK
    d = DN_HEAD_DIM
    pw = DN_PAIR * d
    n_pairs = DN_HEADS // DN_PAIR
    n_groups = seq // DN_ROWS
    assert seq % (DN_ROWS * DN_GROUPS_PER_ITER) == 0 and DN_ROWS % d == 0
    ab_t = small[:, :, :2 * DN_HEADS].transpose(0, 2, 1).reshape(bsz, 2 * DN_HEADS, n_groups, DN_ROWS)
    cw = conv_w.reshape(DN_CONV, 3 * DN_WIDTH)
    cum = jnp.kron(jnp.eye(DN_GROUP, dtype=F32), jnp.triu(jnp.ones((CHUNK, CHUNK), F32)))
    sel = jnp.kron((jnp.arange(SEL_ROWS)[:, None] // N_SPLIT == jnp.arange(2)[None, :]).astype(BF16),
                   jnp.ones((1, d), BF16))
    col = lambda off: pl.BlockSpec((1, seq, pw), lambda b, h: (b, 0, off + h))
    cwspec = lambda off: pl.BlockSpec((DN_CONV, pw), lambda b, h: (0, off + h))
    tspec = lambda off: pl.BlockSpec((1, DN_PAIR, n_groups, DN_ROWS), lambda b, h: (b, off + h, 0, 0))
    one = pl.BlockSpec((DN_PAIR, 1, 1), lambda b, h: (h, 0, 0))
    return pl.pallas_call(
        _deltanet_kernel,
        grid=(bsz, n_pairs),
        in_specs=[col(0), col(n_pairs), col(2 * n_pairs), col(3 * n_pairs),
                  tspec(0), tspec(n_pairs),
                  cwspec(0), cwspec(n_pairs), cwspec(2 * n_pairs),
                  one, one,
                  pl.BlockSpec((1, d), lambda b, h: (0, 0)),
                  pl.BlockSpec((DN_ROWS, DN_ROWS), lambda b, h: (0, 0)),
                  pl.BlockSpec((SEL_ROWS, 2 * d), lambda b, h: (0, 0))],
        out_specs=pl.BlockSpec((1, seq, pw), lambda b, h: (b, 0, h)),
        out_shape=jax.ShapeDtypeStruct((bsz, seq, DN_WIDTH), BF16),
        scratch_shapes=[pltpu.VMEM((seq + SUBLANES, d), F32)] + [pltpu.VMEM((seq, d), F32)] * 3 + [
                        pltpu.VMEM((n_groups, DN_ROWS), F32),
                        pltpu.VMEM((2 * N_SPLIT, n_groups, DN_ROWS), F32),
                        pltpu.VMEM((DN_PAIR, n_chunks, CHUNK + d, d), BF16),
                        pltpu.VMEM((DN_PAIR, n_chunks, d, d), F32),
                        pltpu.VMEM((DN_PAIR, seq, d), F32),
                        pltpu.VMEM((DN_PAIR, n_chunks, d), F32)],
        compiler_params=_params(("parallel", "arbitrary")),
        name="deltanet",
    )(main, main, main, main, ab_t, ab_t, cw, cw, cw,
      a_log.reshape(DN_HEADS, 1, 1), dt_bias.reshape(DN_HEADS, 1, 1), norm_w.reshape(1, d), cum, sel)


def _gelu_tanh(x):
    return x * (0.5 * (1.0 + jnp.tanh(0.7978845608028654 * (x + 0.044715 * (x * x * x)))))


def _s5_kernel(u_ref, are_ref, aim_ref, ls_ref, bre_ref, bim_ref, cre_ref, cim_ref, d_ref, o_ref,
               ut, xs, xb, yt, state, bbd, cbd, lam):
    ns = S5_TILE_STATE
    bsz, ts, _ = u_ref.shape

    @pl.when(pl.program_id(1) == 0)
    def _():
        a_re = are_ref[0]
        a_im = aim_ref[0]
        dt = jnp.exp(ls_ref[0])
        mag = jnp.exp(a_re * dt)
        l_re = mag * jnp.cos(a_im * dt)
        l_im = mag * jnp.sin(a_im * dt)
        den = a_re * a_re + a_im * a_im
        c_re = ((l_re - 1.0) * a_re + l_im * a_im) / den
        c_im = (l_im * a_re - (l_re - 1.0) * a_im) / den
        b_re = bre_ref[0]
        b_im = bim_ref[0]
        bbd[:, :ns] = (c_re * b_re - c_im * b_im).astype(BF16)
        bbd[:, ns:] = (c_re * b_im + c_im * b_re).astype(BF16)
        cbd[:ns, :] = cre_ref[0].astype(BF16)
        cbd[ns:, :] = (-cim_ref[0]).astype(BF16)
        lam[:, :ns] = jnp.broadcast_to(l_re, (bsz, ns))
        lam[:, ns:] = jnp.broadcast_to(l_im, (bsz, ns))
        state[...] = jnp.zeros_like(state)

    for b in range(bsz):
        ut[pl.ds(b, ts, stride=bsz), :] = u_ref[b]
    xs[...] = jnp.dot(ut[...].astype(BF16), bbd[...], preferred_element_type=F32)
    l_re = lam[:, :ns]
    l_im = lam[:, ns:]

    frames = S5_SCAN_ROWS // bsz

    def step(i, carry):
        x_re, x_im = carry
        r = pl.ds(pl.multiple_of(i * S5_SCAN_ROWS, S5_SCAN_ROWS), S5_SCAN_ROWS)
        bu = xs[r, :]
        res_re, res_im = [], []
        for f in range(frames):
            rows = slice(f * bsz, (f + 1) * bsz)
            x_re, x_im = (l_re * x_re - l_im * x_im + bu[rows, :ns], l_re * x_im + l_im * x_re + bu[rows, ns:])
            res_re.append(x_re)
            res_im.append(x_im)
        xb[r, :ns] = jnp.concatenate(res_re, axis=0).astype(BF16)
        xb[r, ns:] = jnp.concatenate(res_im, axis=0).astype(BF16)
        return x_re, x_im

    x_re, x_im = lax.fori_loop(0, ts // frames, step, (state[:, :ns], state[:, ns:]))
    state[:, :ns] = x_re
    state[:, ns:] = x_im
    y = jnp.dot(xb[...], cbd[...], preferred_element_type=F32) + d_ref[0] * ut[...]
    yt[...] = _gelu_tanh(y)
    for b in range(bsz):
        o_ref[b] = yt[pl.ds(b, ts, stride=bsz), :]


def _s5(main, u_col_block, a_re, a_im, b_re, b_im, c_re, c_im, d, log_step, *, ts=256):
    bsz, seq, _ = main.shape
    assert bsz == SUBLANES
    tg, ns, nt = S5_TILE_GROUPS, S5_TILE_STATE, S5_TILES
    eye = jnp.eye(tg, dtype=F32)

    def expand_b(b):
        bt = b.reshape(nt, tg, S5_STATE, S5_GROUP)
        return jnp.einsum('ngph,gk->nghkp', bt, eye).reshape(nt, LANES, ns)

    def expand_c(c):
        ct = c.reshape(nt, tg, S5_GROUP, S5_STATE)
        return jnp.einsum('nghp,gk->ngpkh', ct, eye).reshape(nt, ns, LANES)

    chan = lambda a: a.reshape(nt, 1, ns)
    ls = jnp.broadcast_to(log_step[:, None], (S5_GROUPS, S5_STATE))
    pspec = pl.BlockSpec((1, 1, ns), lambda c, t: (c, 0, 0))
    bspec = pl.BlockSpec((1, LANES, ns), lambda c, t: (c, 0, 0))
    cspec = pl.BlockSpec((1, ns, LANES), lambda c, t: (c, 0, 0))
    return pl.pallas_call(
        _s5_kernel,
        grid=(nt, seq // ts),
        in_specs=[pl.BlockSpec((bsz, ts, LANES), lambda c, t: (0, t, u_col_block + c)),
                  pspec, pspec, pspec, bspec, bspec, cspec, cspec,
                  pl.BlockSpec((1, 1, LANES), lambda c, t: (c, 0, 0))],
        out_specs=pl.BlockSpec((bsz, ts, LANES), lambda c, t: (0, t, c)),
        out_shape=jax.ShapeDtypeStruct((bsz, seq, S5_WIDTH), F32),
        scratch_shapes=[pltpu.VMEM((bsz * ts, LANES), F32),
                        pltpu.VMEM((bsz * ts, 2 * ns), F32),
                        pltpu.VMEM((bsz * ts, 2 * ns), BF16),
                        pltpu.VMEM((bsz * ts, LANES), F32),
                        pltpu.VMEM((bsz, 2 * ns), F32),
                        pltpu.VMEM((LANES, 2 * ns), BF16),
                        pltpu.VMEM((2 * ns, LANES), BF16),
                        pltpu.VMEM((bsz, 2 * ns), F32)],
        compiler_params=_params(("parallel", "arbitrary")),
        name="s5",
    )(main, chan(a_re), chan(a_im), chan(ls), expand_b(b_re), expand_b(b_im), expand_c(c_re), expand_c(c_im),
      d.reshape(nt, 1, LANES))


def _glu_kernel(y_ref, w_ref, b_ref, o_ref):
    y = y_ref[...]
    gate = jnp.dot(y.astype(BF16), w_ref[...], preferred_element_type=F32) + b_ref[...]
    o_ref[...] = (y * jax.nn.sigmoid(gate)).astype(o_ref.dtype)


def _glu(y2, w, b, *, tm=512):
    t, n = y2.shape
    return pl.pallas_call(
        _glu_kernel,
        grid=(t // tm,),
        in_specs=[pl.BlockSpec((tm, n), lambda i: (i, 0)),
                  pl.BlockSpec((n, n), lambda i: (0, 0)),
                  pl.BlockSpec((1, n), lambda i: (0, 0))],
        out_specs=pl.BlockSpec((tm, n), lambda i: (i, 0)),
        out_shape=jax.ShapeDtypeStruct((t, n), BF16),
        compiler_params=_params(("parallel",)),
        name="s5_glu",
    )(y2, w, b.reshape(1, n))


def _gla_scan(q_at, k_at, v_at, bs, gate_ref, nw_ref, o_ref, st_s):
    n_chunks, dv, dk = st_s.shape
    incl, _ = _tri_masks()
    nw = nw_ref[...]

    def chunk_rows(it):
        cs = [it * GLA_GROUP + i for i in range(GLA_GROUP)]
        return cs, [pl.ds(pl.multiple_of(c * CHUNK, CHUNK), CHUNK) for c in cs]

    def states(it, state_t):
        cs, rows = chunk_rows(it)
        b = [bs[r, :] for r in rows]
        b_last = [bi[CHUNK - 1:CHUNK, :] for bi in b]
        inc = [_dot_tn(v_at(r), k_at(r) * jnp.exp(bl - bi)) for r, bl, bi in zip(rows, b_last, b)]
        for c, bl, ic in zip(cs, b_last, inc):
            st_s[c] = state_t.astype(BF16)
            state_t = jnp.exp(bl) * state_t + ic
        return state_t

    lax.fori_loop(0, n_chunks // GLA_GROUP, states, jnp.zeros((dv, dk), F32))

    def outputs(it, carry):
        cs, rows = chunk_rows(it)
        q = [q_at(r) for r in rows]
        b = [bs[r, :] for r in rows]
        b_mid = [bi[CHUNK // 2 - 1:CHUNK // 2, :] for bi in b]
        att = [jnp.where(incl, _dot_nt(qi * jnp.exp(bi - bm), k_at(r) * jnp.exp(bm - bi)), 0.0)
               for qi, bi, bm, r in zip(q, b, b_mid, rows)]
        o_inter = [_dot_nt(qi * jnp.exp(bi), st_s[c]) for qi, bi, c in zip(q, b, cs)]
        o = [_dot(ai, v_at(r)) + oi for ai, r, oi in zip(att, rows, o_inter)]
        for r, oi in zip(rows, o):
            o_ref[0, r, :] = (_rms(oi, nw) * _silu(gate_ref[0, r, :])).astype(o_ref.dtype)
        return carry

    lax.fori_loop(0, n_chunks // GLA_GROUP, outputs, 0)


def _gla_kernel(q_ref, k_ref, v_ref, r_ref, lr_ref, w2_ref, gb_ref, nw_ref, o_ref, bs, st_s):
    gate = _dot(lr_ref[0], w2_ref[...]) + gb_ref[...]
    log_a = -_softplus(-gate) / GLA_TAU
    bs[...] = _chunk_cumsum(log_a)
    _gla_scan(lambda r: q_ref[0, r, :] * GLA_HEAD_K ** -0.5, lambda r: k_ref[0, r, :], lambda r: v_ref[0, r, :],
              bs, r_ref, nw_ref, o_ref, st_s)


def _hgrn2_kernel(q_ref, f_ref, i_ref, g_ref, lbl_ref, nw_ref, o_ref, ks, bs, st_s, *, layer):
    logits = lbl_ref[...]
    e = jnp.exp(logits - jnp.max(logits, axis=0, keepdims=True))
    p = e / jnp.sum(e, axis=0, keepdims=True)
    lb_first = p[0:1, :]
    lb_layer = lb_first
    for i in range(1, layer + 1):
        lb_layer = lb_layer + p[i:i + 1, :]
    lb = lb_layer - lb_first
    z_f = f_ref[0]
    e_z = jnp.exp(-jnp.abs(z_f))
    big = 1.0 / (1.0 + e_z)
    small = e_z * big
    sig_pos = jnp.where(z_f >= 0.0, big, small)
    sig_neg = jnp.where(z_f >= 0.0, small, big)
    ks[...] = (1.0 - lb) * sig_neg
    bs[...] = _chunk_cumsum(jnp.log(lb + (1.0 - lb) * sig_pos))
    _gla_scan(lambda r: q_ref[0, r, :], lambda r: ks[r, :], lambda r: i_ref[0, r, :],
              bs, g_ref, nw_ref, o_ref, st_s)


def _gla(main, small, w2, gate_b, norm_w):
    bsz, seq, _ = main.shape
    assert seq % (CHUNK * GLA_GROUP) == 0
    dk, dv = GLA_HEAD_K, GLA_HEAD_V
    w2p = jnp.concatenate([w2, jnp.zeros((LANES - GLA_GATE_RANK, GLA_KEY_WIDTH), F32)], axis=0).astype(BF16)
    kspec = lambda off: pl.BlockSpec((1, seq, dk), lambda b, h: (b, 0, off + h))
    vspec = lambda off: pl.BlockSpec((1, seq, dv), lambda b, h: (b, 0, off + h))
    return pl.pallas_call(
        _gla_kernel,
        grid=(bsz, GLA_HEADS),
        in_specs=[kspec(0), kspec(GLA_HEADS), vspec(GLA_HEADS), vspec(2 * GLA_HEADS),
                  pl.BlockSpec((1, seq, LANES), lambda b, h: (b, 0, 0)),
                  pl.BlockSpec((LANES, dk), lambda b, h: (0, h)),
                  pl.BlockSpec((1, dk), lambda b, h: (0, h)),
                  pl.BlockSpec((1, dv), lambda b, h: (0, 0))],
        out_specs=pl.BlockSpec((1, seq, dv), lambda b, h: (b, 0, h)),
        out_shape=jax.ShapeDtypeStruct((bsz, seq, GLA_WIDTH), BF16),
        scratch_shapes=[pltpu.VMEM((seq, dk), F32), pltpu.VMEM((seq // CHUNK, dv, dk), BF16)],
        compiler_params=_params(("parallel", "arbitrary")),
        name="gla",
    )(main, main, main, main, small, w2p, gate_b.reshape(1, GLA_KEY_WIDTH), norm_w.reshape(1, dv))


def _hgrn2(main, lb_logits, layer, norm_w):
    bsz, seq, _ = main.shape
    assert seq % (CHUNK * GLA_GROUP) == 0
    d = HG_EXPAND
    base = (2 * GLA_KEY_WIDTH + 2 * GLA_WIDTH) // d
    col = lambda off: pl.BlockSpec((1, seq, d), lambda b, h: (b, 0, base + off + h))
    depth = lb_logits.shape[0]
    return pl.pallas_call(
        functools.partial(_hgrn2_kernel, layer=layer),
        grid=(bsz, HG_HEADS),
        in_specs=[col(0), col(HG_HEADS), col(2 * HG_HEADS), col(3 * HG_HEADS),
                  pl.BlockSpec((depth, d), lambda b, h: (0, h)),
                  pl.BlockSpec((1, d), lambda b, h: (0, 0))],
        out_specs=pl.BlockSpec((1, seq, d), lambda b, h: (b, 0, h)),
        out_shape=jax.ShapeDtypeStruct((bsz, seq, HG_WIDTH), BF16),
        scratch_shapes=[pltpu.VMEM((seq, d), F32)] * 2 + [pltpu.VMEM((seq // CHUNK, d, d), BF16)],
        compiler_params=_params(("parallel", "arbitrary")),
        name="hgrn2",
    )(main, main, main, main, lb_logits, norm_w.reshape(1, d))


def _split_w_in_kernel(w_ref, main_ref, small_ref, *, start, width):
    w = w_ref[...]
    main_ref[...] = jnp.concatenate([w[:, :start], w[:, start + width:]], axis=1).astype(BF16)
    pad = jnp.zeros((w.shape[0], LANES - width), F32)
    small_ref[...] = jnp.concatenate([w[:, start:start + width], pad], axis=1).astype(BF16)


def _split_w_in(w_in, start, width, *, row_tiles=16):
    d, n = w_in.shape
    rows = d // row_tiles
    assert rows * row_tiles == d and rows % BF16_ROWS == 0 and (n - width) % LANES == 0 and width <= LANES
    return pl.pallas_call(
        functools.partial(_split_w_in_kernel, start=start, width=width),
        grid=(row_tiles,),
        in_specs=[pl.BlockSpec((rows, n), lambda i: (i, 0))],
        out_specs=[pl.BlockSpec((rows, n - width), lambda i: (i, 0)),
                   pl.BlockSpec((rows, LANES), lambda i: (i, 0))],
        out_shape=[jax.ShapeDtypeStruct((d, n - width), BF16), jax.ShapeDtypeStruct((d, LANES), BF16)],
        compiler_params=_params(("parallel",)),
        name="split_w_in",
    )(w_in)


def _even_mixer(x2, bsz, seq, mix_norm, w_in, conv_w, a_log, dt_bias, dn_norm_w, s5_a_re, s5_a_im, s5_b_re, s5_b_im,
                s5_c_re, s5_c_im, s5_d, s5_log_step, s5_glu_w, s5_glu_b, w_out):
    n_qkvz = 4 * DN_WIDTH
    n_ab = 2 * DN_HEADS
    w_main, w_small = _split_w_in(w_in, n_qkvz, n_ab)
    main, small = _inproj(x2, mix_norm, w_main, w_small)
    main = main.reshape(bsz, seq, -1)
    small = small.reshape(bsz, seq, LANES)
    y_a = _deltanet(main, small, conv_w, a_log, dt_bias, dn_norm_w)
    y_s5 = _s5(main, n_qkvz // LANES, s5_a_re, s5_a_im, s5_b_re, s5_b_im, s5_c_re, s5_c_im, s5_d, s5_log_step)
    y_b = _glu(y_s5.reshape(bsz * seq, S5_WIDTH), s5_glu_w.astype(BF16), s5_glu_b)
    return _outproj(x2, y_a.reshape(bsz * seq, DN_WIDTH), y_b, w_out.astype(BF16))


def _odd_mixer(x2, bsz, seq, layer, lb_logits, mix_norm, w_in, gate_w2, gate_b, gla_norm_w, hg_norm_w, w_out):
    n_c = 2 * GLA_KEY_WIDTH + 2 * GLA_WIDTH
    w_main, w_small = _split_w_in(w_in, n_c, GLA_GATE_RANK)
    main, small = _inproj(x2, mix_norm, w_main, w_small)
    main = main.reshape(bsz, seq, -1)
    small = small.reshape(bsz, seq, LANES)
    y_c = _gla(main, small, gate_w2, gate_b, gla_norm_w)
    y_d = _hgrn2(main, lb_logits, layer, hg_norm_w)
    return _outproj(x2, y_c.reshape(bsz * seq, GLA_WIDTH), y_d.reshape(bsz * seq, HG_WIDTH), w_out.astype(BF16))


def kernel(x, l0_ffn1_norm, l0_ffn1_w_gate, l0_ffn1_w_up, l0_ffn1_w_down, l0_mix_norm, l0_w_in, l0_dn_conv_w, l0_dn_a_log, l0_dn_dt_bias, l0_dn_norm_w, l0_s5_a_re, l0_s5_a_im, l0_s5_b_re, l0_s5_b_im, l0_s5_c_re, l0_s5_c_im, l0_s5_d, l0_s5_log_step, l0_s5_glu_w, l0_s5_glu_b, l0_w_out, l0_ffn2_norm, l0_ffn2_w_gate, l0_ffn2_w_up, l0_ffn2_w_down, l1_ffn1_norm, l1_ffn1_w_gate, l1_ffn1_w_up, l1_ffn1_w_down, l1_mix_norm, l1_w_in, l1_gla_gate_w2, l1_gla_gate_b, l1_gla_norm_w, l1_hg_norm_w, l1_w_out, l1_ffn2_norm, l1_ffn2_w_gate, l1_ffn2_w_up, l1_ffn2_w_down, hgrn_lb_logits, final_norm):
    bsz, seq, d = x.shape
    x2 = x.reshape(bsz * seq, d)

    w_l0f1 = tuple(w.astype(BF16) for w in (l0_ffn1_w_gate, l0_ffn1_w_up, l0_ffn1_w_down))
    x2, w_l0f2 = _ffn(x2, l0_ffn1_norm, *w_l0f1, final_norm, final=False,
                      cast_next=(l0_ffn2_w_gate, l0_ffn2_w_up, l0_ffn2_w_down))
    x2 = _even_mixer(x2, bsz, seq, l0_mix_norm, l0_w_in, l0_dn_conv_w, l0_dn_a_log, l0_dn_dt_bias, l0_dn_norm_w,
                     l0_s5_a_re, l0_s5_a_im, l0_s5_b_re, l0_s5_b_im, l0_s5_c_re, l0_s5_c_im, l0_s5_d, l0_s5_log_step,
                     l0_s5_glu_w, l0_s5_glu_b, l0_w_out)
    x2, w_l1f1 = _ffn(x2, l0_ffn2_norm, *w_l0f2, final_norm, final=False,
                      cast_next=(l1_ffn1_w_gate, l1_ffn1_w_up, l1_ffn1_w_down))
    x2, w_l1f2 = _ffn(x2, l1_ffn1_norm, *w_l1f1, final_norm, final=False,
                      cast_next=(l1_ffn2_w_gate, l1_ffn2_w_up, l1_ffn2_w_down))
    x2 = _odd_mixer(x2, bsz, seq, 1, hgrn_lb_logits, l1_mix_norm, l1_w_in, l1_gla_gate_w2, l1_gla_gate_b,
                    l1_gla_norm_w, l1_hg_norm_w, l1_w_out)
    x2, _ = _ffn(x2, l1_ffn2_norm, *w_l1f2, final_norm, final=True)
    return x2.reshape(bsz, seq, d)
```

```python
import functools

import jax
import jax.numpy as jnp
from jax import lax
from jax.experimental import pallas as pl
from jax.experimental.pallas import tpu as pltpu

F32 = jnp.float32
BF16 = jnp.bfloat16

D_MODEL = 2048
CHUNK = 64
NORM_EPS = 1e-6
FFN_RES = 0.5
DN_HEADS = 8
DN_HEAD_DIM = 128
DN_WIDTH = DN_HEADS * DN_HEAD_DIM
DN_CONV = 4
S5_WIDTH = D_MODEL - DN_WIDTH
S5_GROUP = 16
S5_GROUPS = S5_WIDTH // S5_GROUP
S5_STATE = 64
GLA_HEADS = 4
GLA_WIDTH = D_MODEL // 2
GLA_KEY_WIDTH = GLA_WIDTH // 2
GLA_HEAD_K = GLA_KEY_WIDTH // GLA_HEADS
GLA_HEAD_V = GLA_WIDTH // GLA_HEADS
GLA_GATE_RANK = 16
GLA_TAU = 16.0
HG_WIDTH = D_MODEL - GLA_WIDTH
HG_EXPAND = 128
HG_HEADS = HG_WIDTH // HG_EXPAND

LANES = 128
SUBLANES = 8
BF16_ROWS = 16
VMEM_LIMIT = 56 * 1024 * 1024
FFN_VMEM_LIMIT = 58 * 1024 * 1024

S5_TILE_GROUPS = LANES // S5_GROUP
S5_TILE_STATE = S5_TILE_GROUPS * S5_STATE
S5_TILES = S5_WIDTH // LANES
S5_SCAN_ROWS = 4 * BF16_ROWS

DN_PAIR = 2
DN_GROUP = 2
DN_ROWS = DN_GROUP * CHUNK
DN_GROUPS_PER_ITER = 16
N_SPLIT = 3
SEL_ROWS = BF16_ROWS
GLA_GROUP = 32


def _params(sem, vmem_limit=VMEM_LIMIT):
    return pltpu.CompilerParams(dimension_semantics=sem, vmem_limit_bytes=vmem_limit)


def _rms(x, w):
    return x * lax.rsqrt(jnp.mean(x * x, axis=-1, keepdims=True) + NORM_EPS) * w


def _silu(x):
    return x * jax.nn.sigmoid(x)


def _softplus(x):
    return jnp.maximum(x, 0.0) + jnp.log1p(jnp.exp(-jnp.abs(x)))


def _dot(a, b):
    return jnp.dot(a.astype(BF16), b.astype(BF16), preferred_element_type=F32)


def _dot_nt(a, b):
    return lax.dot_general(a.astype(BF16), b.astype(BF16), (((1,), (1,)), ((), ())), preferred_element_type=F32)


def _dot_tn(a, b):
    return lax.dot_general(a.astype(BF16), b.astype(BF16), (((0,), (0,)), ((), ())), preferred_element_type=F32)


def _chunk_cumsum(x):
    pos = lax.broadcasted_iota(jnp.int32, x.shape, 0) % CHUNK
    shift = 1
    while shift < CHUNK:
        x = x + jnp.where(pos >= shift, pltpu.roll(x, shift, axis=0), 0.0)
        shift *= 2
    return x


def _tri_masks(n=CHUNK):
    row = lax.broadcasted_iota(jnp.int32, (n, n), 0)
    col = lax.broadcasted_iota(jnp.int32, (n, n), 1)
    same = (row // CHUNK) == (col // CHUNK)
    return same & (row >= col), same & (row > col)


def _ffn_kernel(x_ref, nw_ref, wg_ref, wu_ref, wd_ref, fw_ref, *rest, n_ff_tiles, final, n_cast):
    cast_in = rest[:n_cast]
    o_ref = rest[n_cast]
    cast_out = rest[n_cast + 1:2 * n_cast + 1]
    xn_ref = rest[2 * n_cast + 1]
    j = pl.program_id(1)
    for src, dst in zip(cast_in, cast_out):
        dst[...] = src[...].astype(BF16)

    @pl.when(j == 0)
    def _():
        x = x_ref[...]
        xn_ref[...] = _rms(x, nw_ref[...]).astype(BF16)
        o_ref[...] = x

    xn = xn_ref[...]
    g = jnp.dot(xn, wg_ref[...], preferred_element_type=F32)
    u = jnp.dot(xn, wu_ref[...], preferred_element_type=F32)
    h = (FFN_RES * (_silu(g) * u)).astype(BF16)
    o_ref[...] += jnp.dot(h, wd_ref[...], preferred_element_type=F32)

    if final:
        @pl.when(j == n_ff_tiles - 1)
        def _():
            o_ref[...] = _rms(o_ref[...], fw_ref[...])


def _grid_tiling(rows, cols, n_i, n_j):
    for (r, c, index_map) in (((rows // n_i, cols // n_j, lambda i, j: (i, j))),
                              ((rows // n_j, cols // n_i, lambda i, j: (j, i)))):
        if r % BF16_ROWS == 0 and c % LANES == 0 and r * c * n_i * n_j == rows * cols:
            return pl.BlockSpec((r, c), index_map)
    raise ValueError("no tile-aligned one-tile-per-step cover")


def _ffn(x2, nw, wg, wu, wd, fw, *, final, cast_next=None, tm=1024, tf=512):
    t, d = x2.shape
    f = wg.shape[1]
    n_row, n_ff = t // tm, f // tf
    in_specs = [pl.BlockSpec((tm, d), lambda i, j: (i, 0)),
                pl.BlockSpec((1, d), lambda i, j: (0, 0)),
                pl.BlockSpec((d, tf), lambda i, j: (0, j)),
                pl.BlockSpec((d, tf), lambda i, j: (0, j)),
                pl.BlockSpec((tf, d), lambda i, j: (j, 0)),
                pl.BlockSpec((1, d), lambda i, j: (0, 0))]
    out_specs = [pl.BlockSpec((tm, d), lambda i, j: (i, 0))]
    out_shape = [jax.ShapeDtypeStruct((t, d), F32)]
    operands = [x2, nw.reshape(1, d), wg, wu, wd, fw.reshape(1, d)]
    if cast_next is not None:
        assert all(w.shape == s for w, s in zip(cast_next, ((d, f), (d, f), (f, d))))
        up_spec = _grid_tiling(d, f, n_row, n_ff)
        down_spec = _grid_tiling(f, d, n_row, n_ff)
        in_specs += [up_spec, up_spec, down_spec]
        out_specs += [up_spec, up_spec, down_spec]
        out_shape += [jax.ShapeDtypeStruct(w.shape, BF16) for w in cast_next]
        operands += list(cast_next)
    res = pl.pallas_call(
        functools.partial(_ffn_kernel, n_ff_tiles=n_ff, final=final, n_cast=0 if cast_next is None else 3),
        grid=(n_row, n_ff),
        in_specs=in_specs,
        out_specs=out_specs,
        out_shape=out_shape,
        scratch_shapes=[pltpu.VMEM((tm, d), BF16)],
        compiler_params=_params(("parallel", "arbitrary"), FFN_VMEM_LIMIT),
        name="ffn",
    )(*operands)
    return res[0], tuple(res[1:])


def _inproj_kernel(x_ref, nw_ref, w_ref, ws_ref, o_ref, os_ref, xn_ref):
    @pl.when(pl.program_id(1) == 0)
    def _():
        xn = _rms(x_ref[...], nw_ref[...]).astype(BF16)
        xn_ref[...] = xn
        os_ref[...] = jnp.dot(xn, ws_ref[...], preferred_element_type=F32)

    o_ref[...] = jnp.dot(xn_ref[...], w_ref[...], preferred_element_type=F32)


def _inproj(x2, nw, w_main, w_small, *, tm=1024, tn=1024):
    t, d = x2.shape
    n = w_main.shape[1]
    return pl.pallas_call(
        _inproj_kernel,
        grid=(t // tm, n // tn),
        in_specs=[pl.BlockSpec((tm, d), lambda i, j: (i, 0)),
                  pl.BlockSpec((1, d), lambda i, j: (0, 0)),
                  pl.BlockSpec((d, tn), lambda i, j: (0, j)),
                  pl.BlockSpec((d, LANES), lambda i, j: (0, 0))],
        out_specs=[pl.BlockSpec((tm, tn), lambda i, j: (i, j)),
                   pl.BlockSpec((tm, LANES), lambda i, j: (i, 0))],
        out_shape=[jax.ShapeDtypeStruct((t, n), F32), jax.ShapeDtypeStruct((t, LANES), F32)],
        scratch_shapes=[pltpu.VMEM((tm, d), BF16)],
        compiler_params=_params(("parallel", "arbitrary")),
        name="inproj",
    )(x2, nw.reshape(1, d), w_main, w_small)


def _outproj_kernel(x_ref, ya_ref, yb_ref, wa_ref, wb_ref, o_ref):
    o_ref[...] = (x_ref[...] + jnp.dot(ya_ref[...], wa_ref[...], preferred_element_type=F32)
                  + jnp.dot(yb_ref[...], wb_ref[...], preferred_element_type=F32))


def _outproj(x2, ya, yb, w, *, tm=512):
    t, d = x2.shape
    ka = ya.shape[1]
    kb = yb.shape[1]
    assert ka == kb
    return pl.pallas_call(
        _outproj_kernel,
        grid=(t // tm,),
        in_specs=[pl.BlockSpec((tm, d), lambda i: (i, 0)),
                  pl.BlockSpec((tm, ka), lambda i: (i, 0)),
                  pl.BlockSpec((tm, kb), lambda i: (i, 0)),
                  pl.BlockSpec((ka, d), lambda i: (0, 0)),
                  pl.BlockSpec((kb, d), lambda i: (1, 0))],
        out_specs=pl.BlockSpec((tm, d), lambda i: (i, 0)),
        out_shape=jax.ShapeDtypeStruct((t, d), F32),
        compiler_params=_params(("parallel",)),
        name="outproj",
    )(x2, ya, yb, w, w)


def _causal_conv(x, w, xpad):
    seq = x.shape[0]
    xpad[SUBLANES:, :] = x
    y = x * w[DN_CONV - 1:DN_CONV, :]
    for back in range(1, DN_CONV):
        y = y + xpad[pl.ds(SUBLANES - back, seq), :] * w[DN_CONV - 1 - back:DN_CONV - back, :]
    return y


def _split3(x):
    parts = []
    for _ in range(N_SPLIT):
        p = x.astype(BF16).astype(F32)
        parts.append(p)
        x = x - p
    return parts


def _l2norm(x, scale=None):
    inv = lax.rsqrt(jnp.sum(x * x, axis=-1, keepdims=True) + NORM_EPS)
    return x * (inv if scale is None else inv * scale)


def _deltanet_kernel(q_ref, k_ref, v_ref, z_ref, at_ref, bt_ref, cwq_ref, cwk_ref, cwv_ref,
                     alog1_ref, dtb1_ref, nw_ref, cum_ref, sel_ref, o_ref,
                     xpad, qs, ks, vs, gts, parts_s, qm_s, n_s, op_s, cd_s):
    seq = q_ref.shape[1]
    n_chunks = seq // CHUNK
    d = DN_HEAD_DIM
    incl, strict = _tri_masks(DN_ROWS)
    nw = nw_ref[...]
    sel = sel_ref[...]
    xpad[:SUBLANES, :] = jnp.zeros((SUBLANES, d), F32)

    for hh in range(DN_PAIR):
        cols = slice(hh * d, (hh + 1) * d)
        qs[...] = _l2norm(_silu(_causal_conv(q_ref[0, :, cols], cwq_ref[:, cols], xpad)), DN_HEAD_DIM ** -0.5)
        ks[...] = _l2norm(_silu(_causal_conv(k_ref[0, :, cols], cwk_ref[:, cols], xpad)))
        vs[...] = _silu(_causal_conv(v_ref[0, :, cols], cwv_ref[:, cols], xpad))
        g_t = -jnp.exp(alog1_ref[hh]) * _softplus(at_ref[0, hh] + dtb1_ref[hh])
        gc_t = jnp.dot(g_t, cum_ref[...], preferred_element_type=F32, precision=lax.Precision.HIGHEST)
        gts[...] = gc_t
        for i, part in enumerate(_split3(gc_t) + _split3(jax.nn.sigmoid(bt_ref[0, hh]))):
            parts_s[i] = part

        def prep(it, carry):
            groups = [it * DN_GROUPS_PER_ITER + g for g in range(DN_GROUPS_PER_ITER)]
            rows = [pl.ds(pl.multiple_of(gi * DN_ROWS, DN_ROWS), DN_ROWS) for gi in groups]
            q = [qs[r, :] for r in rows]
            k = [ks[r, :] for r in rows]
            src = [jnp.concatenate([parts_s[i, pl.ds(gi, 1), :] for i in range(2 * N_SPLIT)]
                                   + [jnp.zeros((SEL_ROWS - 2 * N_SPLIT, DN_ROWS), F32)], axis=0) for gi in groups]
            colb = [_dot_tn(s, sel) for s in src]
            gcb = [c[:, :d] for c in colb]
            beta = [c[:, d:] for c in colb]
            decay = [jnp.exp(jnp.where(incl, jnp.concatenate([gc] * (DN_ROWS // d), axis=1) - gts[pl.ds(gi, 1), :],
                                       -jnp.inf)) for gc, gi in zip(gcb, groups)]
            kb = [ki * bi for ki, bi in zip(k, beta)]
            a = [jnp.where(strict, _dot_nt(kbi, ki) * dc, 0.0) for kbi, ki, dc in zip(kb, k, decay)]
            eg = [jnp.exp(gc) for gc in gcb]
            x = [jnp.concatenate([kbi * egi, vs[r, :] * bi], axis=1) for kbi, egi, r, bi in zip(kb, eg, rows, beta)]
            pb = [ai.astype(BF16) for ai in a]
            x = [xi - _dot(pi, xi) for pi, xi in zip(pb, x)]
            for _ in range(CHUNK.bit_length() - 2):
                pb = [jnp.dot(pi, pi, preferred_element_type=F32).astype(BF16) for pi in pb]
                x = [xi + _dot(pi, xi) for pi, xi in zip(pb, x)]
            xb = [xi.astype(BF16) for xi in x]
            qk = [jnp.where(incl, _dot_nt(qi, ki) * dc, 0.0) for qi, ki, dc in zip(q, k, decay)]
            qx = [_dot(qki, xi) for qki, xi in zip(qk, xb)]
            for g in range(DN_GROUPS_PER_ITER):
                last = [gcb[g][(i + 1) * CHUNK - 1:(i + 1) * CHUNK, :] for i in range(DN_GROUP)]
                g_last = jnp.concatenate([jnp.broadcast_to(li, (CHUNK, d)) for li in last], axis=0)
                kd = (k[g] * jnp.exp(g_last - gcb[g])).astype(BF16)
                qp = (q[g] * eg[g] - qx[g][:, :d]).astype(BF16)
                for i in range(DN_GROUP):
                    c = groups[g] * DN_GROUP + i
                    cr = slice(i * CHUNK, (i + 1) * CHUNK)
                    kx = _dot_tn(kd[cr, :], xb[g][cr, :])
                    qm_s[hh, c, :CHUNK, :] = qp[cr, :]
                    qm_s[hh, c, CHUNK:, :] = kx[:, :d].astype(BF16)
                    n_s[hh, c] = kx[:, d:]
                    cd_s[hh, pl.ds(c, 1), :] = jnp.exp(last[i])
                op_s[hh, rows[g], :] = qx[g][:, d:]
            return carry

        lax.fori_loop(0, seq // (DN_ROWS * DN_GROUPS_PER_ITER), prep, 0)

    def emit(c, outs):
        r = pl.ds(pl.multiple_of(c * CHUNK, CHUNK), CHUNK)
        for hh in range(DN_PAIR):
            cols = slice(hh * d, (hh + 1) * d)
            o = outs[hh] + op_s[hh, r, :]
            o_ref[0, r, cols] = (_rms(o, nw) * _silu(z_ref[0, r, cols])).astype(o_ref.dtype)

    def step(c, carry):
        states, outs = carry
        prods = [jnp.dot(qm_s[hh, c], states[hh].astype(BF16), preferred_element_type=F32) for hh in range(DN_PAIR)]
        emit(jnp.maximum(c - 1, 0), outs)
        new = tuple(cd_s[hh, pl.ds(c, 1), :] * states[hh] + n_s[hh, c] - prods[hh][CHUNK:, :]
                    for hh in range(DN_PAIR))
        return new, tuple(p[:CHUNK, :] for p in prods)

    zeros = lambda rows: tuple(jnp.zeros((rows, d), F32) for _ in range(DN_PAIR))
    _, outs = lax.fori_loop(0, n_chunks, step, (zeros(d), zeros(CHUNK)))
    emit(n_chunks - 1, outs)


def _deltanet(main, small, conv_w, a_log, dt_bias, norm_w):
    bsz, seq, _ = main.shape
    n_chunks = seq // CHUNK
    d = DN_HEAD_DIM
    pw = DN_PAIR * d
    n_pairs = DN_HEADS // DN_PAIR
    n_groups = seq // DN_ROWS
    assert seq % (DN_ROWS * DN_GROUPS_PER_ITER) == 0 and DN_ROWS % d == 0
    ab_t = small[:, :, :2 * DN_HEADS].transpose(0, 2, 1).reshape(bsz, 2 * DN_HEADS, n_groups, DN_ROWS)
    cw = conv_w.reshape(DN_CONV, 3 * DN_WIDTH)
    cum = jnp.kron(jnp.eye(DN_GROUP, dtype=F32), jnp.triu(jnp.ones((CHUNK, CHUNK), F32)))
    sel = jnp.kron((jnp.arange(SEL_ROWS)[:, None] // N_SPLIT == jnp.arange(2)[None, :]).astype(BF16),
                   jnp.ones((1, d), BF16))
    col = lambda off: pl.BlockSpec((1, seq, pw), lambda b, h: (b, 0, off + h))
    cwspec = lambda off: pl.BlockSpec((DN_CONV, pw), lambda b, h: (0, off + h))
    tspec = lambda off: pl.BlockSpec((1, DN_PAIR, n_groups, DN_ROWS), lambda b, h: (b, off + h, 0, 0))
    one = pl.BlockSpec((DN_PAIR, 1, 1), lambda b, h: (h, 0, 0))
    return pl.pallas_call(
        _deltanet_kernel,
        grid=(bsz, n_pairs),
        in_specs=[col(0), col(n_pairs), col(2 * n_pairs), col(3 * n_pairs),
                  tspec(0), tspec(n_pairs),
                  cwspec(0), cwspec(n_pairs), cwspec(2 * n_pairs),
                  one, one,
                  pl.BlockSpec((1, d), lambda b, h: (0, 0)),
                  pl.BlockSpec((DN_ROWS, DN_ROWS), lambda b, h: (0, 0)),
                  pl.BlockSpec((SEL_ROWS, 2 * d), lambda b, h: (0, 0))],
        out_specs=pl.BlockSpec((1, seq, pw), lambda b, h: (b, 0, h)),
        out_shape=jax.ShapeDtypeStruct((bsz, seq, DN_WIDTH), BF16),
        scratch_shapes=[pltpu.VMEM((seq + SUBLANES, d), F32)] + [pltpu.VMEM((seq, d), F32)] * 3 + [
                        pltpu.VMEM((n_groups, DN_ROWS), F32),
                        pltpu.VMEM((2 * N_SPLIT, n_groups, DN_ROWS), F32),
                        pltpu.VMEM((DN_PAIR, n_chunks, CHUNK + d, d), BF16),
                        pltpu.VMEM((DN_PAIR, n_chunks, d, d), F32),
                        pltpu.VMEM((DN_PAIR, seq, d), F32),
                        pltpu.VMEM((DN_PAIR, n_chunks, d), F32)],
        compiler_params=_params(("parallel", "arbitrary")),
        name="deltanet",
    )(main, main, main, main, ab_t, ab_t, cw, cw, cw,
      a_log.reshape(DN_HEADS, 1, 1), dt_bias.reshape(DN_HEADS, 1, 1), norm_w.reshape(1, d), cum, sel)


def _gelu_tanh(x):
    return x * (0.5 * (1.0 + jnp.tanh(0.7978845608028654 * (x + 0.044715 * (x * x * x)))))


def _s5_kernel(u_ref, are_ref, aim_ref, ls_ref, bre_ref, bim_ref, cre_ref, cim_ref, d_ref, o_ref,
               ut, xs, xb, yt, state, bbd, cbd, lam):
    ns = S5_TILE_STATE
    bsz, ts, _ = u_ref.shape

    @pl.when(pl.program_id(1) == 0)
    def _():
        a_re = are_ref[0]
        a_im = aim_ref[0]
        dt = jnp.exp(ls_ref[0])
        mag = jnp.exp(a_re * dt)
        l_re = mag * jnp.cos(a_im * dt)
        l_im = mag * jnp.sin(a_im * dt)
        den = a_re * a_re + a_im * a_im
        c_re = ((l_re - 1.0) * a_re + l_im * a_im) / den
        c_im = (l_im * a_re - (l_re - 1.0) * a_im) / den
        b_re = bre_ref[0]
        b_im = bim_ref[0]
        bbd[:, :ns] = (c_re * b_re - c_im * b_im).astype(BF16)
        bbd[:, ns:] = (c_re * b_im + c_im * b_re).astype(BF16)
        cbd[:ns, :] = cre_ref[0].astype(BF16)
        cbd[ns:, :] = (-cim_ref[0]).astype(BF16)
        lam[:, :ns] = jnp.broadcast_to(l_re, (bsz, ns))
        lam[:, ns:] = jnp.broadcast_to(l_im, (bsz, ns))
        state[...] = jnp.zeros_like(state)

    for b in range(bsz):
        ut[pl.ds(b, ts, stride=bsz), :] = u_ref[b]
    xs[...] = jnp.dot(ut[...].astype(BF16), bbd[...], preferred_element_type=F32)
    l_re = lam[:, :ns]
    l_im = lam[:, ns:]

    frames = S5_SCAN_ROWS // bsz

    def step(i, carry):
        x_re, x_im = carry
        r = pl.ds(pl.multiple_of(i * S5_SCAN_ROWS, S5_SCAN_ROWS), S5_SCAN_ROWS)
        bu = xs[r, :]
        res_re, res_im = [], []
        for f in range(frames):
            rows = slice(f * bsz, (f + 1) * bsz)
            x_re, x_im = (l_re * x_re - l_im * x_im + bu[rows, :ns], l_re * x_im + l_im * x_re + bu[rows, ns:])
            res_re.append(x_re)
            res_im.append(x_im)
        xb[r, :ns] = jnp.concatenate(res_re, axis=0).astype(BF16)
        xb[r, ns:] = jnp.concatenate(res_im, axis=0).astype(BF16)
        return x_re, x_im

    x_re, x_im = lax.fori_loop(0, ts // frames, step, (state[:, :ns], state[:, ns:]))
    state[:, :ns] = x_re
    state[:, ns:] = x_im
    y = jnp.dot(xb[...], cbd[...], preferred_element_type=F32) + d_ref[0] * ut[...]
    yt[...] = _gelu_tanh(y)
    for b in range(bsz):
        o_ref[b] = yt[pl.ds(b, ts, stride=bsz), :]


def _s5(main, u_col_block, a_re, a_im, b_re, b_im, c_re, c_im, d, log_step, *, ts=512):
    bsz, seq, _ = main.shape
    assert bsz == SUBLANES
    tg, ns, nt = S5_TILE_GROUPS, S5_TILE_STATE, S5_TILES
    eye = jnp.eye(tg, dtype=F32)

    def expand_b(b):
        bt = b.reshape(nt, tg, S5_STATE, S5_GROUP)
        return jnp.einsum('ngph,gk->nghkp', bt, eye).reshape(nt, LANES, ns)

    def expand_c(c):
        ct = c.reshape(nt, tg, S5_GROUP, S5_STATE)
        return jnp.einsum('nghp,gk->ngpkh', ct, eye).reshape(nt, ns, LANES)

    chan = lambda a: a.reshape(nt, 1, ns)
    ls = jnp.broadcast_to(log_step[:, None], (S5_GROUPS, S5_STATE))
    pspec = pl.BlockSpec((1, 1, ns), lambda c, t: (c, 0, 0))
    bspec = pl.BlockSpec((1, LANES, ns), lambda c, t: (c, 0, 0))
    cspec = pl.BlockSpec((1, ns, LANES), lambda c, t: (c, 0, 0))
    return pl.pallas_call(
        _s5_kernel,
        grid=(nt, seq // ts),
        in_specs=[pl.BlockSpec((bsz, ts, LANES), lambda c, t: (0, t, u_col_block + c)),
                  pspec, pspec, pspec, bspec, bspec, cspec, cspec,
                  pl.BlockSpec((1, 1, LANES), lambda c, t: (c, 0, 0))],
        out_specs=pl.BlockSpec((bsz, ts, LANES), lambda c, t: (0, t, c)),
        out_shape=jax.ShapeDtypeStruct((bsz, seq, S5_WIDTH), F32),
        scratch_shapes=[pltpu.VMEM((bsz * ts, LANES), F32),
                        pltpu.VMEM((bsz * ts, 2 * ns), F32),
                        pltpu.VMEM((bsz * ts, 2 * ns), BF16),
                        pltpu.VMEM((bsz * ts, LANES), F32),
                        pltpu.VMEM((bsz, 2 * ns), F32),
                        pltpu.VMEM((LANES, 2 * ns), BF16),
                        pltpu.VMEM((2 * ns, LANES), BF16),
                        pltpu.VMEM((bsz, 2 * ns), F32)],
        compiler_params=_params(("parallel", "arbitrary")),
        name="s5",
    )(main, chan(a_re), chan(a_im), chan(ls), expand_b(b_re), expand_b(b_im), expand_c(c_re), expand_c(c_im),
      d.reshape(nt, 1, LANES))


def _glu_kernel(y_ref, w_ref, b_ref, o_ref):
    y = y_ref[...]
    gate = jnp.dot(y.astype(BF16), w_ref[...], preferred_element_type=F32) + b_ref[...]
    o_ref[...] = (y * jax.nn.sigmoid(gate)).astype(o_ref.dtype)


def _glu(y2, w, b, *, tm=1024):
    t, n = y2.shape
    return pl.pallas_call(
        _glu_kernel,
        grid=(t // tm,),
        in_specs=[pl.BlockSpec((tm, n), lambda i: (i, 0)),
                  pl.BlockSpec((n, n), lambda i: (0, 0)),
                  pl.BlockSpec((1, n), lambda i: (0, 0))],
        out_specs=pl.BlockSpec((tm, n), lambda i: (i, 0)),
        out_shape=jax.ShapeDtypeStruct((t, n), BF16),
        compiler_params=_params(("parallel",)),
        name="s5_glu",
    )(y2, w, b.reshape(1, n))


def _gla_scan(q_at, k_at, v_at, bs, gate_ref, nw_ref, o_ref, st_s):
    n_chunks, dv, dk = st_s.shape
    incl, _ = _tri_masks()
    nw = nw_ref[...]

    def chunk_rows(it):
        cs = [it * GLA_GROUP + i for i in range(GLA_GROUP)]
        return cs, [pl.ds(pl.multiple_of(c * CHUNK, CHUNK), CHUNK) for c in cs]

    def states(it, state_t):
        cs, rows = chunk_rows(it)
        b = [bs[r, :] for r in rows]
        b_last = [bi[CHUNK - 1:CHUNK, :] for bi in b]
        inc = [_dot_tn(v_at(r), k_at(r) * jnp.exp(bl - bi)) for r, bl, bi in zip(rows, b_last, b)]
        for c, bl, ic in zip(cs, b_last, inc):
            st_s[c] = state_t.astype(BF16)
            state_t = jnp.exp(bl) * state_t + ic
        return state_t

    lax.fori_loop(0, n_chunks // GLA_GROUP, states, jnp.zeros((dv, dk), F32))

    def outputs(it, carry):
        cs, rows = chunk_rows(it)
        q = [q_at(r) for r in rows]
        b = [bs[r, :] for r in rows]
        b_mid = [bi[CHUNK // 2 - 1:CHUNK // 2, :] for bi in b]
        att = [jnp.where(incl, _dot_nt(qi * jnp.exp(bi - bm), k_at(r) * jnp.exp(bm - bi)), 0.0)
               for qi, bi, bm, r in zip(q, b, b_mid, rows)]
        o_inter = [_dot_nt(qi * jnp.exp(bi), st_s[c]) for qi, bi, c in zip(q, b, cs)]
        o = [_dot(ai, v_at(r)) + oi for ai, r, oi in zip(att, rows, o_inter)]
        for r, oi in zip(rows, o):
            o_ref[0, r, :] = (_rms(oi, nw) * _silu(gate_ref[0, r, :])).astype(o_ref.dtype)
        return carry

    lax.fori_loop(0, n_chunks // GLA_GROUP, outputs, 0)


def _gla_kernel(q_ref, k_ref, v_ref, r_ref, lr_ref, w2_ref, gb_ref, nw_ref, o_ref, bs, st_s):
    gate = _dot(lr_ref[0], w2_ref[...]) + gb_ref[...]
    log_a = -_softplus(-gate) / GLA_TAU
    bs[...] = _chunk_cumsum(log_a)
    _gla_scan(lambda r: q_ref[0, r, :] * GLA_HEAD_K ** -0.5, lambda r: k_ref[0, r, :], lambda r: v_ref[0, r, :],
              bs, r_ref, nw_ref, o_ref, st_s)


def _hgrn2_kernel(q_ref, f_ref, i_ref, g_ref, lbl_ref, nw_ref, o_ref, ks, bs, st_s, *, layer):
    logits = lbl_ref[...]
    e = jnp.exp(logits - jnp.max(logits, axis=0, keepdims=True))
    p = e / jnp.sum(e, axis=0, keepdims=True)
    lb_first = p[0:1, :]
    lb_layer = lb_first
    for i in range(1, layer + 1):
        lb_layer = lb_layer + p[i:i + 1, :]
    lb = lb_layer - lb_first
    z_f = f_ref[0]
    e_z = jnp.exp(-jnp.abs(z_f))
    big = 1.0 / (1.0 + e_z)
    small = e_z * big
    sig_pos = jnp.where(z_f >= 0.0, big, small)
    sig_neg = jnp.where(z_f >= 0.0, small, big)
    ks[...] = (1.0 - lb) * sig_neg
    bs[...] = _chunk_cumsum(jnp.log(lb + (1.0 - lb) * sig_pos))
    _gla_scan(lambda r: q_ref[0, r, :], lambda r: ks[r, :], lambda r: i_ref[0, r, :],
              bs, g_ref, nw_ref, o_ref, st_s)


def _gla(main, small, w2, gate_b, norm_w):
    bsz, seq, _ = main.shape
    assert seq % (CHUNK * GLA_GROUP) == 0
    dk, dv = GLA_HEAD_K, GLA_HEAD_V
    w2p = jnp.concatenate([w2, jnp.zeros((LANES - GLA_GATE_RANK, GLA_KEY_WIDTH), F32)], axis=0).astype(BF16)
    kspec = lambda off: pl.BlockSpec((1, seq, dk), lambda b, h: (b, 0, off + h))
    vspec = lambda off: pl.BlockSpec((1, seq, dv), lambda b, h: (b, 0, off + h))
    return pl.pallas_call(
        _gla_kernel,
        grid=(bsz, GLA_HEADS),
        in_specs=[kspec(0), kspec(GLA_HEADS), vspec(GLA_HEADS), vspec(2 * GLA_HEADS),
                  pl.BlockSpec((1, seq, LANES), lambda b, h: (b, 0, 0)),
                  pl.BlockSpec((LANES, dk), lambda b, h: (0, h)),
                  pl.BlockSpec((1, dk), lambda b, h: (0, h)),
                  pl.BlockSpec((1, dv), lambda b, h: (0, 0))],
        out_specs=pl.BlockSpec((1, seq, dv), lambda b, h: (b, 0, h)),
        out_shape=jax.ShapeDtypeStruct((bsz, seq, GLA_WIDTH), BF16),
        scratch_shapes=[pltpu.VMEM((seq, dk), F32), pltpu.VMEM((seq // CHUNK, dv, dk), BF16)],
        compiler_params=_params(("parallel", "arbitrary")),
        name="gla",
    )(main, main, main, main, small, w2p, gate_b.reshape(1, GLA_KEY_WIDTH), norm_w.reshape(1, dv))


def _hgrn2(main, lb_logits, layer, norm_w):
    bsz, seq, _ = main.shape
    assert seq % (CHUNK * GLA_GROUP) == 0
    d = HG_EXPAND
    base = (2 * GLA_KEY_WIDTH + 2 * GLA_WIDTH) // d
    col = lambda off: pl.BlockSpec((1, seq, d), lambda b, h: (b, 0, base + off + h))
    depth = lb_logits.shape[0]
    return pl.pallas_call(
        functools.partial(_hgrn2_kernel, layer=layer),
        grid=(bsz, HG_HEADS),
        in_specs=[col(0), col(HG_HEADS), col(2 * HG_HEADS), col(3 * HG_HEADS),
                  pl.BlockSpec((depth, d), lambda b, h: (0, h)),
                  pl.BlockSpec((1, d), lambda b, h: (0, 0))],
        out_specs=pl.BlockSpec((1, seq, d), lambda b, h: (b, 0, h)),
        out_shape=jax.ShapeDtypeStruct((bsz, seq, HG_WIDTH), BF16),
        scratch_shapes=[pltpu.VMEM((seq, d), F32)] * 2 + [pltpu.VMEM((seq // CHUNK, d, d), BF16)],
        compiler_params=_params(("parallel", "arbitrary")),
        name="hgrn2",
    )(main, main, main, main, lb_logits, norm_w.reshape(1, d))


def _split_w_in_kernel(w_ref, main_ref, small_ref, *, start, width):
    w = w_ref[...]
    main_ref[...] = jnp.concatenate([w[:, :start], w[:, start + width:]], axis=1).astype(BF16)
    pad = jnp.zeros((w.shape[0], LANES - width), F32)
    small_ref[...] = jnp.concatenate([w[:, start:start + width], pad], axis=1).astype(BF16)


def _split_w_in(w_in, start, width, *, row_tiles=16):
    d, n = w_in.shape
    rows = d // row_tiles
    assert rows * row_tiles == d and rows % BF16_ROWS == 0 and (n - width) % LANES == 0 and width <= LANES
    return pl.pallas_call(
        functools.partial(_split_w_in_kernel, start=start, width=width),
        grid=(row_tiles,),
        in_specs=[pl.BlockSpec((rows, n), lambda i: (i, 0))],
        out_specs=[pl.BlockSpec((rows, n - width), lambda i: (i, 0)),
                   pl.BlockSpec((rows, LANES), lambda i: (i, 0))],
        out_shape=[jax.ShapeDtypeStruct((d, n - width), BF16), jax.ShapeDtypeStruct((d, LANES), BF16)],
        compiler_params=_params(("parallel",)),
        name="split_w_in",
    )(w_in)


def _even_mixer(x2, bsz, seq, mix_norm, w_in, conv_w, a_log, dt_bias, dn_norm_w, s5_a_re, s5_a_im, s5_b_re, s5_b_im,
                s5_c_re, s5_c_im, s5_d, s5_log_step, s5_glu_w, s5_glu_b, w_out):
    n_qkvz = 4 * DN_WIDTH
    n_ab = 2 * DN_HEADS
    w_main, w_small = _split_w_in(w_in, n_qkvz, n_ab)
    main, small = _inproj(x2, mix_norm, w_main, w_small)
    main = main.reshape(bsz, seq, -1)
    small = small.reshape(bsz, seq, LANES)
    y_a = _deltanet(main, small, conv_w, a_log, dt_bias, dn_norm_w)
    y_s5 = _s5(main, n_qkvz // LANES, s5_a_re, s5_a_im, s5_b_re, s5_b_im, s5_c_re, s5_c_im, s5_d, s5_log_step)
    y_b = _glu(y_s5.reshape(bsz * seq, S5_WIDTH), s5_glu_w.astype(BF16), s5_glu_b)
    return _outproj(x2, y_a.reshape(bsz * seq, DN_WIDTH), y_b, w_out.astype(BF16))


def _odd_mixer(x2, bsz, seq, layer, lb_logits, mix_norm, w_in, gate_w2, gate_b, gla_norm_w, hg_norm_w, w_out):
    n_c = 2 * GLA_KEY_WIDTH + 2 * GLA_WIDTH
    w_main, w_small = _split_w_in(w_in, n_c, GLA_GATE_RANK)
    main, small = _inproj(x2, mix_norm, w_main, w_small)
    main = main.reshape(bsz, seq, -1)
    small = small.reshape(bsz, seq, LANES)
    y_c = _gla(main, small, gate_w2, gate_b, gla_norm_w)
    y_d = _hgrn2(main, lb_logits, layer, hg_norm_w)
    return _outproj(x2, y_c.reshape(bsz * seq, GLA_WIDTH), y_d.reshape(bsz * seq, HG_WIDTH), w_out.astype(BF16))


def kernel(x, l0_ffn1_norm, l0_ffn1_w_gate, l0_ffn1_w_up, l0_ffn1_w_down, l0_mix_norm, l0_w_in, l0_dn_conv_w, l0_dn_a_log, l0_dn_dt_bias, l0_dn_norm_w, l0_s5_a_re, l0_s5_a_im, l0_s5_b_re, l0_s5_b_im, l0_s5_c_re, l0_s5_c_im, l0_s5_d, l0_s5_log_step, l0_s5_glu_w, l0_s5_glu_b, l0_w_out, l0_ffn2_norm, l0_ffn2_w_gate, l0_ffn2_w_up, l0_ffn2_w_down, l1_ffn1_norm, l1_ffn1_w_gate, l1_ffn1_w_up, l1_ffn1_w_down, l1_mix_norm, l1_w_in, l1_gla_gate_w2, l1_gla_gate_b, l1_gla_norm_w, l1_hg_norm_w, l1_w_out, l1_ffn2_norm, l1_ffn2_w_gate, l1_ffn2_w_up, l1_ffn2_w_down, hgrn_lb_logits, final_norm):
    bsz, seq, d = x.shape
    x2 = x.reshape(bsz * seq, d)

    w_l0f1 = tuple(w.astype(BF16) for w in (l0_ffn1_w_gate, l0_ffn1_w_up, l0_ffn1_w_down))
    x2, w_l0f2 = _ffn(x2, l0_ffn1_norm, *w_l0f1, final_norm, final=False,
                      cast_next=(l0_ffn2_w_gate, l0_ffn2_w_up, l0_ffn2_w_down))
    x2 = _even_mixer(x2, bsz, seq, l0_mix_norm, l0_w_in, l0_dn_conv_w, l0_dn_a_log, l0_dn_dt_bias, l0_dn_norm_w,
                     l0_s5_a_re, l0_s5_a_im, l0_s5_b_re, l0_s5_b_im, l0_s5_c_re, l0_s5_c_im, l0_s5_d, l0_s5_log_step,
                     l0_s5_glu_w, l0_s5_glu_b, l0_w_out)
    x2, w_l1f1 = _ffn(x2, l0_ffn2_norm, *w_l0f2, final_norm, final=False,
                      cast_next=(l1_ffn1_w_gate, l1_ffn1_w_up, l1_ffn1_w_down))
    x2, w_l1f2 = _ffn(x2, l1_ffn1_norm, *w_l1f1, final_norm, final=False,
                      cast_next=(l1_ffn2_w_gate, l1_ffn2_w_up, l1_ffn2_w_down))
    x2 = _odd_mixer(x2, bsz, seq, 1, hgrn_lb_logits, l1_mix_norm, l1_w_in, l1_gla_gate_w2, l1_gla_gate_b,
                    l1_gla_norm_w, l1_hg_norm_w, l1_w_out)
    x2, _ = _ffn(x2, l1_ffn2_norm, *w_l1f2, final_norm, final=True)
    return x2.reshape(bsz, seq, d)
```

```python
import functools

import jax
import jax.numpy as jnp
from jax import lax
from jax.experimental import pallas as pl
from jax.experimental.pallas import tpu as pltpu

F32 = jnp.float32
BF16 = jnp.bfloat16

D_MODEL = 2048
CHUNK = 64
NORM_EPS = 1e-6
FFN_RES = 0.5
DN_HEADS = 8
DN_HEAD_DIM = 128
DN_WIDTH = DN_HEADS * DN_HEAD_DIM
DN_CONV = 4
S5_WIDTH = D_MODEL - DN_WIDTH
S5_GROUP = 16
S5_GROUPS = S5_WIDTH // S5_GROUP
S5_STATE = 64
GLA_HEADS = 4
GLA_WIDTH = D_MODEL // 2
GLA_KEY_WIDTH = GLA_WIDTH // 2
GLA_HEAD_K = GLA_KEY_WIDTH // GLA_HEADS
GLA_HEAD_V = GLA_WIDTH // GLA_HEADS
GLA_GATE_RANK = 16
GLA_TAU = 16.0
HG_WIDTH = D_MODEL - GLA_WIDTH
HG_EXPAND = 128
HG_HEADS = HG_WIDTH // HG_EXPAND

LANES = 128
SUBLANES = 8
BF16_ROWS = 16
VMEM_LIMIT = 56 * 1024 * 1024
FFN_VMEM_LIMIT = 58 * 1024 * 1024

S5_TILE_GROUPS = LANES // S5_GROUP
S5_TILE_STATE = S5_TILE_GROUPS * S5_STATE
S5_TILES = S5_WIDTH // LANES
S5_SCAN_ROWS = 4 * BF16_ROWS

DN_PAIR = 2
DN_GROUP = 2
DN_ROWS = DN_GROUP * CHUNK
DN_GROUPS_PER_ITER = 16
N_SPLIT = 3
SEL_ROWS = BF16_ROWS
GLA_GROUP = 32


def _params(sem, vmem_limit=VMEM_LIMIT):
    return pltpu.CompilerParams(dimension_semantics=sem, vmem_limit_bytes=vmem_limit)


def _rms(x, w):
    return x * lax.rsqrt(jnp.mean(x * x, axis=-1, keepdims=True) + NORM_EPS) * w


def _silu(x):
    return x * jax.nn.sigmoid(x)


def _softplus(x):
    return jnp.maximum(x, 0.0) + jnp.log1p(jnp.exp(-jnp.abs(x)))


def _dot(a, b):
    return jnp.dot(a.astype(BF16), b.astype(BF16), preferred_element_type=F32)


def _dot_nt(a, b):
    return lax.dot_general(a.astype(BF16), b.astype(BF16), (((1,), (1,)), ((), ())), preferred_element_type=F32)


def _dot_tn(a, b):
    return lax.dot_general(a.astype(BF16), b.astype(BF16), (((0,), (0,)), ((), ())), preferred_element_type=F32)


def _chunk_cumsum(x):
    pos = lax.broadcasted_iota(jnp.int32, x.shape, 0) % CHUNK
    shift = 1
    while shift < CHUNK:
        x = x + jnp.where(pos >= shift, pltpu.roll(x, shift, axis=0), 0.0)
        shift *= 2
    return x


def _tri_masks(n=CHUNK):
    row = lax.broadcasted_iota(jnp.int32, (n, n), 0)
    col = lax.broadcasted_iota(jnp.int32, (n, n), 1)
    same = (row // CHUNK) == (col // CHUNK)
    return same & (row >= col), same & (row > col)


def _ffn_kernel(x_ref, nw_ref, wg_ref, wu_ref, wd_ref, fw_ref, *rest, n_ff_tiles, final, n_cast):
    cast_in = rest[:n_cast]
    o_ref = rest[n_cast]
    cast_out = rest[n_cast + 1:2 * n_cast + 1]
    xn_ref = rest[2 * n_cast + 1]
    j = pl.program_id(1)
    for src, dst in zip(cast_in, cast_out):
        dst[...] = src[...].astype(BF16)

    @pl.when(j == 0)
    def _():
        x = x_ref[...]
        xn_ref[...] = _rms(x, nw_ref[...]).astype(BF16)
        o_ref[...] = x

    xn = xn_ref[...]
    g = jnp.dot(xn, wg_ref[...], preferred_element_type=F32)
    u = jnp.dot(xn, wu_ref[...], preferred_element_type=F32)
    h = (FFN_RES * (_silu(g) * u)).astype(BF16)
    o_ref[...] += jnp.dot(h, wd_ref[...], preferred_element_type=F32)

    if final:
        @pl.when(j == n_ff_tiles - 1)
        def _():
            o_ref[...] = _rms(o_ref[...], fw_ref[...])


def _grid_tiling(rows, cols, n_i, n_j):
    for (r, c, index_map) in (((rows // n_i, cols // n_j, lambda i, j: (i, j))),
                              ((rows // n_j, cols // n_i, lambda i, j: (j, i)))):
        if r % BF16_ROWS == 0 and c % LANES == 0 and r * c * n_i * n_j == rows * cols:
            return pl.BlockSpec((r, c), index_map)
    raise ValueError("no tile-aligned one-tile-per-step cover")


def _ffn(x2, nw, wg, wu, wd, fw, *, final, cast_next=None, tm=1024, tf=512):
    t, d = x2.shape
    f = wg.shape[1]
    n_row, n_ff = t // tm, f // tf
    in_specs = [pl.BlockSpec((tm, d), lambda i, j: (i, 0)),
                pl.BlockSpec((1, d), lambda i, j: (0, 0)),
                pl.BlockSpec((d, tf), lambda i, j: (0, j)),
                pl.BlockSpec((d, tf), lambda i, j: (0, j)),
                pl.BlockSpec((tf, d), lambda i, j: (j, 0)),
                pl.BlockSpec((1, d), lambda i, j: (0, 0))]
    out_specs = [pl.BlockSpec((tm, d), lambda i, j: (i, 0))]
    out_shape = [jax.ShapeDtypeStruct((t, d), F32)]
    operands = [x2, nw.reshape(1, d), wg, wu, wd, fw.reshape(1, d)]
    if cast_next is not None:
        assert all(w.shape == s for w, s in zip(cast_next, ((d, f), (d, f), (f, d))))
        up_spec = _grid_tiling(d, f, n_row, n_ff)
        down_spec = _grid_tiling(f, d, n_row, n_ff)
        in_specs += [up_spec, up_spec, down_spec]
        out_specs += [up_spec, up_spec, down_spec]
        out_shape += [jax.ShapeDtypeStruct(w.shape, BF16) for w in cast_next]
        operands += list(cast_next)
    res = pl.pallas_call(
        functools.partial(_ffn_kernel, n_ff_tiles=n_ff, final=final, n_cast=0 if cast_next is None else 3),
        grid=(n_row, n_ff),
        in_specs=in_specs,
        out_specs=out_specs,
        out_shape=out_shape,
        scratch_shapes=[pltpu.VMEM((tm, d), BF16)],
        compiler_params=_params(("parallel", "arbitrary"), FFN_VMEM_LIMIT),
        name="ffn",
    )(*operands)
    return res[0], tuple(res[1:])


def _inproj_kernel(x_ref, nw_ref, w_ref, ws_ref, o_ref, os_ref, xn_ref):
    @pl.when(pl.program_id(1) == 0)
    def _():
        xn = _rms(x_ref[...], nw_ref[...]).astype(BF16)
        xn_ref[...] = xn
        os_ref[...] = jnp.dot(xn, ws_ref[...], preferred_element_type=F32)

    o_ref[...] = jnp.dot(xn_ref[...], w_ref[...], preferred_element_type=F32)


def _inproj(x2, nw, w_main, w_small, *, tm=1024, col_steps=4):
    t, d = x2.shape
    n = w_main.shape[1]
    tn = n // col_steps
    assert tn * col_steps == n and tn % (2 * LANES) == 0
    return pl.pallas_call(
        _inproj_kernel,
        grid=(t // tm, n // tn),
        in_specs=[pl.BlockSpec((tm, d), lambda i, j: (i, 0)),
                  pl.BlockSpec((1, d), lambda i, j: (0, 0)),
                  pl.BlockSpec((d, tn), lambda i, j: (0, j)),
                  pl.BlockSpec((d, LANES), lambda i, j: (0, 0))],
        out_specs=[pl.BlockSpec((tm, tn), lambda i, j: (i, j)),
                   pl.BlockSpec((tm, LANES), lambda i, j: (i, 0))],
        out_shape=[jax.ShapeDtypeStruct((t, n), F32), jax.ShapeDtypeStruct((t, LANES), F32)],
        scratch_shapes=[pltpu.VMEM((tm, d), BF16)],
        compiler_params=_params(("parallel", "arbitrary")),
        name="inproj",
    )(x2, nw.reshape(1, d), w_main, w_small)


def _outproj_kernel(x_ref, ya_ref, yb_ref, wa_ref, wb_ref, o_ref):
    o_ref[...] = (x_ref[...] + jnp.dot(ya_ref[...], wa_ref[...], preferred_element_type=F32)
                  + jnp.dot(yb_ref[...], wb_ref[...], preferred_element_type=F32))


def _outproj(x2, ya, yb, w, *, tm=512):
    t, d = x2.shape
    ka = ya.shape[1]
    kb = yb.shape[1]
    assert ka == kb
    return pl.pallas_call(
        _outproj_kernel,
        grid=(t // tm,),
        in_specs=[pl.BlockSpec((tm, d), lambda i: (i, 0)),
                  pl.BlockSpec((tm, ka), lambda i: (i, 0)),
                  pl.BlockSpec((tm, kb), lambda i: (i, 0)),
                  pl.BlockSpec((ka, d), lambda i: (0, 0)),
                  pl.BlockSpec((kb, d), lambda i: (1, 0))],
        out_specs=pl.BlockSpec((tm, d), lambda i: (i, 0)),
        out_shape=jax.ShapeDtypeStruct((t, d), F32),
        compiler_params=_params(("parallel",)),
        name="outproj",
    )(x2, ya, yb, w, w)


def _causal_conv(x, w, xpad):
    seq = x.shape[0]
    xpad[SUBLANES:, :] = x
    y = x * w[DN_CONV - 1:DN_CONV, :]
    for back in range(1, DN_CONV):
        y = y + xpad[pl.ds(SUBLANES - back, seq), :] * w[DN_CONV - 1 - back:DN_CONV - back, :]
    return y


def _split3(x):
    parts = []
    for _ in range(N_SPLIT):
        p = x.astype(BF16).astype(F32)
        parts.append(p)
        x = x - p
    return parts


def _l2norm(x, scale=None):
    inv = lax.rsqrt(jnp.sum(x * x, axis=-1, keepdims=True) + NORM_EPS)
    return x * (inv if scale is None else inv * scale)


def _deltanet_kernel(q_ref, k_ref, v_ref, z_ref, at_ref, bt_ref, cwq_ref, cwk_ref, cwv_ref,
                     alog1_ref, dtb1_ref, nw_ref, cum_ref, sel_ref, o_ref,
                     xpad, qs, ks, vs, gts, parts_s, qm_s, n_s, op_s, cd_s):
    seq = q_ref.shape[1]
    n_chunks = seq // CHUNK
    d = DN_HEAD_DIM
    incl, strict = _tri_masks(DN_ROWS)
    nw = nw_ref[...]
    sel = sel_ref[...]
    xpad[:SUBLANES, :] = jnp.zeros((SUBLANES, d), F32)

    for hh in range(DN_PAIR):
        cols = slice(hh * d, (hh + 1) * d)
        qs[...] = _l2norm(_silu(_causal_conv(q_ref[0, :, cols], cwq_ref[:, cols], xpad)), DN_HEAD_DIM ** -0.5)
        ks[...] = _l2norm(_silu(_causal_conv(k_ref[0, :, cols], cwk_ref[:, cols], xpad)))
        vs[...] = _silu(_causal_conv(v_ref[0, :, cols], cwv_ref[:, cols], xpad))
        g_t = -jnp.exp(alog1_ref[hh]) * _softplus(at_ref[0, hh] + dtb1_ref[hh])
        gc_t = jnp.dot(g_t, cum_ref[...], preferred_element_type=F32, precision=lax.Precision.HIGHEST)
        gts[...] = gc_t
        for i, part in enumerate(_split3(gc_t) + _split3(jax.nn.sigmoid(bt_ref[0, hh]))):
            parts_s[i] = part

        def prep(it, carry):
            groups = [it * DN_GROUPS_PER_ITER + g for g in range(DN_GROUPS_PER_ITER)]
            rows = [pl.ds(pl.multiple_of(gi * DN_ROWS, DN_ROWS), DN_ROWS) for gi in groups]
            q = [qs[r, :] for r in rows]
            k = [ks[r, :] for r in rows]
            src = [jnp.concatenate([parts_s[i, pl.ds(gi, 1), :] for i in range(2 * N_SPLIT)]
                                   + [jnp.zeros((SEL_ROWS - 2 * N_SPLIT, DN_ROWS), F32)], axis=0) for gi in groups]
            colb = [_dot_tn(s, sel) for s in src]
            gcb = [c[:, :d] for c in colb]
            beta = [c[:, d:] for c in colb]
            decay = [jnp.exp(jnp.where(incl, jnp.concatenate([gc] * (DN_ROWS // d), axis=1) - gts[pl.ds(gi, 1), :],
                                       -jnp.inf)) for gc, gi in zip(gcb, groups)]
            kb = [ki * bi for ki, bi in zip(k, beta)]
            a = [jnp.where(strict, _dot_nt(kbi, ki) * dc, 0.0) for kbi, ki, dc in zip(kb, k, decay)]
            eg = [jnp.exp(gc) for gc in gcb]
            x = [jnp.concatenate([kbi * egi, vs[r, :] * bi], axis=1) for kbi, egi, r, bi in zip(kb, eg, rows, beta)]
            pb = [ai.astype(BF16) for ai in a]
            x = [xi - _dot(pi, xi) for pi, xi in zip(pb, x)]
            for _ in range(CHUNK.bit_length() - 2):
                pb = [jnp.dot(pi, pi, preferred_element_type=F32).astype(BF16) for pi in pb]
                x = [xi + _dot(pi, xi) for pi, xi in zip(pb, x)]
            xb = [xi.astype(BF16) for xi in x]
            qk = [jnp.where(incl, _dot_nt(qi, ki) * dc, 0.0) for qi, ki, dc in zip(q, k, decay)]
            qx = [_dot(qki, xi) for qki, xi in zip(qk, xb)]
            for g in range(DN_GROUPS_PER_ITER):
                last = [gcb[g][(i + 1) * CHUNK - 1:(i + 1) * CHUNK, :] for i in range(DN_GROUP)]
                g_last = jnp.concatenate([jnp.broadcast_to(li, (CHUNK, d)) for li in last], axis=0)
                kd = (k[g] * jnp.exp(g_last - gcb[g])).astype(BF16)
                qp = (q[g] * eg[g] - qx[g][:, :d]).astype(BF16)
                for i in range(DN_GROUP):
                    c = groups[g] * DN_GROUP + i
                    cr = slice(i * CHUNK, (i + 1) * CHUNK)
                    kx = _dot_tn(kd[cr, :], xb[g][cr, :])
                    qm_s[hh, c, :CHUNK, :] = qp[cr, :]
                    qm_s[hh, c, CHUNK:, :] = kx[:, :d].astype(BF16)
                    n_s[hh, c] = kx[:, d:]
                    cd_s[hh, pl.ds(c, 1), :] = jnp.exp(last[i])
                op_s[hh, rows[g], :] = qx[g][:, d:]
            return carry

        lax.fori_loop(0, seq // (DN_ROWS * DN_GROUPS_PER_ITER), prep, 0)

    def emit(c, outs):
        r = pl.ds(pl.multiple_of(c * CHUNK, CHUNK), CHUNK)
        for hh in range(DN_PAIR):
            cols = slice(hh * d, (hh + 1) * d)
            o = outs[hh] + op_s[hh, r, :]
            o_ref[0, r, cols] = (_rms(o, nw) * _silu(z_ref[0, r, cols])).astype(o_ref.dtype)

    def step(c, carry):
        states, outs = carry
        prods = [jnp.dot(qm_s[hh, c], states[hh].astype(BF16), preferred_element_type=F32) for hh in range(DN_PAIR)]
        emit(jnp.maximum(c - 1, 0), outs)
        new = tuple(cd_s[hh, pl.ds(c, 1), :] * states[hh] + n_s[hh, c] - prods[hh][CHUNK:, :]
                    for hh in range(DN_PAIR))
        return new, tuple(p[:CHUNK, :] for p in prods)

    zeros = lambda rows: tuple(jnp.zeros((rows, d), F32) for _ in range(DN_PAIR))
    _, outs = lax.fori_loop(0, n_chunks, step, (zeros(d), zeros(CHUNK)))
    emit(n_chunks - 1, outs)


def _deltanet(main, small, conv_w, a_log, dt_bias, norm_w):
    bsz, seq, _ = main.shape
    n_chunks = seq // CHUNK
    d = DN_HEAD_DIM
    pw = DN_PAIR * d
    n_pairs = DN_HEADS // DN_PAIR
    n_groups = seq // DN_ROWS
    assert seq % (DN_ROWS * DN_GROUPS_PER_ITER) == 0 and DN_ROWS % d == 0
    ab_t = small[:, :, :2 * DN_HEADS].transpose(0, 2, 1).reshape(bsz, 2 * DN_HEADS, n_groups, DN_ROWS)
    cw = conv_w.reshape(DN_CONV, 3 * DN_WIDTH)
    cum = jnp.kron(jnp.eye(DN_GROUP, dtype=F32), jnp.triu(jnp.ones((CHUNK, CHUNK), F32)))
    sel = jnp.kron((jnp.arange(SEL_ROWS)[:, None] // N_SPLIT == jnp.arange(2)[None, :]).astype(BF16),
                   jnp.ones((1, d), BF16))
    col = lambda off: pl.BlockSpec((1, seq, pw), lambda b, h: (b, 0, off + h))
    cwspec = lambda off: pl.BlockSpec((DN_CONV, pw), lambda b, h: (0, off + h))
    tspec = lambda off: pl.BlockSpec((1, DN_PAIR, n_groups, DN_ROWS), lambda b, h: (b, off + h, 0, 0))
    one = pl.BlockSpec((DN_PAIR, 1, 1), lambda b, h: (h, 0, 0))
    return pl.pallas_call(
        _deltanet_kernel,
        grid=(bsz, n_pairs),
        in_specs=[col(0), col(n_pairs), col(2 * n_pairs), col(3 * n_pairs),
                  tspec(0), tspec(n_pairs),
                  cwspec(0), cwspec(n_pairs), cwspec(2 * n_pairs),
                  one, one,
                  pl.BlockSpec((1, d), lambda b, h: (0, 0)),
                  pl.BlockSpec((DN_ROWS, DN_ROWS), lambda b, h: (0, 0)),
                  pl.BlockSpec((SEL_ROWS, 2 * d), lambda b, h: (0, 0))],
        out_specs=pl.BlockSpec((1, seq, pw), lambda b, h: (b, 0, h)),
        out_shape=jax.ShapeDtypeStruct((bsz, seq, DN_WIDTH), BF16),
        scratch_shapes=[pltpu.VMEM((seq + SUBLANES, d), F32)] + [pltpu.VMEM((seq, d), F32)] * 3 + [
                        pltpu.VMEM((n_groups, DN_ROWS), F32),
                        pltpu.VMEM((2 * N_SPLIT, n_groups, DN_ROWS), F32),
                        pltpu.VMEM((DN_PAIR, n_chunks, CHUNK + d, d), BF16),
                        pltpu.VMEM((DN_PAIR, n_chunks, d, d), F32),
                        pltpu.VMEM((DN_PAIR, seq, d), F32),
                        pltpu.VMEM((DN_PAIR, n_chunks, d), F32)],
        compiler_params=_params(("parallel", "arbitrary")),
        name="deltanet",
    )(main, main, main, main, ab_t, ab_t, cw, cw, cw,
      a_log.reshape(DN_HEADS, 1, 1), dt_bias.reshape(DN_HEADS, 1, 1), norm_w.reshape(1, d), cum, sel)


def _gelu_tanh(x):
    return x * (0.5 * (1.0 + jnp.tanh(0.7978845608028654 * (x + 0.044715 * (x * x * x)))))


def _s5_kernel(u_ref, are_ref, aim_ref, ls_ref, bre_ref, bim_ref, cre_ref, cim_ref, d_ref, o_ref,
               ut, xs, xb, yt, state, bbd, cbd, lam):
    ns = S5_TILE_STATE
    bsz, ts, _ = u_ref.shape

    @pl.when(pl.program_id(1) == 0)
    def _():
        a_re = are_ref[0]
        a_im = aim_ref[0]
        dt = jnp.exp(ls_ref[0])
        mag = jnp.exp(a_re * dt)
        l_re = mag * jnp.cos(a_im * dt)
        l_im = mag * jnp.sin(a_im * dt)
        den = a_re * a_re + a_im * a_im
        c_re = ((l_re - 1.0) * a_re + l_im * a_im) / den
        c_im = (l_im * a_re - (l_re - 1.0) * a_im) / den
        b_re = bre_ref[0]
        b_im = bim_ref[0]
        bbd[:, :ns] = (c_re * b_re - c_im * b_im).astype(BF16)
        bbd[:, ns:] = (c_re * b_im + c_im * b_re).astype(BF16)
        cbd[:ns, :] = cre_ref[0].astype(BF16)
        cbd[ns:, :] = (-cim_ref[0]).astype(BF16)
        lam[:, :ns] = jnp.broadcast_to(l_re, (bsz, ns))
        lam[:, ns:] = jnp.broadcast_to(l_im, (bsz, ns))
        state[...] = jnp.zeros_like(state)

    for b in range(bsz):
        ut[pl.ds(b, ts, stride=bsz), :] = u_ref[b]
    xs[...] = jnp.dot(ut[...].astype(BF16), bbd[...], preferred_element_type=F32)
    l_re = lam[:, :ns]
    l_im = lam[:, ns:]

    frames = S5_SCAN_ROWS // bsz

    def step(i, carry):
        x_re, x_im = carry
        r = pl.ds(pl.multiple_of(i * S5_SCAN_ROWS, S5_SCAN_ROWS), S5_SCAN_ROWS)
        bu = xs[r, :]
        res_re, res_im = [], []
        for f in range(frames):
            rows = slice(f * bsz, (f + 1) * bsz)
            x_re, x_im = (l_re * x_re - l_im * x_im + bu[rows, :ns], l_re * x_im + l_im * x_re + bu[rows, ns:])
            res_re.append(x_re)
            res_im.append(x_im)
        xb[r, :ns] = jnp.concatenate(res_re, axis=0).astype(BF16)
        xb[r, ns:] = jnp.concatenate(res_im, axis=0).astype(BF16)
        return x_re, x_im

    x_re, x_im = lax.fori_loop(0, ts // frames, step, (state[:, :ns], state[:, ns:]))
    state[:, :ns] = x_re
    state[:, ns:] = x_im
    y = jnp.dot(xb[...], cbd[...], preferred_element_type=F32) + d_ref[0] * ut[...]
    yt[...] = _gelu_tanh(y)
    for b in range(bsz):
        o_ref[b] = yt[pl.ds(b, ts, stride=bsz), :]


def _s5(main, u_col_block, a_re, a_im, b_re, b_im, c_re, c_im, d, log_step, *, ts=512):
    bsz, seq, _ = main.shape
    assert bsz == SUBLANES
    tg, ns, nt = S5_TILE_GROUPS, S5_TILE_STATE, S5_TILES
    eye = jnp.eye(tg, dtype=F32)

    def expand_b(b):
        bt = b.reshape(nt, tg, S5_STATE, S5_GROUP)
        return jnp.einsum('ngph,gk->nghkp', bt, eye).reshape(nt, LANES, ns)

    def expand_c(c):
        ct = c.reshape(nt, tg, S5_GROUP, S5_STATE)
        return jnp.einsum('nghp,gk->ngpkh', ct, eye).reshape(nt, ns, LANES)

    chan = lambda a: a.reshape(nt, 1, ns)
    ls = jnp.broadcast_to(log_step[:, None], (S5_GROUPS, S5_STATE))
    pspec = pl.BlockSpec((1, 1, ns), lambda c, t: (c, 0, 0))
    bspec = pl.BlockSpec((1, LANES, ns), lambda c, t: (c, 0, 0))
    cspec = pl.BlockSpec((1, ns, LANES), lambda c, t: (c, 0, 0))
    return pl.pallas_call(
        _s5_kernel,
        grid=(nt, seq // ts),
        in_specs=[pl.BlockSpec((bsz, ts, LANES), lambda c, t: (0, t, u_col_block + c)),
                  pspec, pspec, pspec, bspec, bspec, cspec, cspec,
                  pl.BlockSpec((1, 1, LANES), lambda c, t: (c, 0, 0))],
        out_specs=pl.BlockSpec((bsz, ts, LANES), lambda c, t: (0, t, c)),
        out_shape=jax.ShapeDtypeStruct((bsz, seq, S5_WIDTH), F32),
        scratch_shapes=[pltpu.VMEM((bsz * ts, LANES), F32),
                        pltpu.VMEM((bsz * ts, 2 * ns), F32),
                        pltpu.VMEM((bsz * ts, 2 * ns), BF16),
                        pltpu.VMEM((bsz * ts, LANES), F32),
                        pltpu.VMEM((bsz, 2 * ns), F32),
                        pltpu.VMEM((LANES, 2 * ns), BF16),
                        pltpu.VMEM((2 * ns, LANES), BF16),
                        pltpu.VMEM((bsz, 2 * ns), F32)],
        compiler_params=_params(("parallel", "arbitrary")),
        name="s5",
    )(main, chan(a_re), chan(a_im), chan(ls), expand_b(b_re), expand_b(b_im), expand_c(c_re), expand_c(c_im),
      d.reshape(nt, 1, LANES))


def _glu_kernel(y_ref, w_ref, b_ref, o_ref):
    y = y_ref[...]
    gate = jnp.dot(y.astype(BF16), w_ref[...], preferred_element_type=F32) + b_ref[...]
    o_ref[...] = (y * jax.nn.sigmoid(gate)).astype(o_ref.dtype)


def _glu(y2, w, b, *, tm=1024):
    t, n = y2.shape
    return pl.pallas_call(
        _glu_kernel,
        grid=(t // tm,),
        in_specs=[pl.BlockSpec((tm, n), lambda i: (i, 0)),
                  pl.BlockSpec((n, n), lambda i: (0, 0)),
                  pl.BlockSpec((1, n), lambda i: (0, 0))],
        out_specs=pl.BlockSpec((tm, n), lambda i: (i, 0)),
        out_shape=jax.ShapeDtypeStruct((t, n), BF16),
        compiler_params=_params(("parallel",)),
        name="s5_glu",
    )(y2, w, b.reshape(1, n))


def _gla_scan(q_at, k_at, v_at, bs, gate_ref, nw_ref, o_ref, st_s):
    n_chunks, dv, dk = st_s.shape
    incl, _ = _tri_masks()
    nw = nw_ref[...]

    def chunk_rows(it):
        cs = [it * GLA_GROUP + i for i in range(GLA_GROUP)]
        return cs, [pl.ds(pl.multiple_of(c * CHUNK, CHUNK), CHUNK) for c in cs]

    def states(it, state_t):
        cs, rows = chunk_rows(it)
        b = [bs[r, :] for r in rows]
        b_last = [bi[CHUNK - 1:CHUNK, :] for bi in b]
        inc = [_dot_tn(v_at(r), k_at(r) * jnp.exp(bl - bi)) for r, bl, bi in zip(rows, b_last, b)]
        for c, bl, ic in zip(cs, b_last, inc):
            st_s[c] = state_t.astype(BF16)
            state_t = jnp.exp(bl) * state_t + ic
        return state_t

    lax.fori_loop(0, n_chunks // GLA_GROUP, states, jnp.zeros((dv, dk), F32))

    def outputs(it, carry):
        cs, rows = chunk_rows(it)
        q = [q_at(r) for r in rows]
        b = [bs[r, :] for r in rows]
        b_mid = [bi[CHUNK // 2 - 1:CHUNK // 2, :] for bi in b]
        att = [jnp.where(incl, _dot_nt(qi * jnp.exp(bi - bm), k_at(r) * jnp.exp(bm - bi)), 0.0)
               for qi, bi, bm, r in zip(q, b, b_mid, rows)]
        o_inter = [_dot_nt(qi * jnp.exp(bi), st_s[c]) for qi, bi, c in zip(q, b, cs)]
        o = [_dot(ai, v_at(r)) + oi for ai, r, oi in zip(att, rows, o_inter)]
        for r, oi in zip(rows, o):
            o_ref[0, r, :] = (_rms(oi, nw) * _silu(gate_ref[0, r, :])).astype(o_ref.dtype)
        return carry

    lax.fori_loop(0, n_chunks // GLA_GROUP, outputs, 0)


def _gla_kernel(q_ref, k_ref, v_ref, r_ref, lr_ref, w2_ref, gb_ref, nw_ref, o_ref, bs, st_s):
    gate = _dot(lr_ref[0], w2_ref[...]) + gb_ref[...]
    log_a = -_softplus(-gate) / GLA_TAU
    bs[...] = _chunk_cumsum(log_a)
    _gla_scan(lambda r: q_ref[0, r, :] * GLA_HEAD_K ** -0.5, lambda r: k_ref[0, r, :], lambda r: v_ref[0, r, :],
              bs, r_ref, nw_ref, o_ref, st_s)


def _hgrn2_kernel(q_ref, f_ref, i_ref, g_ref, lbl_ref, nw_ref, o_ref, ks, bs, st_s, *, layer):
    logits = lbl_ref[...]
    e = jnp.exp(logits - jnp.max(logits, axis=0, keepdims=True))
    p = e / jnp.sum(e, axis=0, keepdims=True)
    lb_first = p[0:1, :]
    lb_layer = lb_first
    for i in range(1, layer + 1):
        lb_layer = lb_layer + p[i:i + 1, :]
    lb = lb_layer - lb_first
    z_f = f_ref[0]
    e_z = jnp.exp(-jnp.abs(z_f))
    big = 1.0 / (1.0 + e_z)
    small = e_z * big
    sig_pos = jnp.where(z_f >= 0.0, big, small)
    sig_neg = jnp.where(z_f >= 0.0, small, big)
    ks[...] = (1.0 - lb) * sig_neg
    bs[...] = _chunk_cumsum(jnp.log(lb + (1.0 - lb) * sig_pos))
    _gla_scan(lambda r: q_ref[0, r, :], lambda r: ks[r, :], lambda r: i_ref[0, r, :],
              bs, g_ref, nw_ref, o_ref, st_s)


def _gla(main, small, w2, gate_b, norm_w):
    bsz, seq, _ = main.shape
    assert seq % (CHUNK * GLA_GROUP) == 0
    dk, dv = GLA_HEAD_K, GLA_HEAD_V
    w2p = jnp.concatenate([w2, jnp.zeros((LANES - GLA_GATE_RANK, GLA_KEY_WIDTH), F32)], axis=0).astype(BF16)
    kspec = lambda off: pl.BlockSpec((1, seq, dk), lambda b, h: (b, 0, off + h))
    vspec = lambda off: pl.BlockSpec((1, seq, dv), lambda b, h: (b, 0, off + h))
    return pl.pallas_call(
        _gla_kernel,
        grid=(bsz, GLA_HEADS),
        in_specs=[kspec(0), kspec(GLA_HEADS), vspec(GLA_HEADS), vspec(2 * GLA_HEADS),
                  pl.BlockSpec((1, seq, LANES), lambda b, h: (b, 0, 0)),
                  pl.BlockSpec((LANES, dk), lambda b, h: (0, h)),
                  pl.BlockSpec((1, dk), lambda b, h: (0, h)),
                  pl.BlockSpec((1, dv), lambda b, h: (0, 0))],
        out_specs=pl.BlockSpec((1, seq, dv), lambda b, h: (b, 0, h)),
        out_shape=jax.ShapeDtypeStruct((bsz, seq, GLA_WIDTH), BF16),
        scratch_shapes=[pltpu.VMEM((seq, dk), F32), pltpu.VMEM((seq // CHUNK, dv, dk), BF16)],
        compiler_params=_params(("parallel", "arbitrary")),
        name="gla",
    )(main, main, main, main, small, w2p, gate_b.reshape(1, GLA_KEY_WIDTH), norm_w.reshape(1, dv))


def _hgrn2(main, lb_logits, layer, norm_w):
    bsz, seq, _ = main.shape
    assert seq % (CHUNK * GLA_GROUP) == 0
    d = HG_EXPAND
    base = (2 * GLA_KEY_WIDTH + 2 * GLA_WIDTH) // d
    col = lambda off: pl.BlockSpec((1, seq, d), lambda b, h: (b, 0, base + off + h))
    depth = lb_logits.shape[0]
    return pl.pallas_call(
        functools.partial(_hgrn2_kernel, layer=layer),
        grid=(bsz, HG_HEADS),
        in_specs=[col(0), col(HG_HEADS), col(2 * HG_HEADS), col(3 * HG_HEADS),
                  pl.BlockSpec((depth, d), lambda b, h: (0, h)),
                  pl.BlockSpec((1, d), lambda b, h: (0, 0))],
        out_specs=pl.BlockSpec((1, seq, d), lambda b, h: (b, 0, h)),
        out_shape=jax.ShapeDtypeStruct((bsz, seq, HG_WIDTH), BF16),
        scratch_shapes=[pltpu.VMEM((seq, d), F32)] * 2 + [pltpu.VMEM((seq // CHUNK, d, d), BF16)],
        compiler_params=_params(("parallel", "arbitrary")),
        name="hgrn2",
    )(main, main, main, main, lb_logits, norm_w.reshape(1, d))


def _split_w_in_kernel(w_ref, main_ref, small_ref, *, start, width):
    w = w_ref[...]
    main_ref[...] = jnp.concatenate([w[:, :start], w[:, start + width:]], axis=1).astype(BF16)
    pad = jnp.zeros((w.shape[0], LANES - width), F32)
    small_ref[...] = jnp.concatenate([w[:, start:start + width], pad], axis=1).astype(BF16)


def _split_w_in(w_in, start, width, *, row_tiles=16):
    d, n = w_in.shape
    rows = d // row_tiles
    assert rows * row_tiles == d and rows % BF16_ROWS == 0 and (n - width) % LANES == 0 and width <= LANES
    return pl.pallas_call(
        functools.partial(_split_w_in_kernel, start=start, width=width),
        grid=(row_tiles,),
        in_specs=[pl.BlockSpec((rows, n), lambda i: (i, 0))],
        out_specs=[pl.BlockSpec((rows, n - width), lambda i: (i, 0)),
                   pl.BlockSpec((rows, LANES), lambda i: (i, 0))],
        out_shape=[jax.ShapeDtypeStruct((d, n - width), BF16), jax.ShapeDtypeStruct((d, LANES), BF16)],
        compiler_params=_params(("parallel",)),
        name="split_w_in",
    )(w_in)


def _even_mixer(x2, bsz, seq, mix_norm, w_in, conv_w, a_log, dt_bias, dn_norm_w, s5_a_re, s5_a_im, s5_b_re, s5_b_im,
                s5_c_re, s5_c_im, s5_d, s5_log_step, s5_glu_w, s5_glu_b, w_out):
    n_qkvz = 4 * DN_WIDTH
    n_ab = 2 * DN_HEADS
    w_main, w_small = _split_w_in(w_in, n_qkvz, n_ab)
    main, small = _inproj(x2, mix_norm, w_main, w_small)
    main = main.reshape(bsz, seq, -1)
    small = small.reshape(bsz, seq, LANES)
    y_a = _deltanet(main, small, conv_w, a_log, dt_bias, dn_norm_w)
    y_s5 = _s5(main, n_qkvz // LANES, s5_a_re, s5_a_im, s5_b_re, s5_b_im, s5_c_re, s5_c_im, s5_d, s5_log_step)
    y_b = _glu(y_s5.reshape(bsz * seq, S5_WIDTH), s5_glu_w.astype(BF16), s5_glu_b)
    return _outproj(x2, y_a.reshape(bsz * seq, DN_WIDTH), y_b, w_out.astype(BF16))


def _odd_mixer(x2, bsz, seq, layer, lb_logits, mix_norm, w_in, gate_w2, gate_b, gla_norm_w, hg_norm_w, w_out):
    n_c = 2 * GLA_KEY_WIDTH + 2 * GLA_WIDTH
    w_main, w_small = _split_w_in(w_in, n_c, GLA_GATE_RANK)
    main, small = _inproj(x2, mix_norm, w_main, w_small)
    main = main.reshape(bsz, seq, -1)
    small = small.reshape(bsz, seq, LANES)
    y_c = _gla(main, small, gate_w2, gate_b, gla_norm_w)
    y_d = _hgrn2(main, lb_logits, layer, hg_norm_w)
    return _outproj(x2, y_c.reshape(bsz * seq, GLA_WIDTH), y_d.reshape(bsz * seq, HG_WIDTH), w_out.astype(BF16))


def kernel(x, l0_ffn1_norm, l0_ffn1_w_gate, l0_ffn1_w_up, l0_ffn1_w_down, l0_mix_norm, l0_w_in, l0_dn_conv_w, l0_dn_a_log, l0_dn_dt_bias, l0_dn_norm_w, l0_s5_a_re, l0_s5_a_im, l0_s5_b_re, l0_s5_b_im, l0_s5_c_re, l0_s5_c_im, l0_s5_d, l0_s5_log_step, l0_s5_glu_w, l0_s5_glu_b, l0_w_out, l0_ffn2_norm, l0_ffn2_w_gate, l0_ffn2_w_up, l0_ffn2_w_down, l1_ffn1_norm, l1_ffn1_w_gate, l1_ffn1_w_up, l1_ffn1_w_down, l1_mix_norm, l1_w_in, l1_gla_gate_w2, l1_gla_gate_b, l1_gla_norm_w, l1_hg_norm_w, l1_w_out, l1_ffn2_norm, l1_ffn2_w_gate, l1_ffn2_w_up, l1_ffn2_w_down, hgrn_lb_logits, final_norm):
    bsz, seq, d = x.shape
    x2 = x.reshape(bsz * seq, d)

    w_l0f1 = tuple(w.astype(BF16) for w in (l0_ffn1_w_gate, l0_ffn1_w_up, l0_ffn1_w_down))
    x2, w_l0f2 = _ffn(x2, l0_ffn1_norm, *w_l0f1, final_norm, final=False,
                      cast_next=(l0_ffn2_w_gate, l0_ffn2_w_up, l0_ffn2_w_down))
    x2 = _even_mixer(x2, bsz, seq, l0_mix_norm, l0_w_in, l0_dn_conv_w, l0_dn_a_log, l0_dn_dt_bias, l0_dn_norm_w,
                     l0_s5_a_re, l0_s5_a_im, l0_s5_b_re, l0_s5_b_im, l0_s5_c_re, l0_s5_c_im, l0_s5_d, l0_s5_log_step,
                     l0_s5_glu_w, l0_s5_glu_b, l0_w_out)
    x2, w_l1f1 = _ffn(x2, l0_ffn2_norm, *w_l0f2, final_norm, final=False,
                      cast_next=(l1_ffn1_w_gate, l1_ffn1_w_up, l1_ffn1_w_down))
    x2, w_l1f2 = _ffn(x2, l1_ffn1_norm, *w_l1f1, final_norm, final=False,
                      cast_next=(l1_ffn2_w_gate, l1_ffn2_w_up, l1_ffn2_w_down))
    x2 = _odd_mixer(x2, bsz, seq, 1, hgrn_lb_logits, l1_mix_norm, l1_w_in, l1_gla_gate_w2, l1_gla_gate_b,
                    l1_gla_norm_w, l1_hg_norm_w, l1_w_out)
    x2, _ = _ffn(x2, l1_ffn2_norm, *w_l1f2, final_norm, final=True)
    return x2.reshape(bsz, seq, d)
```
